```python
import jax, jax.numpy as jnp
from jax import lax
import numpy as np

D_MODEL = 2048
BATCH = 2
SEQ = 4096
DEPTH = 4
DEC_BATCH = 8
DEC_SEQ = 4
PAST_LEN = 16384
PAGE_SIZE = 128

F32 = jnp.float32
L_EVEN = (DEPTH + 1) // 2
L_ODD = DEPTH // 2
ALPHA = (2 * DEPTH) ** 0.25
BETA = (8 * DEPTH) ** -0.25
LN_EPS = 1e-5
A_HEADS = 12
A_HEAD_DIM = 64
A_WIDTH = A_HEADS * A_HEAD_DIM
A_LORA_W = 64
A_LORA_A = 64
A_SHIFT_W = 3 * A_WIDTH + A_LORA_W + A_LORA_A
A_GN_EPS = 64e-5
B_CONFIGS = ((128, 1), (512, 4), (2048, 16))
B_GROUPS = 3
B_HEADS_PER_GROUP = 4
B_HEAD_DIM = 64
B_WIDTH = B_GROUPS * B_HEADS_PER_GROUP * B_HEAD_DIM
B_OUT = B_HEADS_PER_GROUP * B_HEAD_DIM
C_HEADS = 6
C_HEAD_DIM = 256
C_WIDTH = C_HEADS * C_HEAD_DIM
C_CHUNK = 128
C_ROT_BASE = 10000.0
C_NORM_EPS = 1e-6
M_TOKENS = 256
M_HEADS = 4
M_HEAD_DIM = 128
M_WIDTH = M_HEADS * M_HEAD_DIM
EVEN_IN = A_SHIFT_W + A_WIDTH + 3 * B_WIDTH + B_OUT + 2 * M_WIDTH
EVEN_OUT = A_WIDTH + B_OUT + M_WIDTH
ODD_IN = 4 * C_WIDTH + 2 * M_WIDTH
ODD_OUT = C_WIDTH + M_WIDTH

kernel_name = 'rwkv7_dilated_retention_hybrid_step'


def _split(h, sizes):
    return jnp.split(h, np.cumsum(sizes)[:-1].tolist(), axis=-1)


def layer_norm(x, g, b):
    xf = x.astype(F32)
    mu = jnp.mean(xf, -1, keepdims=True)
    var = jnp.mean(jnp.square(xf - mu), -1, keepdims=True)
    return (xf - mu) * lax.rsqrt(var + LN_EPS) * g + b


def _softmax_with_lse(s, mask):
    s = jnp.where(mask, s, -jnp.inf)
    m = jnp.max(s, -1, keepdims=True)
    e = jnp.exp(s - m)
    l = jnp.sum(e, -1, keepdims=True)
    return e / l, (m + jnp.log(l))[..., 0]


def rwkv7_mix(hs, s0, w0, w_up, a0, a_up, k_k, k_a, r_k, lnx_g, lnx_b):
    bn, t, _ = hs.shape
    r, k, v, hw, ha = _split(hs, (A_WIDTH, A_WIDTH, A_WIDTH, A_LORA_W, A_LORA_A))
    w_log = -jax.nn.softplus(-(w0 + jnp.tanh(hw) @ w_up).astype(F32)) - 0.5
    decay = jnp.exp(-jnp.exp(w_log))
    a = jax.nn.sigmoid(a0 + ha @ a_up)
    heads = lambda z: z.reshape(bn, t, A_HEADS, A_HEAD_DIM).astype(F32)
    kk = heads(k * k_k)
    kk = kk * lax.rsqrt(jnp.maximum(jnp.sum(kk * kk, -1, keepdims=True), 1e-24))
    k = k * (1.0 + (a - 1.0) * k_a)
    r_h, k_h, v_h, a_h, d_h = heads(r), heads(k), heads(v), heads(a), heads(decay)

    def step(S, inp):
        r_t, d_t, k_t, v_t, kk_t, a_t = inp
        sa = jnp.einsum('bhij,bhj->bhi', S, -kk_t)
        S = S * d_t[:, :, None, :] + sa[..., None] * (kk_t * a_t)[:, :, None, :] + v_t[..., None] * k_t[:, :, None, :]
        return S, jnp.einsum('bhij,bhj->bhi', S, r_t)

    xs = tuple(jnp.swapaxes(z, 0, 1) for z in (r_h, d_h, k_h, v_h, kk, a_h))
    s_new, y = lax.scan(step, s0.astype(F32), xs)
    y = jnp.swapaxes(y, 0, 1)
    mu = jnp.mean(y, -1, keepdims=True)
    var = jnp.mean(jnp.square(y - mu), -1, keepdims=True)
    y = ((y - mu) * lax.rsqrt(var + A_GN_EPS)).reshape(bn, t, A_WIDTH) * lnx_g + lnx_b
    bonus = jnp.sum(r_h * k_h * r_k, -1, keepdims=True) * v_h
    return y + bonus.reshape(bn, t, A_WIDTH), s_new


def dilated_window_prompt(q, k, v, window, dilation):
    bn, s_len, nh, dh = q.shape
    blk = window // dilation
    unit = blk * dilation
    s_pad = -(-s_len // unit) * unit
    n_blk = s_pad // unit

    def fold(z):
        z = jnp.pad(z, ((0, 0), (0, s_pad - s_len), (0, 0), (0, 0)))
        return z.reshape(bn, n_blk, blk, dilation, nh, dh)

    qb, kb, vb = fold(q), fold(k), fold(v)
    prev = lambda z: jnp.concatenate([jnp.zeros_like(z[:, :1]), z[:, :-1]], axis=1)
    kw = jnp.concatenate([prev(kb), kb], axis=2)
    vw = jnp.concatenate([prev(vb), vb], axis=2)
    s = jnp.einsum('bnirhd,bnjrhd->bnrhij', qb, kw).astype(F32) * dh ** -0.5
    i = jnp.arange(blk)[:, None]
    j = jnp.arange(2 * blk)[None, :]
    band = (j >= i) & (j <= i + blk)
    has_prev = (jnp.arange(n_blk) > 0)[:, None, None] | (j >= blk)[None]
    mask = (band[None] & has_prev)[None, :, None, None]
    p, lse = _softmax_with_lse(s, mask)
    o = jnp.einsum('bnrhij,bnjrhd->bnirhd', p.astype(vw.dtype), vw)
    o = o.reshape(bn, s_pad, nh, dh)[:, :s_len]
    lse = jnp.transpose(lse, (0, 1, 4, 2, 3)).reshape(bn, s_pad, nh)[:, :s_len]
    return o, lse


def dilated_window_decode(q, k, v, buf, window, dilation):
    L = buf.shape[1]
    t = q.shape[1]
    dh = q.shape[-1]
    keys = jnp.concatenate([buf[:, :, 0], k.astype(buf.dtype)], axis=1)
    vals = jnp.concatenate([buf[:, :, 1], v.astype(buf.dtype)], axis=1)
    n_keys = window // dilation + 1
    idx = L + jnp.arange(t)[:, None] - dilation * jnp.arange(n_keys)[None, :]
    valid = idx >= 0
    idx = jnp.maximum(idx, 0)
    kg, vg = keys[:, idx], vals[:, idx]
    s = jnp.einsum('nthd,ntjhd->nthj', q, kg).astype(F32) * dh ** -0.5
    p, lse = _softmax_with_lse(s, valid[None, :, None, :])
    o = jnp.einsum('nthj,ntjhd->nthd', p.astype(vg.dtype), vg)
    return o, lse


def combine_by_denominator(outs, lses):
    wts = jax.nn.softmax(jnp.stack(lses, 0), axis=0)
    return jnp.sum(wts[..., None] * jnp.stack(outs, 0).astype(F32), axis=0)


def memory_attention(q, mkv):
    s = jnp.einsum('bthd,bmhd->bhtm', q, mkv[:, :, 0]).astype(F32) * M_HEAD_DIM ** -0.5
    p = jax.nn.softmax(s, axis=-1)
    return jnp.einsum('bhtm,bmhd->bthd', p.astype(mkv.dtype), mkv[:, :, 1])


def retention_rotate(z, pos):
    angle = 1.0 / (C_ROT_BASE ** jnp.linspace(0.0, 1.0, C_HEAD_DIM // 2, dtype=F32))
    ph = pos[:, None] * jnp.repeat(angle, 2)[None]
    sin, cos = jnp.sin(ph)[None, :, None], jnp.cos(ph)[None, :, None]
    rot = jnp.stack([-z[..., 1::2], z[..., 0::2]], axis=-1).reshape(z.shape)
    return z.astype(F32) * cos + rot.astype(F32) * sin


def retention_chunked(q, k, v, r0):
    bn, t, nh, dk = q.shape
    chunk = C_CHUNK if t % C_CHUNK == 0 else t
    n = t // chunk
    lg = jnp.log(1.0 - 2.0 ** (-5.0 - jnp.arange(nh, dtype=F32)))
    idx = jnp.arange(chunk, dtype=F32)
    diff = idx[:, None] - idx[None, :]
    dmat = jnp.where(diff >= 0, jnp.exp(lg[:, None, None] * jnp.maximum(diff, 0.0)), 0.0)
    xi = jnp.exp(lg[None, :] * (idx[:, None] + 1.0))
    zeta = jnp.exp(lg[None, :] * (chunk - 1.0 - idx[:, None]))
    g_chunk = jnp.exp(lg * chunk)
    to_chunks = lambda z: jnp.moveaxis(z.astype(F32).reshape(bn, n, chunk, nh, z.shape[-1]), 1, 0)

    def step(R, inp):
        qc, kc, vc = inp
        sc = jnp.einsum('bihd,bjhd->bhij', qc, kc) * dmat[None]
        o = jnp.einsum('bhij,bjhe->bihe', sc, vc) + jnp.einsum('bihd,bhde->bihe', qc, R) * xi[None, :, :, None]
        R = R * g_chunk[None, :, None, None] + jnp.einsum('bjhd,bjhe->bhde', kc * zeta[None, :, :, None], vc)
        return R, o

    r_new, o = lax.scan(step, r0.astype(F32), (to_chunks(q), to_chunks(k), to_chunks(v)))
    return jnp.moveaxis(o, 0, 1).reshape(bn, t, nh, v.shape[-1]), r_new


def even_layer(x, mkv, dwa_bufs, s0, shift0, w_in, w_out, ln_g, ln_b, mu, w0, w_up, a0, a_up, k_k, k_a, r_k, lnx_g, lnx_b):
    bn, t, _ = x.shape
    h = x @ w_in
    h_sh, gate_a, q_b, k_b, v_b, gate_b, q_m, gate_m = _split(
        h, (A_SHIFT_W, A_WIDTH, B_WIDTH, B_WIDTH, B_WIDTH, B_OUT, M_WIDTH, M_WIDTH))
    prev = jnp.concatenate([shift0[:, None].astype(h_sh.dtype), h_sh[:, :-1]], axis=1)
    y_a, s_new = rwkv7_mix(h_sh + (prev - h_sh) * mu, s0, w0, w_up, a0, a_up, k_k, k_a, r_k, lnx_g, lnx_b)
    grp = lambda z: z.reshape(bn, t, B_GROUPS, B_HEADS_PER_GROUP, B_HEAD_DIM)
    q_b, k_b, v_b = grp(q_b), grp(k_b), grp(v_b)
    outs, lses, rows = [], [], []
    for g, (win, dil) in enumerate(B_CONFIGS):
        qg, kg, vg = q_b[:, :, g], k_b[:, :, g], v_b[:, :, g]
        if dwa_bufs is None:
            o, lse = dilated_window_prompt(qg, kg, vg, win, dil)
            keep = min(win, t)
            rows.append(jnp.stack([kg[:, t - keep:], vg[:, t - keep:]], axis=2))
        else:
            o, lse = dilated_window_decode(qg, kg, vg, dwa_bufs[g], win, dil)
            rows.append(jnp.stack([kg, vg], axis=2))
        outs.append(o)
        lses.append(lse)
    y_b = combine_by_denominator(outs, lses).reshape(bn, t, B_OUT)
    y_m = memory_attention(q_m.reshape(bn, t, M_HEADS, M_HEAD_DIM), mkv).reshape(bn, t, M_WIDTH)
    u = jnp.concatenate([y_a * jax.nn.silu(gate_a), y_b * jax.nn.silu(gate_b), y_m * jax.nn.silu(gate_m)], axis=-1).astype(x.dtype)
    x = layer_norm(ALPHA * x + u @ w_out, ln_g, ln_b)
    return x, s_new, h_sh[:, -1], rows


def odd_layer(x, mkv, r0, pos0, w_in, w_out, ln_g, ln_b):
    bn, t, _ = x.shape
    h = x @ w_in
    q, k, v, g, q_m, g_m = _split(h, (C_WIDTH, C_WIDTH, C_WIDTH, C_WIDTH, M_WIDTH, M_WIDTH))
    heads = lambda z: z.reshape(bn, t, C_HEADS, C_HEAD_DIM)
    pos = jnp.arange(t, dtype=F32) + pos0
    q = retention_rotate(heads(q), pos)
    k = retention_rotate(heads(k), pos) * C_HEAD_DIM ** -0.5
    y, r_new = retention_chunked(q, k, heads(v), r0)
    y = y * lax.rsqrt(jnp.mean(y * y, -1, keepdims=True) + C_NORM_EPS)
    y_m = memory_attention(q_m.reshape(bn, t, M_HEADS, M_HEAD_DIM), mkv).reshape(bn, t, M_WIDTH)
    u = jnp.concatenate([y.reshape(bn, t, C_WIDTH) * jax.nn.silu(g), y_m * jax.nn.silu(g_m)], axis=-1).astype(x.dtype)
    x = layer_norm(ALPHA * x + u @ w_out, ln_g, ln_b)
    return x, r_new


def setup_inputs(seed: int = 0) -> dict:
    key = jax.random.key(seed)
    ks = iter(jax.random.split(key, 40))

    def nrm(shape, scale=1.0):
        return jax.random.normal(next(ks), shape, F32) * scale

    d = D_MODEL
    rows = [min(w, PAST_LEN) for w, _ in B_CONFIGS]
    return {
        'x_prompt': nrm((BATCH, SEQ, d)),
        'x_sample': nrm((DEC_BATCH, DEC_SEQ, d)),
        'state_rwkv': nrm((L_EVEN, DEC_BATCH, A_HEADS, A_HEAD_DIM, A_HEAD_DIM), 0.5),
        'state_rwkv_shift': nrm((L_EVEN, DEC_BATCH, A_SHIFT_W)),
        'cache_dwa_g0': nrm((L_EVEN, DEC_BATCH, rows[0], 2, B_HEADS_PER_GROUP, B_HEAD_DIM)),
        'cache_dwa_g1': nrm((L_EVEN, DEC_BATCH, rows[1], 2, B_HEADS_PER_GROUP, B_HEAD_DIM)),
        'cache_dwa_g2': nrm((L_EVEN, DEC_BATCH, rows[2], 2, B_HEADS_PER_GROUP, B_HEAD_DIM)),
        'state_ret': nrm((L_ODD, DEC_BATCH, C_HEADS, C_HEAD_DIM, C_HEAD_DIM), 0.5),
        'cache_mem_kv': nrm((DEPTH, DEC_BATCH, M_TOKENS, 2, M_HEADS, M_HEAD_DIM)),
        'mem_prompt': nrm((BATCH, M_TOKENS, d)),
        'w_in_even': nrm((L_EVEN, d, EVEN_IN), d ** -0.5),
        'w_out_even': nrm((L_EVEN, EVEN_OUT, d), BETA * EVEN_OUT ** -0.5),
        'w_in_odd': nrm((L_ODD, d, ODD_IN), d ** -0.5),
        'w_out_odd': nrm((L_ODD, ODD_OUT, d), BETA * ODD_OUT ** -0.5),
        'w_mem_kv': nrm((DEPTH, d, 2 * M_WIDTH), d ** -0.5),
        'ln_g': 1.0 + nrm((DEPTH, d), 0.02),
        'ln_b': nrm((DEPTH, d), 0.02),
        'rwkv_mu': jax.random.uniform(next(ks), (L_EVEN, A_SHIFT_W), F32),
        'rwkv_w0': jnp.linspace(-6.0, -1.0, A_WIDTH, dtype=F32)[None] + nrm((L_EVEN, A_WIDTH), 0.1),
        'rwkv_w_up': nrm((L_EVEN, A_LORA_W, A_WIDTH), 0.1),
        'rwkv_a0': nrm((L_EVEN, A_WIDTH), 0.1),
        'rwkv_a_up': nrm((L_EVEN, A_LORA_A, A_WIDTH), A_LORA_A ** -0.5),
        'rwkv_k_k': 0.85 + nrm((L_EVEN, A_WIDTH), 0.02),
        'rwkv_k_a': 1.0 + nrm((L_EVEN, A_WIDTH), 0.02),
        'rwkv_r_k': nrm((L_EVEN, A_HEADS, A_HEAD_DIM), 0.1),
        'rwkv_lnx_g': 1.0 + nrm((L_EVEN, A_WIDTH), 0.02),
        'rwkv_lnx_b': nrm((L_EVEN, A_WIDTH), 0.02),
    }


def reference(x_prompt, x_sample, state_rwkv, state_rwkv_shift, cache_dwa_g0, cache_dwa_g1, cache_dwa_g2, state_ret, cache_mem_kv, mem_prompt, w_in_even, w_out_even, w_in_odd, w_out_odd, w_mem_kv, ln_g, ln_b, rwkv_mu, rwkv_w0, rwkv_w_up, rwkv_a0, rwkv_a_up, rwkv_k_k, rwkv_k_a, rwkv_r_k, rwkv_lnx_g, rwkv_lnx_b):
    xp, xs = x_prompt, x_sample
    bp = xp.shape[0]
    dwa_cache = (cache_dwa_g0, cache_dwa_g1, cache_dwa_g2)
    rwkv_p, rwkv_s, shift_p, shift_s, ret_p, ret_s, mem_p = [], [], [], [], [], [], []
    dwa_p = [[] for _ in B_CONFIGS]
    dwa_s = [[] for _ in B_CONFIGS]
    for l in range(DEPTH):
        mkv_p = (mem_prompt @ w_mem_kv[l]).reshape(bp, M_TOKENS, 2, M_HEADS, M_HEAD_DIM)
        mem_p.append(mkv_p)
        mkv_s = cache_mem_kv[l]
        if l % 2 == 0:
            e = l // 2
            prm = (w_in_even[e], w_out_even[e], ln_g[l], ln_b[l], rwkv_mu[e], rwkv_w0[e], rwkv_w_up[e], rwkv_a0[e],
                   rwkv_a_up[e], rwkv_k_k[e], rwkv_k_a[e], rwkv_r_k[e], rwkv_lnx_g[e], rwkv_lnx_b[e])
            s0 = jnp.zeros((bp, A_HEADS, A_HEAD_DIM, A_HEAD_DIM), F32)
            sh0 = jnp.zeros((bp, A_SHIFT_W), xp.dtype)
            xp, st, sh, rows = even_layer(xp, mkv_p, None, s0, sh0, *prm)
            rwkv_p.append(st)
            shift_p.append(sh)
            for g in range(B_GROUPS):
                dwa_p[g].append(rows[g])
            bufs = tuple(c[e] for c in dwa_cache)
            xs, st, sh, rows = even_layer(xs, mkv_s, bufs, state_rwkv[e], state_rwkv_shift[e], *prm)
            rwkv_s.append(st)
            shift_s.append(sh)
            for g in range(B_GROUPS):
                dwa_s[g].append(rows[g])
        else:
            o = l // 2
            prm = (w_in_odd[o], w_out_odd[o], ln_g[l], ln_b[l])
            r0 = jnp.zeros((bp, C_HEADS, C_HEAD_DIM, C_HEAD_DIM), F32)
            xp, st = odd_layer(xp, mkv_p, r0, 0, *prm)
            ret_p.append(st)
            xs, st = odd_layer(xs, mkv_s, state_ret[o], PAST_LEN, *prm)
            ret_s.append(st)
    y_prompt, y_sample = xp, xs
    new_rwkv_prompt, new_rwkv_sample = jnp.stack(rwkv_p), jnp.stack(rwkv_s)
    new_shift_prompt, new_shift_sample = jnp.stack(shift_p), jnp.stack(shift_s)
    new_dwa_g0_prompt, new_dwa_g0_sample = jnp.stack(dwa_p[0]), jnp.stack(dwa_s[0])
    new_dwa_g1_prompt, new_dwa_g1_sample = jnp.stack(dwa_p[1]), jnp.stack(dwa_s[1])
    new_dwa_g2_prompt, new_dwa_g2_sample = jnp.stack(dwa_p[2]), jnp.stack(dwa_s[2])
    new_ret_prompt, new_ret_sample = jnp.stack(ret_p), jnp.stack(ret_s)
    new_mem_kv_prompt = jnp.stack(mem_p)
    return (y_prompt, y_sample, new_rwkv_prompt, new_rwkv_sample, new_shift_prompt, new_shift_sample, new_dwa_g0_prompt, new_dwa_g0_sample, new_dwa_g1_prompt, new_dwa_g1_sample, new_dwa_g2_prompt, new_dwa_g2_sample, new_ret_prompt, new_ret_sample, new_mem_kv_prompt)
```

```python
import functools

import numpy as np
import jax
import jax.numpy as jnp
from jax import lax
from jax.experimental import pallas as pl
from jax.experimental.pallas import tpu as pltpu

F32 = jnp.float32
BF16 = jnp.bfloat16

D_MODEL = 2048
DEPTH = 4
PAST_LEN = 16384
ALPHA = (2 * DEPTH) ** 0.25
LN_EPS = 1e-5
A_HEADS = 12
A_HEAD_DIM = 64
A_WIDTH = A_HEADS * A_HEAD_DIM
A_LORA = 64
A_SHIFT_W = 3 * A_WIDTH + 2 * A_LORA
A_GN_EPS = 64e-5
B_CONFIGS = ((128, 1), (512, 4), (2048, 16))
B_GROUPS = 3
B_HEADS_PER_GROUP = 4
B_HEAD_DIM = 64
B_OUT = B_HEADS_PER_GROUP * B_HEAD_DIM
B_WIDTH = B_GROUPS * B_OUT
B_BLK = 128
C_HEADS = 6
C_HEAD_DIM = 256
C_WIDTH = C_HEADS * C_HEAD_DIM
C_CHUNK = 128
C_ROT_BASE = 10000.0
C_NORM_EPS = 1e-6
M_TOKENS = 256
M_HEADS = 4
M_HEAD_DIM = 128
M_WIDTH = M_HEADS * M_HEAD_DIM
EVEN_IN = A_SHIFT_W + A_WIDTH + 3 * B_WIDTH + B_OUT + 2 * M_WIDTH
ODD_IN = 4 * C_WIDTH + 2 * M_WIDTH

LANES = 128
SUBLANES = 8
VMEM_LIMIT = 48 * 1024 * 1024

EVEN_IN_PAD = 7168
U_GATE_A = A_SHIFT_W // LANES
U_QB = U_GATE_A + A_WIDTH // LANES
U_KB = U_QB + B_WIDTH // LANES
U_VB = U_KB + B_WIDTH // LANES
U_GATE_B = U_VB + B_WIDTH // LANES
U_QM_EVEN = U_GATE_B + B_OUT // LANES
U_GM_EVEN = U_QM_EVEN + M_WIDTH // LANES
U_QM_ODD = 4 * C_WIDTH // LANES
U_GM_ODD = U_QM_ODD + M_WIDTH // LANES
NU = EVEN_IN_PAD // LANES


def _cparams(sem):
    return pltpu.CompilerParams(dimension_semantics=sem, vmem_limit_bytes=VMEM_LIMIT)


def _sigmoid(z):
    return 1.0 / (1.0 + jnp.exp(-z))


def _silu(z):
    return z * _sigmoid(z)


def _mm_kernel(x_ref, w_ref, o_ref):
    o_ref[...] = jnp.dot(x_ref[...], w_ref[...], preferred_element_type=F32)


def matmul(x, w, tn=512):
    m, k = x.shape
    n = w.shape[1]
    tm = min(m, 1024)
    return pl.pallas_call(
        _mm_kernel,
        grid=(m // tm, n // tn),
        in_specs=[pl.BlockSpec((tm, k), lambda i, j: (i, 0)),
                  pl.BlockSpec((k, tn), lambda i, j: (0, j))],
        out_specs=pl.BlockSpec((tm, tn), lambda i, j: (i, j)),
        out_shape=jax.ShapeDtypeStruct((m, n), F32),
        compiler_params=_cparams(("parallel", "parallel")),
        name="matmul",
    )(x, w)


def _outproj_kernel(n_u, *refs):
    u_refs = refs[:n_u]
    w_refs = refs[n_u:2 * n_u]
    x_ref, g_ref, b_ref, o_ref, obf_ref = refs[2 * n_u:]
    acc = jnp.dot(u_refs[0][...], w_refs[0][...], preferred_element_type=F32)
    for u_ref, w_ref in zip(u_refs[1:], w_refs[1:]):
        acc = acc + jnp.dot(u_ref[...], w_ref[...], preferred_element_type=F32)
    z = ALPHA * x_ref[...] + acc
    mu = jnp.mean(z, axis=-1, keepdims=True)
    zc = z - mu
    var = jnp.mean(zc * zc, axis=-1, keepdims=True)
    y = zc * lax.rsqrt(var + LN_EPS) * g_ref[...] + b_ref[...]
    o_ref[...] = y
    obf_ref[...] = y.astype(BF16)


def outproj_ln(us, w_out, x, g, b):
    m, d = x.shape
    tm = min(m, 256)
    n_u = len(us)
    in_specs = [pl.BlockSpec((tm, u.shape[1]), lambda i: (i, 0)) for u in us]
    row = 0
    for u in us:
        kw = u.shape[1]
        assert row % kw == 0
        in_specs.append(pl.BlockSpec((kw, d), functools.partial(lambda i, r: (r, 0), r=row // kw)))
        row += kw
    assert row == w_out.shape[0]
    in_specs += [pl.BlockSpec((tm, d), lambda i: (i, 0)),
                 pl.BlockSpec((1, d), lambda i: (0, 0)),
                 pl.BlockSpec((1, d), lambda i: (0, 0))]
    return pl.pallas_call(
        functools.partial(_outproj_kernel, n_u),
        grid=(m // tm,),
        in_specs=in_specs,
        out_specs=[pl.BlockSpec((tm, d), lambda i: (i, 0)), pl.BlockSpec((tm, d), lambda i: (i, 0))],
        out_shape=[jax.ShapeDtypeStruct((m, d), F32), jax.ShapeDtypeStruct((m, d), BF16)],
        compiler_params=_cparams(("parallel",)),
        name="outproj_ln",
    )(*us, *([w_out] * n_u), x, g.reshape(1, d), b.reshape(1, d))


def _seg_sum(x, ones_bf):
    hi = x.astype(BF16)
    r1 = x - hi.astype(F32)
    mid = r1.astype(BF16)
    lo = (r1 - mid.astype(F32)).astype(BF16)
    return (jnp.dot(hi, ones_bf, preferred_element_type=F32)
            + jnp.dot(mid, ones_bf, preferred_element_type=F32)
            + jnp.dot(lo, ones_bf, preferred_element_type=F32))


def _block_ones(width, seg):
    i = np.arange(width) // seg
    return jnp.asarray((i[:, None] == i[None, :]).astype(np.float32), dtype=BF16)


def _rwkv_prep_kernel(tt, tp, h_ref, prev_ref, sh0_ref, mu_ref, w0_ref, wup_ref, a0_ref, aup_ref,
                      kk_ref_p, ka_ref_p, rk_ref_p, ones_ref,
                      r_o, d_o, k_o, kk_o, b_o, vt_o, bonus_o, *scratch):
    i = pl.program_id(1)
    x = h_ref[0]
    pr = prev_ref[0]
    prev_last = pr[pr.shape[0] - 1:pr.shape[0], :]
    first = jnp.where(i == 0, sh0_ref[0], prev_last)
    row = lax.broadcasted_iota(jnp.int32, x.shape, 0)
    if tt % SUBLANES == 0:
        rolled = pltpu.roll(x, 1, axis=0)
    else:
        rolled = jnp.concatenate([x[tt - 1:tt], x[:tt - 1]], axis=0)
    prev = jnp.where(row == 0, first, rolled)
    hs = x + (prev - x) * mu_ref[...]
    r = hs[:, 0:A_WIDTH]
    k = hs[:, A_WIDTH:2 * A_WIDTH]
    v = hs[:, 2 * A_WIDTH:3 * A_WIDTH]
    hw = hs[:, 3 * A_WIDTH:3 * A_WIDTH + A_LORA]
    ha = hs[:, 3 * A_WIDTH + A_LORA:A_SHIFT_W]
    zw = w0_ref[...] + jnp.dot(jnp.tanh(hw).astype(BF16), wup_ref[...], preferred_element_type=F32)
    nz = -zw
    softplus = jnp.maximum(nz, 0.0) + jnp.log(1.0 + jnp.exp(-jnp.abs(nz)))
    w_log = -softplus - 0.5
    decay = jnp.exp(-jnp.exp(w_log))
    a = _sigmoid(a0_ref[...] + jnp.dot(ha.astype(BF16), aup_ref[...], preferred_element_type=F32))
    ones_bf = ones_ref[...]
    kk = k * kk_ref_p[...]
    kk = kk * lax.rsqrt(jnp.maximum(_seg_sum(kk * kk, ones_bf), 1e-24))
    k2 = k * (1.0 + (a - 1.0) * ka_ref_p[...])
    bb = kk * a
    bonus_o[0] = _seg_sum(r * k2 * rk_ref_p[...], ones_bf) * v
    for hd in range(A_HEADS):
        sl = slice(hd * A_HEAD_DIM, (hd + 1) * A_HEAD_DIM)
        r_o[0, hd] = r[:, sl]
        d_o[0, hd] = decay[:, sl]
        k_o[0, hd] = k2[:, sl]
        kk_o[0, hd] = kk[:, sl]
        b_o[0, hd] = bb[:, sl]
    for p in range(A_HEADS // 2):
        vp = v[:, p * LANES:(p + 1) * LANES]
        if tt == tp:
            vt_o[0, p] = vp.T
        else:
            pad = scratch[0]
            pad[...] = jnp.zeros_like(pad)
            pad[0:tt, :] = vp
            vt_o[0, p] = pad[...].T


def rwkv_prep(h3, shift0, prm, ones768):
    bsz, t, _ = h3.shape
    tt = min(t, 128)
    tp = max(tt, LANES)
    nt = t // tt
    pr_rows = min(t, SUBLANES)
    pb = tt // pr_rows
    mu, w0, wup, a0, aup, k_k, k_a, r_k = prm
    row_spec = pl.BlockSpec((1, A_WIDTH), lambda b, i: (0, 0))
    lora_spec = pl.BlockSpec((A_LORA, A_WIDTH), lambda b, i: (0, 0))
    head_spec = pl.BlockSpec((1, A_HEADS, tt, A_HEAD_DIM), lambda b, i: (b, 0, i, 0))
    head_shape = jax.ShapeDtypeStruct((bsz, A_HEADS, t, A_HEAD_DIM), F32)
    scratch = [] if tt == tp else [pltpu.VMEM((tp, LANES), F32)]
    return pl.pallas_call(
        functools.partial(_rwkv_prep_kernel, tt, tp),
        grid=(bsz, nt),
        in_specs=[pl.BlockSpec((1, tt, A_SHIFT_W), lambda b, i: (b, i, 0)),
                  pl.BlockSpec((1, pr_rows, A_SHIFT_W), lambda b, i: (b, jnp.maximum(i * pb - 1, 0), 0)),
                  pl.BlockSpec((1, 1, A_SHIFT_W), lambda b, i: (b, 0, 0)),
                  pl.BlockSpec((1, A_SHIFT_W), lambda b, i: (0, 0)),
                  row_spec, lora_spec, row_spec, lora_spec, row_spec, row_spec, row_spec,
                  pl.BlockSpec((A_WIDTH, A_WIDTH), lambda b, i: (0, 0))],
        out_specs=[head_spec] * 5 + [
            pl.BlockSpec((1, A_HEADS // 2, LANES, tp), lambda b, i: (b, 0, 0, i)),
            pl.BlockSpec((1, tt, A_WIDTH), lambda b, i: (b, i, 0))],
        out_shape=[head_shape] * 5 + [
            jax.ShapeDtypeStruct((bsz, A_HEADS // 2, LANES, nt * tp), F32),
            jax.ShapeDtypeStruct((bsz, t, A_WIDTH), F32)],
        scratch_shapes=scratch,
        compiler_params=_cparams(("parallel", "parallel")),
        name="rwkv_prep",
    )(h3, h3, shift0.reshape(bsz, 1, A_SHIFT_W), mu.reshape(1, -1), w0.reshape(1, -1), wup.astype(BF16),
      a0.reshape(1, -1), aup.astype(BF16), k_k.reshape(1, -1), k_a.reshape(1, -1), r_k.reshape(1, -1), ones768)


def _rwkv_scan_kernel(steps, r_ref, d_ref, k_ref, kk_ref, b_ref, vt_ref, s0_ref, yt_ref, sout_ref, s_scr):
    c = pl.program_id(1)

    @pl.when(c == 0)
    def _():
        s_scr[...] = s0_ref[0]

    tp = vt_ref.shape[3]
    lane = lax.broadcasted_iota(jnp.int32, (A_HEAD_DIM, tp), 1)
    yt_ref[...] = jnp.zeros_like(yt_ref)

    def body(t, carry):
        msk = lane == t
        for hd in range(A_HEADS):
            s = s_scr[hd]
            kk = kk_ref[0, hd, pl.ds(t, 1), :]
            sa = -jnp.sum(s * kk, axis=-1, keepdims=True)
            vcol = jnp.sum(jnp.where(msk, vt_ref[0, hd], 0.0), axis=-1, keepdims=True)
            s = (s * d_ref[0, hd, pl.ds(t, 1), :] + sa * b_ref[0, hd, pl.ds(t, 1), :]
                 + vcol * k_ref[0, hd, pl.ds(t, 1), :])
            s_scr[hd] = s
            y = jnp.sum(s * r_ref[0, hd, pl.ds(t, 1), :], axis=-1, keepdims=True)
            yt_ref[0, hd] = jnp.where(msk, y, yt_ref[0, hd])
        return carry

    lax.fori_loop(0, steps, body, 0)

    @pl.when(c == pl.num_programs(1) - 1)
    def _():
        sout_ref[0] = s_scr[...]


def rwkv_scan(ops, vt, s0):
    r, d, k, kk, b = ops
    bsz, _, t, _ = r.shape
    tt = min(t, 128)
    tp = max(tt, LANES)
    nt = t // tt
    vt = vt.reshape(bsz, A_HEADS, A_HEAD_DIM, nt * tp)
    head_spec = pl.BlockSpec((1, A_HEADS, tt, A_HEAD_DIM), lambda b_, i: (b_, 0, i, 0))
    vt_spec = pl.BlockSpec((1, A_HEADS, A_HEAD_DIM, tp), lambda b_, i: (b_, 0, 0, i))
    st_spec = pl.BlockSpec((1, A_HEADS, A_HEAD_DIM, A_HEAD_DIM), lambda b_, i: (b_, 0, 0, 0))
    yt, s_new = pl.pallas_call(
        functools.partial(_rwkv_scan_kernel, tt),
        grid=(bsz, nt),
        in_specs=[head_spec] * 5 + [vt_spec, st_spec],
        out_specs=[vt_spec, st_spec],
        out_shape=[jax.ShapeDtypeStruct(vt.shape, F32), jax.ShapeDtypeStruct(s0.shape, F32)],
        scratch_shapes=[pltpu.VMEM((A_HEADS, A_HEAD_DIM, A_HEAD_DIM), F32)],
        compiler_params=_cparams(("parallel", "arbitrary")),
        name="rwkv_scan",
    )(r, d, k, kk, b, vt, s0)
    return yt.reshape(bsz, A_HEADS // 2, LANES, nt * tp), s_new


def _dwa_prompt_kernel(q_ref, kc_ref, kp_ref, vc_ref, vp_ref, o_ref, lse_ref):
    n = pl.program_id(2)
    i = lax.broadcasted_iota(jnp.int32, (B_BLK, B_BLK), 0)
    j = lax.broadcasted_iota(jnp.int32, (B_BLK, B_BLK), 1)
    mask_prev = j >= i + jnp.where(n > 0, 0, B_BLK)
    mask_cur = j <= i
    neg = -jnp.inf
    scale = B_HEAD_DIM ** -0.5
    dn = (((1,), (1,)), ((), ()))
    outs, lses = [], []
    for hh in range(LANES // B_HEAD_DIM):
        sl = slice(hh * B_HEAD_DIM, (hh + 1) * B_HEAD_DIM)
        q = q_ref[0][:, sl].astype(BF16)
        sp = lax.dot_general(q, kp_ref[0][:, sl].astype(BF16), dn, preferred_element_type=F32) * scale
        sc = lax.dot_general(q, kc_ref[0][:, sl].astype(BF16), dn, preferred_element_type=F32) * scale
        sp = jnp.where(mask_prev, sp, neg)
        sc = jnp.where(mask_cur, sc, neg)
        m = jnp.maximum(jnp.max(sp, axis=-1, keepdims=True), jnp.max(sc, axis=-1, keepdims=True))
        ep = jnp.exp(sp - m)
        ec = jnp.exp(sc - m)
        l = jnp.sum(ep, axis=-1, keepdims=True) + jnp.sum(ec, axis=-1, keepdims=True)
        o = (jnp.dot((ep / l).astype(BF16), vp_ref[0][:, sl].astype(BF16), preferred_element_type=F32)
             + jnp.dot((ec / l).astype(BF16), vc_ref[0][:, sl].astype(BF16), preferred_element_type=F32))
        outs.append(o)
        lses.append(jnp.broadcast_to(m + jnp.log(l), (B_BLK, B_HEAD_DIM)))
    o_ref[0] = jnp.concatenate(outs, axis=-1)
    lse_ref[0] = jnp.concatenate(lses, axis=-1)


def dwa_prompt(h3, g):
    bsz, t, _ = h3.shape
    _, dil = B_CONFIGS[g]
    rows = t // dil
    nb = rows // B_BLK
    hv = h3.reshape(bsz, rows, dil * NU * LANES)
    npair = B_OUT // LANES

    def in_spec(unit, prev):
        def imap(b, r, n, hp):
            nn = jnp.maximum(n - 1, 0) if prev else n
            return (b, nn, r * NU + unit + npair * g + hp)
        return pl.BlockSpec((1, B_BLK, LANES), imap)

    out_spec = pl.BlockSpec((1, B_BLK, LANES), lambda b, r, n, hp: (b, n, r * npair + hp))
    out_shape = jax.ShapeDtypeStruct((bsz, rows, dil * B_OUT), F32)
    o, lse = pl.pallas_call(
        _dwa_prompt_kernel,
        grid=(bsz, dil, nb, npair),
        in_specs=[in_spec(U_QB, False), in_spec(U_KB, False), in_spec(U_KB, True),
                  in_spec(U_VB, False), in_spec(U_VB, True)],
        out_specs=[out_spec, out_spec],
        out_shape=[out_shape, out_shape],
        compiler_params=_cparams(("parallel", "parallel", "parallel", "parallel")),
        name="dwa_prompt",
    )(hv, hv, hv, hv, hv)
    return o.reshape(bsz, t, B_OUT), lse.reshape(bsz, t, B_OUT)


def _dwa_decode_kernel(t_new, h_ref, c0_ref, c1_ref, c2_ref, *out_refs):
    kv_w = 2 * B_OUT
    scale = B_HEAD_DIM ** -0.5
    dn = (((1,), (1,)), ((), ()))
    head_of_lane = lax.broadcasted_iota(jnp.int32, (SUBLANES, B_OUT), 1) // B_HEAD_DIM
    sub = lax.broadcasted_iota(jnp.int32, (SUBLANES, B_OUT), 0)
    own = head_of_lane == sub
    m_idx = lax.broadcasted_iota(jnp.int32, (SUBLANES, B_BLK), 1)
    neg = -jnp.inf
    hrow = h_ref[0]
    for g, c_ref in enumerate((c0_ref, c1_ref, c2_ref)):
        _, dil = B_CONFIGS[g]
        o_ref, lse_ref = out_refs[2 * g], out_refs[2 * g + 1]
        qs = hrow[:, U_QB * LANES + g * B_OUT:U_QB * LANES + (g + 1) * B_OUT]
        ks = hrow[:, U_KB * LANES + g * B_OUT:U_KB * LANES + (g + 1) * B_OUT]
        vs = hrow[:, U_VB * LANES + g * B_OUT:U_VB * LANES + (g + 1) * B_OUT]
        for t in range(t_new):
            res = t % dil
            kbuf = c_ref[0][:, res * kv_w:res * kv_w + B_OUT]
            vbuf = c_ref[0][:, res * kv_w + B_OUT:(res + 1) * kv_w]
            qbd = jnp.where(own, jnp.broadcast_to(qs[t:t + 1], (SUBLANES, B_OUT)), 0.0)
            s_buf = lax.dot_general(qbd.astype(BF16), kbuf.astype(BF16), dn, preferred_element_type=F32) * scale
            s_buf = jnp.where(m_idx * dil + res >= t, s_buf, neg)
            new_rows = [n for n in range(t + 1) if (t - n) % dil == 0]
            kq = qbd.astype(BF16).astype(F32)
            s_new = [jnp.sum(kq * ks[n:n + 1].astype(BF16).astype(F32), axis=-1, keepdims=True) * scale
                     for n in new_rows]
            m = jnp.max(s_buf, axis=-1, keepdims=True)
            for sn in s_new:
                m = jnp.maximum(m, sn)
            e_buf = jnp.exp(s_buf - m)
            e_new = [jnp.exp(sn - m) for sn in s_new]
            l = jnp.sum(e_buf, axis=-1, keepdims=True)
            for en in e_new:
                l = l + en
            o = jnp.dot((e_buf / l).astype(BF16), vbuf.astype(BF16), preferred_element_type=F32)
            for n, en in zip(new_rows, e_new):
                o = o + (en / l).astype(BF16).astype(F32) * vs[n:n + 1].astype(BF16).astype(F32)
            lse = jnp.broadcast_to(m + jnp.log(l), (SUBLANES, B_OUT))
            o_ref[0, t:t + 1, :] = jnp.sum(jnp.where(own, o, 0.0), axis=0, keepdims=True)
            lse_ref[0, t:t + 1, :] = jnp.sum(jnp.where(own, lse, 0.0), axis=0, keepdims=True)


def dwa_decode(h3, caches):
    bsz, t, _ = h3.shape
    views, specs = [], []
    for g, (win, dil) in enumerate(B_CONFIGS):
        assert caches[g].shape[1] == win and (dil == 1 or t <= dil)
        views.append(caches[g].reshape(bsz, win // dil, dil * 2 * B_OUT))
        used = min(dil, t) * 2 * B_OUT
        specs.append(pl.BlockSpec((1, B_BLK, used), lambda b: (b, 0, 0)))
    out_spec = pl.BlockSpec((1, t, B_OUT), lambda b: (b, 0, 0))
    out_shape = jax.ShapeDtypeStruct((bsz, t, B_OUT), F32)
    outs = pl.pallas_call(
        functools.partial(_dwa_decode_kernel, t),
        grid=(bsz,),
        in_specs=[pl.BlockSpec((1, t, NU * LANES), lambda b: (b, 0, 0))] + specs,
        out_specs=[out_spec] * 6,
        out_shape=[out_shape] * 6,
        compiler_params=_cparams(("parallel",)),
        name="dwa_decode",
    )(h3, *views)
    return [(outs[0], outs[1]), (outs[2], outs[3]), (outs[4], outs[5])]


def _even_post_kernel(tt, yt_ref, bonus_ref, *refs):
    n_ga = A_WIDTH // LANES
    ga_refs = refs[:n_ga]
    (gb0_ref, gb1_ref, lnxg_ref, lnxb_ref, ones_ref,
     o0_ref, l0_ref, o1_ref, l1_ref, o2_ref, l2_ref, ua_ref, ub_ref) = refs[n_ga:]
    ys = []
    for p in range(A_HEADS // 2):
        ys.append(yt_ref[0, p].T[0:tt, :])
    y = jnp.concatenate(ys, axis=-1)
    ones_bf = ones_ref[...]
    inv = 1.0 / A_HEAD_DIM
    mu = _seg_sum(y, ones_bf) * inv
    yc = y - mu
    var = _seg_sum(yc * yc, ones_bf) * inv
    ya = yc * lax.rsqrt(var + A_GN_EPS) * lnxg_ref[...] + lnxb_ref[...] + bonus_ref[0]
    gate_a = jnp.concatenate([g_ref[0] for g_ref in ga_refs], axis=-1)
    ua_ref[0] = (ya * _silu(gate_a)).astype(BF16)
    l0, l1, l2 = l0_ref[0], l1_ref[0], l2_ref[0]
    m = jnp.maximum(jnp.maximum(l0, l1), l2)
    e0, e1, e2 = jnp.exp(l0 - m), jnp.exp(l1 - m), jnp.exp(l2 - m)
    den = e0 + e1 + e2
    yb = (e0 / den) * o0_ref[0] + (e1 / den) * o1_ref[0] + (e2 / den) * o2_ref[0]
    gate_b = jnp.concatenate([gb0_ref[0], gb1_ref[0]], axis=-1)
    ub_ref[0] = (yb * _silu(gate_b)).astype(BF16)


def even_post(yt, bonus, h3, lnx_g, lnx_b, ones768, dwa):
    bsz, t, _ = h3.shape
    tt = min(t, 128)
    tp = max(tt, LANES)
    nt = t // tt
    (o0, l0), (o1, l1), (o2, l2) = dwa
    bspec = pl.BlockSpec((1, tt, B_OUT), lambda b, i: (b, i, 0))
    row_spec = pl.BlockSpec((1, A_WIDTH), lambda b, i: (0, 0))
    return pl.pallas_call(
        functools.partial(_even_post_kernel, tt),
        grid=(bsz, nt),
        in_specs=[pl.BlockSpec((1, A_HEADS // 2, LANES, tp), lambda b, i: (b, 0, 0, i)),
                  pl.BlockSpec((1, tt, A_WIDTH), lambda b, i: (b, i, 0)),
                  *[pl.BlockSpec((1, tt, LANES), functools.partial(lambda b, i, u: (b, i, u), u=U_GATE_A + u))
                    for u in range(A_WIDTH // LANES)],
                  pl.BlockSpec((1, tt, LANES), lambda b, i: (b, i, U_GATE_B)),
                  pl.BlockSpec((1, tt, LANES), lambda b, i: (b, i, U_GATE_B + 1)),
                  row_spec, row_spec,
                  pl.BlockSpec((A_WIDTH, A_WIDTH), lambda b, i: (0, 0))] + [bspec] * 6,
        out_specs=[pl.BlockSpec((1, tt, A_WIDTH), lambda b, i: (b, i, 0)), bspec],
        out_shape=[jax.ShapeDtypeStruct((bsz, t, A_WIDTH), BF16), jax.ShapeDtypeStruct((bsz, t, B_OUT), BF16)],
        compiler_params=_cparams(("parallel", "parallel")),
        name="even_post",
    )(yt, bonus, *([h3] * (A_WIDTH // LANES + 2)), lnx_g.reshape(1, -1), lnx_b.reshape(1, -1), ones768, o0, l0, o1, l1, o2, l2)


def _mem_attn_kernel(q_ref, g_ref, k_ref, v_ref, u_ref):
    dn = (((1,), (1,)), ((), ()))
    s = lax.dot_general(q_ref[0].astype(BF16), k_ref[0].astype(BF16), dn,
                        preferred_element_type=F32) * (M_HEAD_DIM ** -0.5)
    m = jnp.max(s, axis=-1, keepdims=True)
    e = jnp.exp(s - m)
    p = e / jnp.sum(e, axis=-1, keepdims=True)
    o = jnp.dot(p.astype(BF16), v_ref[0].astype(BF16), preferred_element_type=F32)
    u_ref[0] = (o * _silu(g_ref[0])).astype(BF16)


def mem_attn(h3, mkv, u_q, u_g):
    bsz, t, _ = h3.shape
    tq = min(t, 512)
    return pl.pallas_call(
        _mem_attn_kernel,
        grid=(bsz, t // tq, M_HEADS),
        in_specs=[pl.BlockSpec((1, tq, LANES), lambda b, i, hd: (b, i, u_q + hd)),
                  pl.BlockSpec((1, tq, LANES), lambda b, i, hd: (b, i, u_g + hd)),
                  pl.BlockSpec((1, M_TOKENS, LANES), lambda b, i, hd: (b, 0, hd)),
                  pl.BlockSpec((1, M_TOKENS, LANES), lambda b, i, hd: (b, 0, M_HEADS + hd))],
        out_specs=pl.BlockSpec((1, tq, LANES), lambda b, i, hd: (b, i, hd)),
        out_shape=jax.ShapeDtypeStruct((bsz, t, M_WIDTH), BF16),
        compiler_params=_cparams(("parallel", "parallel", "parallel")),
        name="mem_attn",
    )(h3, h3, mkv, mkv)


def _rope_kernel(pos0, ang_ref, cos_ref, sin_ref):
    rows = cos_ref.shape[0]
    base = pl.program_id(0) * rows
    pos = (lax.broadcasted_iota(jnp.int32, cos_ref.shape, 0) + base).astype(F32) + pos0
    ph = pos * ang_ref[...]
    cos_ref[...] = jnp.cos(ph)
    sin_ref[...] = jnp.sin(ph)


def rope_tables(rows, pos0):
    angle = 1.0 / (C_ROT_BASE ** jnp.linspace(0.0, 1.0, C_HEAD_DIM // 2, dtype=F32))
    ang = jnp.repeat(angle, 2).reshape(1, C_HEAD_DIM)
    tr = min(rows, 512)
    spec = pl.BlockSpec((tr, C_HEAD_DIM), lambda i: (i, 0))
    shape = jax.ShapeDtypeStruct((rows, C_HEAD_DIM), F32)
    return pl.pallas_call(
        functools.partial(_rope_kernel, float(pos0)),
        grid=(rows // tr,),
        in_specs=[pl.BlockSpec((1, C_HEAD_DIM), lambda i: (0, 0))],
        out_specs=[spec, spec],
        out_shape=[shape, shape],
        compiler_params=_cparams(("parallel",)),
        name="rope_tables",
    )(ang)


def _rot_pairs(z):
    even = lax.broadcasted_iota(jnp.int32, (z.shape[0], LANES), 1) % 2 == 0
    parts = []
    for blk in range(z.shape[1] // LANES):
        zb = z[:, blk * LANES:(blk + 1) * LANES]
        nxt = pltpu.roll(zb, LANES - 1, axis=1)
        prv = pltpu.roll(zb, 1, axis=1)
        parts.append(jnp.where(even, -nxt, prv))
    return jnp.concatenate(parts, axis=-1)


def _retention_kernel(tb, chunk, lg_ref, q_ref, k_ref, v_ref, g_ref, cos_ref, sin_ref, r0_ref,
                      u_ref, rout_ref, r_scr, *pad):
    c = pl.program_id(2)

    @pl.when(c == 0)
    def _():
        r_scr[...] = r0_ref[0, 0]

    if tb == C_CHUNK:
        q, k, v = q_ref[0], k_ref[0], v_ref[0]
    else:
        vals = []
        for src, buf in zip((q_ref, k_ref, v_ref), pad):
            buf[...] = jnp.zeros_like(buf)
            buf[0:tb, :] = src[0]
            vals.append(buf[...])
        q, k, v = vals
    lg = lg_ref[pl.program_id(1)]
    cos, sin = cos_ref[...], sin_ref[...]
    qr = q * cos + _rot_pairs(q) * sin
    kr = (k * cos + _rot_pairs(k) * sin) * (C_HEAD_DIM ** -0.5)
    ii = lax.broadcasted_iota(jnp.int32, (C_CHUNK, C_CHUNK), 0)
    jj = lax.broadcasted_iota(jnp.int32, (C_CHUNK, C_CHUNK), 1)
    diff = (ii - jj).astype(F32)
    dmat = jnp.where(diff >= 0, jnp.exp(lg * jnp.maximum(diff, 0.0)), 0.0)
    idx = lax.broadcasted_iota(jnp.int32, (C_CHUNK, 1), 0).astype(F32)
    xi = jnp.exp(lg * (idx + 1.0))
    zeta = jnp.exp(lg * (chunk - 1.0 - idx))
    g_chunk = jnp.exp(lg * jnp.full((1, 1), chunk, F32))
    dn = (((1,), (1,)), ((), ()))
    qb = qr.astype(BF16)
    vb = v.astype(BF16)
    sc = lax.dot_general(qb, kr.astype(BF16), dn, preferred_element_type=F32) * dmat
    r_old = r_scr[...]
    o = (jnp.dot(sc.astype(BF16), vb, preferred_element_type=F32)
         + jnp.dot(qb, r_old.astype(BF16), preferred_element_type=F32) * xi)
    kz_t = (kr * zeta).T.astype(BF16)
    r_scr[...] = r_old * g_chunk + jnp.dot(kz_t, vb, preferred_element_type=F32)
    o = o[0:tb]
    y = o * lax.rsqrt(jnp.mean(o * o, axis=-1, keepdims=True) + C_NORM_EPS)
    u_ref[0] = (y * _silu(g_ref[0])).astype(BF16)

    @pl.when(c == pl.num_programs(2) - 1)
    def _():
        rout_ref[0, 0] = r_scr[...]


def retention(h3, cos, sin, r0):
    bsz, t, _ = h3.shape
    tb = min(t, C_CHUNK)
    nc = t // tb
    chunk = float(tb)
    lg = jnp.log(1.0 - 2.0 ** (-5.0 - jnp.arange(C_HEADS, dtype=F32)))
    nq = C_WIDTH // C_HEAD_DIM

    def col_spec(off):
        return pl.BlockSpec((1, tb, C_HEAD_DIM), lambda b, hd, c: (b, c, off + hd))

    tab_spec = pl.BlockSpec((C_CHUNK, C_HEAD_DIM), lambda b, hd, c: (c, 0))
    st_spec = pl.BlockSpec((1, 1, C_HEAD_DIM, C_HEAD_DIM), lambda b, hd, c: (b, hd, 0, 0))
    scratch = [pltpu.VMEM((C_HEAD_DIM, C_HEAD_DIM), F32)]
    if tb != C_CHUNK:
        scratch += [pltpu.VMEM((C_CHUNK, C_HEAD_DIM), F32)] * 3
    return pl.pallas_call(
        functools.partial(_retention_kernel, tb, chunk),
        grid=(bsz, C_HEADS, nc),
        in_specs=[pl.BlockSpec(memory_space=pltpu.SMEM),
                  col_spec(0), col_spec(nq), col_spec(2 * nq), col_spec(3 * nq),
                  tab_spec, tab_spec, st_spec],
        out_specs=[pl.BlockSpec((1, tb, C_HEAD_DIM), lambda b, hd, c: (b, c, hd)), st_spec],
        out_shape=[jax.ShapeDtypeStruct((bsz, t, C_WIDTH), BF16), jax.ShapeDtypeStruct(r0.shape, F32)],
        scratch_shapes=scratch,
        compiler_params=_cparams(("parallel", "parallel", "arbitrary")),
        name="retention",
    )(lg, h3, h3, h3, h3, cos, sin, r0)


def _even_layer(x, x_bf, mkv, dwa_bufs, s0, shift0, w_in_bf, w_out_bf, ln_g, ln_b, rw, ones768):
    bsz, t, d = x.shape
    h = matmul(x_bf.reshape(bsz * t, d), w_in_bf)
    h3 = h.reshape(bsz, t, NU * LANES)
    mu, w0, wup, a0, aup, k_k, k_a, r_k, lnx_g, lnx_b = rw
    r, dcy, k2, kk, bb, vt, bonus = rwkv_prep(h3, shift0, (mu, w0, wup, a0, aup, k_k, k_a, r_k), ones768)
    yt, s_new = rwkv_scan((r, dcy, k2, kk, bb), vt, s0)
    if dwa_bufs is None:
        dwa = [dwa_prompt(h3, g) for g in range(B_GROUPS)]
    else:
        dwa = dwa_decode(h3, dwa_bufs)
    u_a, u_b = even_post(yt, bonus, h3, lnx_g, lnx_b, ones768, dwa)
    u_m = mem_attn(h3, mkv, U_QM_EVEN, U_GM_EVEN)
    x2, x2_bf = outproj_ln([u_a.reshape(bsz * t, -1), u_b.reshape(bsz * t, -1), u_m.reshape(bsz * t, -1)],
                           w_out_bf, x.reshape(bsz * t, d), ln_g, ln_b)
    rows = []
    for g, (win, _) in enumerate(B_CONFIGS):
        keep = t if dwa_bufs is not None else min(win, t)
        kg = h3[:, t - keep:, U_KB * LANES + g * B_OUT:U_KB * LANES + (g + 1) * B_OUT]
        vg = h3[:, t - keep:, U_VB * LANES + g * B_OUT:U_VB * LANES + (g + 1) * B_OUT]
        rows.append(jnp.stack([kg.reshape(bsz, keep, B_HEADS_PER_GROUP, B_HEAD_DIM),
                               vg.reshape(bsz, keep, B_HEADS_PER_GROUP, B_HEAD_DIM)], axis=2))
    return x2.reshape(bsz, t, d), x2_bf.reshape(bsz, t, d), s_new, h3[:, t - 1, :A_SHIFT_W], rows


def _odd_layer(x, x_bf, mkv, r0, tabs, w_in_bf, w_out_bf, ln_g, ln_b):
    bsz, t, d = x.shape
    h = matmul(x_bf.reshape(bsz * t, d), w_in_bf)
    h3 = h.reshape(bsz, t, NU * LANES)
    u_c, r_new = retention(h3, tabs[0], tabs[1], r0)
    u_m = mem_attn(h3, mkv, U_QM_ODD, U_GM_ODD)
    x2, x2_bf = outproj_ln([u_c.reshape(bsz * t, -1), u_m.reshape(bsz * t, -1)],
                           w_out_bf, x.reshape(bsz * t, d), ln_g, ln_b)
    return x2.reshape(bsz, t, d), x2_bf.reshape(bsz, t, d), r_new


def kernel(x_prompt, x_sample, state_rwkv, state_rwkv_shift, cache_dwa_g0, cache_dwa_g1, cache_dwa_g2, state_ret, cache_mem_kv, mem_prompt, w_in_even, w_out_even, w_in_odd, w_out_odd, w_mem_kv, ln_g, ln_b, rwkv_mu, rwkv_w0, rwkv_w_up, rwkv_a0, rwkv_a_up, rwkv_k_k, rwkv_k_a, rwkv_r_k, rwkv_lnx_g, rwkv_lnx_b):
    xp, xs = x_prompt, x_sample
    xp_bf, xs_bf = xp.astype(BF16), xs.astype(BF16)
    bp, tp_len, d = xp.shape
    bs, ts_len, _ = xs.shape
    dwa_cache = (cache_dwa_g0, cache_dwa_g1, cache_dwa_g2)
    ones768 = _block_ones(A_WIDTH, A_HEAD_DIM)
    mem_bf = mem_prompt.reshape(bp * M_TOKENS, d).astype(BF16)
    tabs_p = rope_tables(max(tp_len, C_CHUNK), 0)
    tabs_s = rope_tables(max(ts_len, C_CHUNK), PAST_LEN)
    rwkv_p, rwkv_s, shift_p, shift_s, ret_p, ret_s, mem_p = [], [], [], [], [], [], []
    dwa_p = [[] for _ in B_CONFIGS]
    dwa_s = [[] for _ in B_CONFIGS]
    for l in range(DEPTH):
        mkv_p = matmul(mem_bf, w_mem_kv[l].astype(BF16)).reshape(bp, M_TOKENS, 2 * M_WIDTH)
        mem_p.append(mkv_p.reshape(bp, M_TOKENS, 2, M_HEADS, M_HEAD_DIM))
        mkv_s = cache_mem_kv[l].reshape(bs, M_TOKENS, 2 * M_WIDTH)
        if l % 2 == 0:
            e = l // 2
            w_in_bf = jnp.pad(w_in_even[e].astype(BF16), ((0, 0), (0, EVEN_IN_PAD - EVEN_IN)))
            w_out_bf = w_out_even[e].astype(BF16)
            rw = (rwkv_mu[e], rwkv_w0[e], rwkv_w_up[e], rwkv_a0[e], rwkv_a_up[e], rwkv_k_k[e], rwkv_k_a[e],
                  rwkv_r_k[e], rwkv_lnx_g[e], rwkv_lnx_b[e])
            s0 = jnp.zeros((bp, A_HEADS, A_HEAD_DIM, A_HEAD_DIM), F32)
            sh0 = jnp.zeros((bp, A_SHIFT_W), F32)
            xp, xp_bf, st, sh, rows = _even_layer(xp, xp_bf, mkv_p, None, s0, sh0, w_in_bf, w_out_bf,
                                                  ln_g[l], ln_b[l], rw, ones768)
            rwkv_p.append(st)
            shift_p.append(sh)
            for g in range(B_GROUPS):
                dwa_p[g].append(rows[g])
            bufs = tuple(c[e] for c in dwa_cache)
            xs, xs_bf, st, sh, rows = _even_layer(xs, xs_bf, mkv_s, bufs, state_rwkv[e], state_rwkv_shift[e],
                                                  w_in_bf, w_out_bf, ln_g[l], ln_b[l], rw, ones768)
            rwkv_s.append(st)
            shift_s.append(sh)
            for g in range(B_GROUPS):
                dwa_s[g].append(rows[g])
        else:
            o = l // 2
            w_in_bf = w_in_odd[o].astype(BF16)
            w_out_bf = w_out_odd[o].astype(BF16)
            r0 = jnp.zeros((bp, C_HEADS, C_HEAD_DIM, C_HEAD_DIM), F32)
            xp, xp_bf, st = _odd_layer(xp, xp_bf, mkv_p, r0, tabs_p, w_in_bf, w_out_bf, ln_g[l], ln_b[l])
            ret_p.append(st)
            xs, xs_bf, st = _odd_layer(xs, xs_bf, mkv_s, state_ret[o], tabs_s, w_in_bf, w_out_bf, ln_g[l], ln_b[l])
            ret_s.append(st)
    return (xp, xs, jnp.stack(rwkv_p), jnp.stack(rwkv_s), jnp.stack(shift_p), jnp.stack(shift_s),
            jnp.stack(dwa_p[0]), jnp.stack(dwa_s[0]), jnp.stack(dwa_p[1]), jnp.stack(dwa_s[1]),
            jnp.stack(dwa_p[2]), jnp.stack(dwa_s[2]), jnp.stack(ret_p), jnp.stack(ret_s), jnp.stack(mem_p))
```

```python
import functools

import numpy as np
import jax
import jax.numpy as jnp
from jax import lax
from jax.experimental import pallas as pl
from jax.experimental.pallas import tpu as pltpu

F32 = jnp.float32
BF16 = jnp.bfloat16

D_MODEL = 2048
DEPTH = 4
PAST_LEN = 16384
ALPHA = (2 * DEPTH) ** 0.25
LN_EPS = 1e-5
A_HEADS = 12
A_HEAD_DIM = 64
A_WIDTH = A_HEADS * A_HEAD_DIM
A_LORA = 64
A_SHIFT_W = 3 * A_WIDTH + 2 * A_LORA
A_GN_EPS = 64e-5
B_CONFIGS = ((128, 1), (512, 4), (2048, 16))
B_GROUPS = 3
B_HEADS_PER_GROUP = 4
B_HEAD_DIM = 64
B_OUT = B_HEADS_PER_GROUP * B_HEAD_DIM
B_WIDTH = B_GROUPS * B_OUT
B_BLK = 128
C_HEADS = 6
C_HEAD_DIM = 256
C_WIDTH = C_HEADS * C_HEAD_DIM
C_CHUNK = 128
C_ROT_BASE = 10000.0
C_NORM_EPS = 1e-6
M_TOKENS = 256
M_HEADS = 4
M_HEAD_DIM = 128
M_WIDTH = M_HEADS * M_HEAD_DIM
EVEN_IN = A_SHIFT_W + A_WIDTH + 3 * B_WIDTH + B_OUT + 2 * M_WIDTH
ODD_IN = 4 * C_WIDTH + 2 * M_WIDTH

LANES = 128
SUBLANES = 8
VMEM_LIMIT = 48 * 1024 * 1024

EVEN_IN_PAD = 7168
U_GATE_A = A_SHIFT_W // LANES
U_QB = U_GATE_A + A_WIDTH // LANES
U_KB = U_QB + B_WIDTH // LANES
U_VB = U_KB + B_WIDTH // LANES
U_GATE_B = U_VB + B_WIDTH // LANES
U_QM_EVEN = U_GATE_B + B_OUT // LANES
U_GM_EVEN = U_QM_EVEN + M_WIDTH // LANES
U_QM_ODD = 4 * C_WIDTH // LANES
U_GM_ODD = U_QM_ODD + M_WIDTH // LANES
NU = EVEN_IN_PAD // LANES


def _cparams(sem):
    return pltpu.CompilerParams(dimension_semantics=sem, vmem_limit_bytes=VMEM_LIMIT)


def _sigmoid(z):
    return 1.0 / (1.0 + jnp.exp(-z))


def _silu(z):
    return z * _sigmoid(z)


def _mm_kernel(x_ref, w_ref, o_ref):
    o_ref[...] = jnp.dot(x_ref[...], w_ref[0], preferred_element_type=F32)


def matmul(x, w, layer, tn=512):
    m, k = x.shape
    n = w.shape[2]
    tm = min(m, 1024)
    return pl.pallas_call(
        _mm_kernel,
        grid=(m // tm, n // tn),
        in_specs=[pl.BlockSpec((tm, k), lambda i, j: (i, 0)),
                  pl.BlockSpec((1, k, tn), lambda i, j: (layer, 0, j))],
        out_specs=pl.BlockSpec((tm, tn), lambda i, j: (i, j)),
        out_shape=jax.ShapeDtypeStruct((m, n), F32),
        compiler_params=_cparams(("parallel", "parallel")),
        name="matmul",
    )(x, w)


def _outproj_kernel(n_u, *refs):
    u_refs = refs[:n_u]
    w_refs = refs[n_u:2 * n_u]
    x_ref, g_ref, b_ref, o_ref, obf_ref = refs[2 * n_u:]
    acc = jnp.dot(u_refs[0][...], w_refs[0][0], preferred_element_type=F32)
    for u_ref, w_ref in zip(u_refs[1:], w_refs[1:]):
        acc = acc + jnp.dot(u_ref[...], w_ref[0], preferred_element_type=F32)
    z = ALPHA * x_ref[...] + acc
    mu = jnp.mean(z, axis=-1, keepdims=True)
    zc = z - mu
    var = jnp.mean(zc * zc, axis=-1, keepdims=True)
    y = zc * lax.rsqrt(var + LN_EPS) * g_ref[...] + b_ref[...]
    o_ref[...] = y
    obf_ref[...] = y.astype(BF16)


def outproj_ln(us, w_out, layer, x, g, b):
    m, d = x.shape
    tm = min(m, 256)
    n_u = len(us)
    in_specs = [pl.BlockSpec((tm, u.shape[1]), lambda i: (i, 0)) for u in us]
    row = 0
    for u in us:
        kw = u.shape[1]
        assert row % kw == 0
        in_specs.append(pl.BlockSpec((1, kw, d), functools.partial(lambda i, r: (layer, r, 0), r=row // kw)))
        row += kw
    assert row == w_out.shape[1]
    in_specs += [pl.BlockSpec((tm, d), lambda i: (i, 0)),
                 pl.BlockSpec((1, d), lambda i: (0, 0)),
                 pl.BlockSpec((1, d), lambda i: (0, 0))]
    return pl.pallas_call(
        functools.partial(_outproj_kernel, n_u),
        grid=(m // tm,),
        in_specs=in_specs,
        out_specs=[pl.BlockSpec((tm, d), lambda i: (i, 0)), pl.BlockSpec((tm, d), lambda i: (i, 0))],
        out_shape=[jax.ShapeDtypeStruct((m, d), F32), jax.ShapeDtypeStruct((m, d), BF16)],
        compiler_params=_cparams(("parallel",)),
        name="outproj_ln",
    )(*us, *([w_out] * n_u), x, g.reshape(1, d), b.reshape(1, d))


def _seg_sum(x, ones_bf):
    hi = x.astype(BF16)
    r1 = x - hi.astype(F32)
    mid = r1.astype(BF16)
    lo = (r1 - mid.astype(F32)).astype(BF16)
    return (jnp.dot(hi, ones_bf, preferred_element_type=F32)
            + jnp.dot(mid, ones_bf, preferred_element_type=F32)
            + jnp.dot(lo, ones_bf, preferred_element_type=F32))


def _block_ones(width, seg):
    i = np.arange(width) // seg
    return jnp.asarray((i[:, None] == i[None, :]).astype(np.float32), dtype=BF16)


def _rwkv_prep_kernel(tt, tp, h_ref, prev_ref, sh0_ref, mu_ref, w0_ref, wup_ref, a0_ref, aup_ref,
                      kk_ref_p, ka_ref_p, rk_ref_p, ones_ref,
                      r_o, d_o, k_o, kk_o, b_o, vt_o, bonus_o, *scratch):
    i = pl.program_id(1)
    x = h_ref[0]
    pr = prev_ref[0]
    prev_last = pr[pr.shape[0] - 1:pr.shape[0], :]
    first = jnp.where(i == 0, sh0_ref[0], prev_last)
    row = lax.broadcasted_iota(jnp.int32, x.shape, 0)
    if tt % SUBLANES == 0:
        rolled = pltpu.roll(x, 1, axis=0)
    else:
        rolled = jnp.concatenate([x[tt - 1:tt], x[:tt - 1]], axis=0)
    prev = jnp.where(row == 0, first, rolled)
    hs = x + (prev - x) * mu_ref[...]
    r = hs[:, 0:A_WIDTH]
    k = hs[:, A_WIDTH:2 * A_WIDTH]
    v = hs[:, 2 * A_WIDTH:3 * A_WIDTH]
    hw = hs[:, 3 * A_WIDTH:3 * A_WIDTH + A_LORA]
    ha = hs[:, 3 * A_WIDTH + A_LORA:A_SHIFT_W]
    zw = w0_ref[...] + jnp.dot(jnp.tanh(hw).astype(BF16), wup_ref[...], preferred_element_type=F32)
    nz = -zw
    softplus = jnp.maximum(nz, 0.0) + jnp.log(1.0 + jnp.exp(-jnp.abs(nz)))
    w_log = -softplus - 0.5
    decay = jnp.exp(-jnp.exp(w_log))
    a = _sigmoid(a0_ref[...] + jnp.dot(ha.astype(BF16), aup_ref[...], preferred_element_type=F32))
    ones_bf = ones_ref[...]
    kk = k * kk_ref_p[...]
    kk = kk * lax.rsqrt(jnp.maximum(_seg_sum(kk * kk, ones_bf), 1e-24))
    k2 = k * (1.0 + (a - 1.0) * ka_ref_p[...])
    bb = kk * a
    bonus_o[0] = _seg_sum(r * k2 * rk_ref_p[...], ones_bf) * v
    for hd in range(A_HEADS):
        sl = slice(hd * A_HEAD_DIM, (hd + 1) * A_HEAD_DIM)
        r_o[0, hd] = r[:, sl]
        d_o[0, hd] = decay[:, sl]
        k_o[0, hd] = k2[:, sl]
        kk_o[0, hd] = kk[:, sl]
        b_o[0, hd] = bb[:, sl]
    for p in range(A_HEADS // 2):
        vp = v[:, p * LANES:(p + 1) * LANES]
        if tt == tp:
            vt_o[0, p] = vp.T
        else:
            pad = scratch[0]
            pad[...] = jnp.zeros_like(pad)
            pad[0:tt, :] = vp
            vt_o[0, p] = pad[...].T


def rwkv_prep(h3, shift0, prm, ones768):
    bsz, t, _ = h3.shape
    tt = min(t, 128)
    tp = max(tt, LANES)
    nt = t // tt
    pr_rows = min(t, SUBLANES)
    pb = tt // pr_rows
    mu, w0, wup, a0, aup, k_k, k_a, r_k = prm
    row_spec = pl.BlockSpec((1, A_WIDTH), lambda b, i: (0, 0))
    lora_spec = pl.BlockSpec((A_LORA, A_WIDTH), lambda b, i: (0, 0))
    head_spec = pl.BlockSpec((1, A_HEADS, tt, A_HEAD_DIM), lambda b, i: (b, 0, i, 0))
    head_shape = jax.ShapeDtypeStruct((bsz, A_HEADS, t, A_HEAD_DIM), F32)
    scratch = [] if tt == tp else [pltpu.VMEM((tp, LANES), F32)]
    return pl.pallas_call(
        functools.partial(_rwkv_prep_kernel, tt, tp),
        grid=(bsz, nt),
        in_specs=[pl.BlockSpec((1, tt, A_SHIFT_W), lambda b, i: (b, i, 0)),
                  pl.BlockSpec((1, pr_rows, A_SHIFT_W), lambda b, i: (b, jnp.maximum(i * pb - 1, 0), 0)),
                  pl.BlockSpec((1, 1, A_SHIFT_W), lambda b, i: (b, 0, 0)),
                  pl.BlockSpec((1, A_SHIFT_W), lambda b, i: (0, 0)),
                  row_spec, lora_spec, row_spec, lora_spec, row_spec, row_spec, row_spec,
                  pl.BlockSpec((A_WIDTH, A_WIDTH), lambda b, i: (0, 0))],
        out_specs=[head_spec] * 5 + [
            pl.BlockSpec((1, A_HEADS // 2, LANES, tp), lambda b, i: (b, 0, 0, i)),
            pl.BlockSpec((1, tt, A_WIDTH), lambda b, i: (b, i, 0))],
        out_shape=[head_shape] * 5 + [
            jax.ShapeDtypeStruct((bsz, A_HEADS // 2, LANES, nt * tp), F32),
            jax.ShapeDtypeStruct((bsz, t, A_WIDTH), F32)],
        scratch_shapes=scratch,
        compiler_params=_cparams(("parallel", "parallel")),
        name="rwkv_prep",
    )(h3, h3, shift0.reshape(bsz, 1, A_SHIFT_W), mu.reshape(1, -1), w0.reshape(1, -1), wup.astype(BF16),
      a0.reshape(1, -1), aup.astype(BF16), k_k.reshape(1, -1), k_a.reshape(1, -1), r_k.reshape(1, -1), ones768)


def _rwkv_scan_kernel(steps, r_ref, d_ref, k_ref, kk_ref, b_ref, vt_ref, s0_ref, yt_ref, sout_ref, s_scr):
    c = pl.program_id(1)

    @pl.when(c == 0)
    def _():
        s_scr[...] = s0_ref[0]

    tp = vt_ref.shape[3]
    lane = lax.broadcasted_iota(jnp.int32, (A_HEAD_DIM, tp), 1)
    yt_ref[...] = jnp.zeros_like(yt_ref)

    def body(t, carry):
        msk = lane == t
        sas, vcols, ys = [], [], []
        for hd in range(A_HEADS):
            sas.append(-jnp.sum(s_scr[hd] * kk_ref[0, hd, pl.ds(t, 1), :], axis=-1, keepdims=True))
            vcols.append(jnp.sum(jnp.where(msk, vt_ref[0, hd], 0.0), axis=-1, keepdims=True))
        for hd in range(A_HEADS):
            s = (s_scr[hd] * d_ref[0, hd, pl.ds(t, 1), :] + sas[hd] * b_ref[0, hd, pl.ds(t, 1), :]
                 + vcols[hd] * k_ref[0, hd, pl.ds(t, 1), :])
            s_scr[hd] = s
            ys.append(jnp.sum(s * r_ref[0, hd, pl.ds(t, 1), :], axis=-1, keepdims=True))
        for hd in range(A_HEADS):
            yt_ref[0, hd] = jnp.where(msk, ys[hd], yt_ref[0, hd])
        return carry

    lax.fori_loop(0, steps, body, 0, unroll=2)

    @pl.when(c == pl.num_programs(1) - 1)
    def _():
        sout_ref[0] = s_scr[...]


def rwkv_scan(ops, vt, s0):
    r, d, k, kk, b = ops
    bsz, _, t, _ = r.shape
    tt = min(t, 128)
    tp = max(tt, LANES)
    nt = t // tt
    vt = vt.reshape(bsz, A_HEADS, A_HEAD_DIM, nt * tp)
    head_spec = pl.BlockSpec((1, A_HEADS, tt, A_HEAD_DIM), lambda b_, i: (b_, 0, i, 0))
    vt_spec = pl.BlockSpec((1, A_HEADS, A_HEAD_DIM, tp), lambda b_, i: (b_, 0, 0, i))
    st_spec = pl.BlockSpec((1, A_HEADS, A_HEAD_DIM, A_HEAD_DIM), lambda b_, i: (b_, 0, 0, 0))
    yt, s_new = pl.pallas_call(
        functools.partial(_rwkv_scan_kernel, tt),
        grid=(bsz, nt),
        in_specs=[head_spec] * 5 + [vt_spec, st_spec],
        out_specs=[vt_spec, st_spec],
        out_shape=[jax.ShapeDtypeStruct(vt.shape, F32), jax.ShapeDtypeStruct(s0.shape, F32)],
        scratch_shapes=[pltpu.VMEM((A_HEADS, A_HEAD_DIM, A_HEAD_DIM), F32)],
        compiler_params=_cparams(("parallel", "arbitrary")),
        name="rwkv_scan",
    )(r, d, k, kk, b, vt, s0)
    return yt.reshape(bsz, A_HEADS // 2, LANES, nt * tp), s_new


def _dwa_prompt_kernel(dil, q_ref, kc_ref, kp_ref, vc_ref, vp_ref, o_ref, lse_ref):
    n = pl.program_id(1)
    i = lax.broadcasted_iota(jnp.int32, (B_BLK, B_BLK), 0)
    j = lax.broadcasted_iota(jnp.int32, (B_BLK, B_BLK), 1)
    mask_prev = j >= i + jnp.where(n > 0, 0, B_BLK)
    mask_cur = j <= i
    neg = -jnp.inf
    scale = B_HEAD_DIM ** -0.5
    dn = (((1,), (1,)), ((), ()))

    def residue(r, carry):
        rows = pl.ds(r, B_BLK, stride=dil) if dil > 1 else pl.ds(0, B_BLK)
        q2, kc2, kp2, vc2, vp2 = (ref[0, rows, :] for ref in (q_ref, kc_ref, kp_ref, vc_ref, vp_ref))
        outs, lses = [], []
        for hh in range(LANES // B_HEAD_DIM):
            sl = slice(hh * B_HEAD_DIM, (hh + 1) * B_HEAD_DIM)
            q = q2[:, sl].astype(BF16)
            sp = lax.dot_general(q, kp2[:, sl].astype(BF16), dn, preferred_element_type=F32) * scale
            sc = lax.dot_general(q, kc2[:, sl].astype(BF16), dn, preferred_element_type=F32) * scale
            sp = jnp.where(mask_prev, sp, neg)
            sc = jnp.where(mask_cur, sc, neg)
            m = jnp.maximum(jnp.max(sp, axis=-1, keepdims=True), jnp.max(sc, axis=-1, keepdims=True))
            ep = jnp.exp(sp - m)
            ec = jnp.exp(sc - m)
            l = jnp.sum(ep, axis=-1, keepdims=True) + jnp.sum(ec, axis=-1, keepdims=True)
            o = (jnp.dot((ep / l).astype(BF16), vp2[:, sl].astype(BF16), preferred_element_type=F32)
                 + jnp.dot((ec / l).astype(BF16), vc2[:, sl].astype(BF16), preferred_element_type=F32))
            outs.append(o)
            lses.append(jnp.broadcast_to(m + jnp.log(l), (B_BLK, B_HEAD_DIM)))
        o_ref[0, rows, :] = jnp.concatenate(outs, axis=-1)
        lse_ref[0, rows, :] = jnp.concatenate(lses, axis=-1)
        return carry

    lax.fori_loop(0, dil, residue, 0)


def dwa_prompt(h3, g):
    bsz, t, _ = h3.shape
    win, dil = B_CONFIGS[g]
    assert win == B_BLK * dil and t % win == 0
    npair = B_OUT // LANES

    def in_spec(unit, prev):
        def imap(b, n, hp):
            nn = jnp.maximum(n - 1, 0) if prev else n
            return (b, nn, unit + npair * g + hp)
        return pl.BlockSpec((1, win, LANES), imap)

    out_spec = pl.BlockSpec((1, win, LANES), lambda b, n, hp: (b, n, hp))
    out_shape = jax.ShapeDtypeStruct((bsz, t, B_OUT), F32)
    return pl.pallas_call(
        functools.partial(_dwa_prompt_kernel, dil),
        grid=(bsz, t // win, npair),
        in_specs=[in_spec(U_QB, False), in_spec(U_KB, False), in_spec(U_KB, True),
                  in_spec(U_VB, False), in_spec(U_VB, True)],
        out_specs=[out_spec, out_spec],
        out_shape=[out_shape, out_shape],
        compiler_params=_cparams(("parallel", "parallel", "parallel")),
        name="dwa_prompt",
    )(h3, h3, h3, h3, h3)


def _dwa_decode_kernel(t_new, h_ref, c0_ref, c1_ref, c2_ref, *out_refs):
    kv_w = 2 * B_OUT
    scale = B_HEAD_DIM ** -0.5
    dn = (((1,), (1,)), ((), ()))
    head_of_lane = lax.broadcasted_iota(jnp.int32, (SUBLANES, B_OUT), 1) // B_HEAD_DIM
    sub = lax.broadcasted_iota(jnp.int32, (SUBLANES, B_OUT), 0)
    own = head_of_lane == sub
    m_idx = lax.broadcasted_iota(jnp.int32, (SUBLANES, B_BLK), 1)
    neg = -jnp.inf
    hrow = h_ref[0]
    for g, c_ref in enumerate((c0_ref, c1_ref, c2_ref)):
        _, dil = B_CONFIGS[g]
        o_ref, lse_ref = out_refs[2 * g], out_refs[2 * g + 1]
        qs = hrow[:, U_QB * LANES + g * B_OUT:U_QB * LANES + (g + 1) * B_OUT]
        ks = hrow[:, U_KB * LANES + g * B_OUT:U_KB * LANES + (g + 1) * B_OUT]
        vs = hrow[:, U_VB * LANES + g * B_OUT:U_VB * LANES + (g + 1) * B_OUT]
        for t in range(t_new):
            res = t % dil
            kbuf = c_ref[0][:, res * kv_w:res * kv_w + B_OUT]
            vbuf = c_ref[0][:, res * kv_w + B_OUT:(res + 1) * kv_w]
            qbd = jnp.where(own, jnp.broadcast_to(qs[t:t + 1], (SUBLANES, B_OUT)), 0.0)
            s_buf = lax.dot_general(qbd.astype(BF16), kbuf.astype(BF16), dn, preferred_element_type=F32) * scale
            s_buf = jnp.where(m_idx * dil + res >= t, s_buf, neg)
            new_rows = [n for n in range(t + 1) if (t - n) % dil == 0]
            kq = qbd.astype(BF16).astype(F32)
            s_new = [jnp.sum(kq * ks[n:n + 1].astype(BF16).astype(F32), axis=-1, keepdims=True) * scale
                     for n in new_rows]
            m = jnp.max(s_buf, axis=-1, keepdims=True)
            for sn in s_new:
                m = jnp.maximum(m, sn)
            e_buf = jnp.exp(s_buf - m)
            e_new = [jnp.exp(sn - m) for sn in s_new]
            l = jnp.sum(e_buf, axis=-1, keepdims=True)
            for en in e_new:
                l = l + en
            o = jnp.dot((e_buf / l).astype(BF16), vbuf.astype(BF16), preferred_element_type=F32)
            for n, en in zip(new_rows, e_new):
                o = o + (en / l).astype(BF16).astype(F32) * vs[n:n + 1].astype(BF16).astype(F32)
            lse = jnp.broadcast_to(m + jnp.log(l), (SUBLANES, B_OUT))
            o_ref[0, t:t + 1, :] = jnp.sum(jnp.where(own, o, 0.0), axis=0, keepdims=True)
            lse_ref[0, t:t + 1, :] = jnp.sum(jnp.where(own, lse, 0.0), axis=0, keepdims=True)


def dwa_decode(h3, caches):
    bsz, t, _ = h3.shape
    views, specs = [], []
    for g, (win, dil) in enumerate(B_CONFIGS):
        assert caches[g].shape[1] == win and (dil == 1 or t <= dil)
        views.append(caches[g].reshape(bsz, win // dil, dil * 2 * B_OUT))
        used = min(dil, t) * 2 * B_OUT
        specs.append(pl.BlockSpec((1, B_BLK, used), lambda b: (b, 0, 0)))
    out_spec = pl.BlockSpec((1, t, B_OUT), lambda b: (b, 0, 0))
    out_shape = jax.ShapeDtypeStruct((bsz, t, B_OUT), F32)
    outs = pl.pallas_call(
        functools.partial(_dwa_decode_kernel, t),
        grid=(bsz,),
        in_specs=[pl.BlockSpec((1, t, NU * LANES), lambda b: (b, 0, 0))] + specs,
        out_specs=[out_spec] * 6,
        out_shape=[out_shape] * 6,
        compiler_params=_cparams(("parallel",)),
        name="dwa_decode",
    )(h3, *views)
    return [(outs[0], outs[1]), (outs[2], outs[3]), (outs[4], outs[5])]


def _even_post_kernel(tt, yt_ref, bonus_ref, *refs):
    n_ga = A_WIDTH // LANES
    ga_refs = refs[:n_ga]
    (gb0_ref, gb1_ref, lnxg_ref, lnxb_ref, ones_ref,
     o0_ref, l0_ref, o1_ref, l1_ref, o2_ref, l2_ref, ua_ref, ub_ref) = refs[n_ga:]
    ys = []
    for p in range(A_HEADS // 2):
        ys.append(yt_ref[0, p].T[0:tt, :])
    y = jnp.concatenate(ys, axis=-1)
    ones_bf = ones_ref[...]
    inv = 1.0 / A_HEAD_DIM
    mu = _seg_sum(y, ones_bf) * inv
    yc = y - mu
    var = _seg_sum(yc * yc, ones_bf) * inv
    ya = yc * lax.rsqrt(var + A_GN_EPS) * lnxg_ref[...] + lnxb_ref[...] + bonus_ref[0]
    gate_a = jnp.concatenate([g_ref[0] for g_ref in ga_refs], axis=-1)
    ua_ref[0] = (ya * _silu(gate_a)).astype(BF16)
    l0, l1, l2 = l0_ref[0], l1_ref[0], l2_ref[0]
    m = jnp.maximum(jnp.maximum(l0, l1), l2)
    e0, e1, e2 = jnp.exp(l0 - m), jnp.exp(l1 - m), jnp.exp(l2 - m)
    den = e0 + e1 + e2
    yb = (e0 / den) * o0_ref[0] + (e1 / den) * o1_ref[0] + (e2 / den) * o2_ref[0]
    gate_b = jnp.concatenate([gb0_ref[0], gb1_ref[0]], axis=-1)
    ub_ref[0] = (yb * _silu(gate_b)).astype(BF16)


def even_post(yt, bonus, h3, lnx_g, lnx_b, ones768, dwa):
    bsz, t, _ = h3.shape
    tt = min(t, 128)
    tp = max(tt, LANES)
    nt = t // tt
    (o0, l0), (o1, l1), (o2, l2) = dwa
    bspec = pl.BlockSpec((1, tt, B_OUT), lambda b, i: (b, i, 0))
    row_spec = pl.BlockSpec((1, A_WIDTH), lambda b, i: (0, 0))
    return pl.pallas_call(
        functools.partial(_even_post_kernel, tt),
        grid=(bsz, nt),
        in_specs=[pl.BlockSpec((1, A_HEADS // 2, LANES, tp), lambda b, i: (b, 0, 0, i)),
                  pl.BlockSpec((1, tt, A_WIDTH), lambda b, i: (b, i, 0)),
                  *[pl.BlockSpec((1, tt, LANES), functools.partial(lambda b, i, u: (b, i, u), u=U_GATE_A + u))
                    for u in range(A_WIDTH // LANES)],
                  pl.BlockSpec((1, tt, LANES), lambda b, i: (b, i, U_GATE_B)),
                  pl.BlockSpec((1, tt, LANES), lambda b, i: (b, i, U_GATE_B + 1)),
                  row_spec, row_spec,
                  pl.BlockSpec((A_WIDTH, A_WIDTH), lambda b, i: (0, 0))] + [bspec] * 6,
        out_specs=[pl.BlockSpec((1, tt, A_WIDTH), lambda b, i: (b, i, 0)), bspec],
        out_shape=[jax.ShapeDtypeStruct((bsz, t, A_WIDTH), BF16), jax.ShapeDtypeStruct((bsz, t, B_OUT), BF16)],
        compiler_params=_cparams(("parallel", "parallel")),
        name="even_post",
    )(yt, bonus, *([h3] * (A_WIDTH // LANES + 2)), lnx_g.reshape(1, -1), lnx_b.reshape(1, -1), ones768, o0, l0, o1, l1, o2, l2)


def _mem_attn_kernel(q_ref, g_ref, k_ref, v_ref, u_ref):
    dn = (((1,), (1,)), ((), ()))
    s = lax.dot_general(q_ref[0].astype(BF16), k_ref[0].astype(BF16), dn,
                        preferred_element_type=F32) * (M_HEAD_DIM ** -0.5)
    m = jnp.max(s, axis=-1, keepdims=True)
    e = jnp.exp(s - m)
    p = e / jnp.sum(e, axis=-1, keepdims=True)
    o = jnp.dot(p.astype(BF16), v_ref[0].astype(BF16), preferred_element_type=F32)
    u_ref[0] = (o * _silu(g_ref[0])).astype(BF16)


def mem_attn(h3, mkv, u_q, u_g):
    bsz, t, _ = h3.shape
    tq = min(t, 512)
    return pl.pallas_call(
        _mem_attn_kernel,
        grid=(bsz, t // tq, M_HEADS),
        in_specs=[pl.BlockSpec((1, tq, LANES), lambda b, i, hd: (b, i, u_q + hd)),
                  pl.BlockSpec((1, tq, LANES), lambda b, i, hd: (b, i, u_g + hd)),
                  pl.BlockSpec((1, M_TOKENS, LANES), lambda b, i, hd: (b, 0, hd)),
                  pl.BlockSpec((1, M_TOKENS, LANES), lambda b, i, hd: (b, 0, M_HEADS + hd))],
        out_specs=pl.BlockSpec((1, tq, LANES), lambda b, i, hd: (b, i, hd)),
        out_shape=jax.ShapeDtypeStruct((bsz, t, M_WIDTH), BF16),
        compiler_params=_cparams(("parallel", "parallel", "parallel")),
        name="mem_attn",
    )(h3, h3, mkv, mkv)


def _rope_kernel(pos0, ang_ref, cos_ref, sin_ref):
    rows = cos_ref.shape[0]
    base = pl.program_id(0) * rows
    pos = (lax.broadcasted_iota(jnp.int32, cos_ref.shape, 0) + base).astype(F32) + pos0
    ph = pos * ang_ref[...]
    cos_ref[...] = jnp.cos(ph)
    sin_ref[...] = jnp.sin(ph)


def rope_tables(rows, pos0):
    angle = 1.0 / (C_ROT_BASE ** jnp.linspace(0.0, 1.0, C_HEAD_DIM // 2, dtype=F32))
    ang = jnp.repeat(angle, 2).reshape(1, C_HEAD_DIM)
    tr = min(rows, 512)
    spec = pl.BlockSpec((tr, C_HEAD_DIM), lambda i: (i, 0))
    shape = jax.ShapeDtypeStruct((rows, C_HEAD_DIM), F32)
    return pl.pallas_call(
        functools.partial(_rope_kernel, float(pos0)),
        grid=(rows // tr,),
        in_specs=[pl.BlockSpec((1, C_HEAD_DIM), lambda i: (0, 0))],
        out_specs=[spec, spec],
        out_shape=[shape, shape],
        compiler_params=_cparams(("parallel",)),
        name="rope_tables",
    )(ang)


def _rot_pairs(z):
    even = lax.broadcasted_iota(jnp.int32, (z.shape[0], LANES), 1) % 2 == 0
    parts = []
    for blk in range(z.shape[1] // LANES):
        zb = z[:, blk * LANES:(blk + 1) * LANES]
        nxt = pltpu.roll(zb, LANES - 1, axis=1)
        prv = pltpu.roll(zb, 1, axis=1)
        parts.append(jnp.where(even, -nxt, prv))
    return jnp.concatenate(parts, axis=-1)


def _retention_kernel(tb, chunk, lg_ref, q_ref, k_ref, v_ref, g_ref, cos_ref, sin_ref, r0_ref,
                      u_ref, rout_ref, r_scr, *pad):
    c = pl.program_id(2)

    @pl.when(c == 0)
    def _():
        r_scr[...] = r0_ref[0, 0]

    if tb == C_CHUNK:
        q, k, v = q_ref[0], k_ref[0], v_ref[0]
    else:
        vals = []
        for src, buf in zip((q_ref, k_ref, v_ref), pad):
            buf[...] = jnp.zeros_like(buf)
            buf[0:tb, :] = src[0]
            vals.append(buf[...])
        q, k, v = vals
    lg = lg_ref[pl.program_id(1)]
    cos, sin = cos_ref[...], sin_ref[...]
    qr = q * cos + _rot_pairs(q) * sin
    kr = (k * cos + _rot_pairs(k) * sin) * (C_HEAD_DIM ** -0.5)
    ii = lax.broadcasted_iota(jnp.int32, (C_CHUNK, C_CHUNK), 0)
    jj = lax.broadcasted_iota(jnp.int32, (C_CHUNK, C_CHUNK), 1)
    diff = (ii - jj).astype(F32)
    dmat = jnp.where(diff >= 0, jnp.exp(lg * jnp.maximum(diff, 0.0)), 0.0)
    idx = lax.broadcasted_iota(jnp.int32, (C_CHUNK, 1), 0).astype(F32)
    xi = jnp.exp(lg * (idx + 1.0))
    zeta = jnp.exp(lg * (chunk - 1.0 - idx))
    g_chunk = jnp.exp(lg * jnp.full((1, 1), chunk, F32))
    dn = (((1,), (1,)), ((), ()))
    qb = qr.astype(BF16)
    vb = v.astype(BF16)
    sc = lax.dot_general(qb, kr.astype(BF16), dn, preferred_element_type=F32) * dmat
    r_old = r_scr[...]
    o = (jnp.dot(sc.astype(BF16), vb, preferred_element_type=F32)
         + jnp.dot(qb, r_old.astype(BF16), preferred_element_type=F32) * xi)
    kz_t = (kr * zeta).T.astype(BF16)
    r_scr[...] = r_old * g_chunk + jnp.dot(kz_t, vb, preferred_element_type=F32)
    o = o[0:tb]
    y = o * lax.rsqrt(jnp.mean(o * o, axis=-1, keepdims=True) + C_NORM_EPS)
    u_ref[0] = (y * _silu(g_ref[0])).astype(BF16)

    @pl.when(c == pl.num_programs(2) - 1)
    def _():
        rout_ref[0, 0] = r_scr[...]


def retention(h3, cos, sin, r0):
    bsz, t, _ = h3.shape
    tb = min(t, C_CHUNK)
    nc = t // tb
    chunk = float(tb)
    lg = jnp.log(1.0 - 2.0 ** (-5.0 - jnp.arange(C_HEADS, dtype=F32)))
    nq = C_WIDTH // C_HEAD_DIM

    def col_spec(off):
        return pl.BlockSpec((1, tb, C_HEAD_DIM), lambda b, hd, c: (b, c, off + hd))

    tab_spec = pl.BlockSpec((C_CHUNK, C_HEAD_DIM), lambda b, hd, c: (c, 0))
    st_spec = pl.BlockSpec((1, 1, C_HEAD_DIM, C_HEAD_DIM), lambda b, hd, c: (b, hd, 0, 0))
    scratch = [pltpu.VMEM((C_HEAD_DIM, C_HEAD_DIM), F32)]
    if tb != C_CHUNK:
        scratch += [pltpu.VMEM((C_CHUNK, C_HEAD_DIM), F32)] * 3
    return pl.pallas_call(
        functools.partial(_retention_kernel, tb, chunk),
        grid=(bsz, C_HEADS, nc),
        in_specs=[pl.BlockSpec(memory_space=pltpu.SMEM),
                  col_spec(0), col_spec(nq), col_spec(2 * nq), col_spec(3 * nq),
                  tab_spec, tab_spec, st_spec],
        out_specs=[pl.BlockSpec((1, tb, C_HEAD_DIM), lambda b, hd, c: (b, c, hd)), st_spec],
        out_shape=[jax.ShapeDtypeStruct((bsz, t, C_WIDTH), BF16), jax.ShapeDtypeStruct(r0.shape, F32)],
        scratch_shapes=scratch,
        compiler_params=_cparams(("parallel", "parallel", "arbitrary")),
        name="retention",
    )(lg, h3, h3, h3, h3, cos, sin, r0)


def _even_layer(x, x_bf, mkv, dwa_bufs, s0, shift0, w_in_bf, w_out_bf, e, ln_g, ln_b, rw, ones768):
    bsz, t, d = x.shape
    h = matmul(x_bf.reshape(bsz * t, d), w_in_bf, e)
    h3 = h.reshape(bsz, t, NU * LANES)
    mu, w0, wup, a0, aup, k_k, k_a, r_k, lnx_g, lnx_b = rw
    r, dcy, k2, kk, bb, vt, bonus = rwkv_prep(h3, shift0, (mu, w0, wup, a0, aup, k_k, k_a, r_k), ones768)
    yt, s_new = rwkv_scan((r, dcy, k2, kk, bb), vt, s0)
    if dwa_bufs is None:
        dwa = [dwa_prompt(h3, g) for g in range(B_GROUPS)]
    else:
        dwa = dwa_decode(h3, dwa_bufs)
    u_a, u_b = even_post(yt, bonus, h3, lnx_g, lnx_b, ones768, dwa)
    u_m = mem_attn(h3, mkv, U_QM_EVEN, U_GM_EVEN)
    x2, x2_bf = outproj_ln([u_a.reshape(bsz * t, -1), u_b.reshape(bsz * t, -1), u_m.reshape(bsz * t, -1)],
                           w_out_bf, e, x.reshape(bsz * t, d), ln_g, ln_b)
    rows = []
    for g, (win, _) in enumerate(B_CONFIGS):
        keep = t if dwa_bufs is not None else min(win, t)
        kg = h3[:, t - keep:, U_KB * LANES + g * B_OUT:U_KB * LANES + (g + 1) * B_OUT]
        vg = h3[:, t - keep:, U_VB * LANES + g * B_OUT:U_VB * LANES + (g + 1) * B_OUT]
        rows.append(jnp.stack([kg.reshape(bsz, keep, B_HEADS_PER_GROUP, B_HEAD_DIM),
                               vg.reshape(bsz, keep, B_HEADS_PER_GROUP, B_HEAD_DIM)], axis=2))
    return x2.reshape(bsz, t, d), x2_bf.reshape(bsz, t, d), s_new, h3[:, t - 1, :A_SHIFT_W], rows


def _odd_layer(x, x_bf, mkv, r0, tabs, w_in_bf, w_out_bf, o, ln_g, ln_b):
    bsz, t, d = x.shape
    h = matmul(x_bf.reshape(bsz * t, d), w_in_bf, o)
    h3 = h.reshape(bsz, t, NU * LANES)
    u_c, r_new = retention(h3, tabs[0], tabs[1], r0)
    u_m = mem_attn(h3, mkv, U_QM_ODD, U_GM_ODD)
    x2, x2_bf = outproj_ln([u_c.reshape(bsz * t, -1), u_m.reshape(bsz * t, -1)],
                           w_out_bf, o, x.reshape(bsz * t, d), ln_g, ln_b)
    return x2.reshape(bsz, t, d), x2_bf.reshape(bsz, t, d), r_new


def kernel(x_prompt, x_sample, state_rwkv, state_rwkv_shift, cache_dwa_g0, cache_dwa_g1, cache_dwa_g2, state_ret, cache_mem_kv, mem_prompt, w_in_even, w_out_even, w_in_odd, w_out_odd, w_mem_kv, ln_g, ln_b, rwkv_mu, rwkv_w0, rwkv_w_up, rwkv_a0, rwkv_a_up, rwkv_k_k, rwkv_k_a, rwkv_r_k, rwkv_lnx_g, rwkv_lnx_b):
    xp, xs = x_prompt, x_sample
    xp_bf, xs_bf = xp.astype(BF16), xs.astype(BF16)
    bp, tp_len, d = xp.shape
    bs, ts_len, _ = xs.shape
    dwa_cache = (cache_dwa_g0, cache_dwa_g1, cache_dwa_g2)
    ones768 = _block_ones(A_WIDTH, A_HEAD_DIM)
    mem_bf = mem_prompt.reshape(bp * M_TOKENS, d).astype(BF16)
    tabs_p = rope_tables(max(tp_len, C_CHUNK), 0)
    tabs_s = rope_tables(max(ts_len, C_CHUNK), PAST_LEN)
    rwkv_p, rwkv_s, shift_p, shift_s, ret_p, ret_s, mem_p = [], [], [], [], [], [], []
    dwa_p = [[] for _ in B_CONFIGS]
    dwa_s = [[] for _ in B_CONFIGS]
    w_in_even_bf = jnp.pad(w_in_even.astype(BF16), ((0, 0), (0, 0), (0, EVEN_IN_PAD - EVEN_IN)))
    w_out_even_bf = w_out_even.astype(BF16)
    w_in_odd_bf = w_in_odd.astype(BF16)
    w_out_odd_bf = w_out_odd.astype(BF16)
    w_mem_bf = w_mem_kv.astype(BF16)
    for l in range(DEPTH):
        mkv_p = matmul(mem_bf, w_mem_bf, l).reshape(bp, M_TOKENS, 2 * M_WIDTH)
        mem_p.append(mkv_p.reshape(bp, M_TOKENS, 2, M_HEADS, M_HEAD_DIM))
        mkv_s = cache_mem_kv[l].reshape(bs, M_TOKENS, 2 * M_WIDTH)
        if l % 2 == 0:
            e = l // 2
            w_in_bf, w_out_bf = w_in_even_bf, w_out_even_bf
            rw = (rwkv_mu[e], rwkv_w0[e], rwkv_w_up[e], rwkv_a0[e], rwkv_a_up[e], rwkv_k_k[e], rwkv_k_a[e],
                  rwkv_r_k[e], rwkv_lnx_g[e], rwkv_lnx_b[e])
            s0 = jnp.zeros((bp, A_HEADS, A_HEAD_DIM, A_HEAD_DIM), F32)
            sh0 = jnp.zeros((bp, A_SHIFT_W), F32)
            xp, xp_bf, st, sh, rows = _even_layer(xp, xp_bf, mkv_p, None, s0, sh0, w_in_bf, w_out_bf, e,
                                                  ln_g[l], ln_b[l], rw, ones768)
            rwkv_p.append(st)
            shift_p.append(sh)
            for g in range(B_GROUPS):
                dwa_p[g].append(rows[g])
            bufs = tuple(c[e] for c in dwa_cache)
            xs, xs_bf, st, sh, rows = _even_layer(xs, xs_bf, mkv_s, bufs, state_rwkv[e], state_rwkv_shift[e],
                                                  w_in_bf, w_out_bf, e, ln_g[l], ln_b[l], rw, ones768)
            rwkv_s.append(st)
            shift_s.append(sh)
            for g in range(B_GROUPS):
                dwa_s[g].append(rows[g])
        else:
            o = l // 2
            r0 = jnp.zeros((bp, C_HEADS, C_HEAD_DIM, C_HEAD_DIM), F32)
            xp, xp_bf, st = _odd_layer(xp, xp_bf, mkv_p, r0, tabs_p, w_in_odd_bf, w_out_odd_bf, o, ln_g[l], ln_b[l])
            ret_p.append(st)
            xs, xs_bf, st = _odd_layer(xs, xs_bf, mkv_s, state_ret[o], tabs_s, w_in_odd_bf, w_out_odd_bf, o,
                                       ln_g[l], ln_b[l])
            ret_s.append(st)
    return (xp, xs, jnp.stack(rwkv_p), jnp.stack(rwkv_s), jnp.stack(shift_p), jnp.stack(shift_s),
            jnp.stack(dwa_p[0]), jnp.stack(dwa_s[0]), jnp.stack(dwa_p[1]), jnp.stack(dwa_s[1]),
            jnp.stack(dwa_p[2]), jnp.stack(dwa_s[2]), jnp.stack(ret_p), jnp.stack(ret_s), jnp.stack(mem_p))
```

```python
import functools

import numpy as np
import jax
import jax.numpy as jnp
from jax import lax
from jax.experimental import pallas as pl
from jax.experimental.pallas import tpu as pltpu

F32 = jnp.float32
BF16 = jnp.bfloat16

D_MODEL = 2048
DEPTH = 4
PAST_LEN = 16384
ALPHA = (2 * DEPTH) ** 0.25
LN_EPS = 1e-5
A_HEADS = 12
A_HEAD_DIM = 64
A_WIDTH = A_HEADS * A_HEAD_DIM
A_LORA = 64
A_SHIFT_W = 3 * A_WIDTH + 2 * A_LORA
A_GN_EPS = 64e-5
B_CONFIGS = ((128, 1), (512, 4), (2048, 16))
B_GROUPS = 3
B_HEADS_PER_GROUP = 4
B_HEAD_DIM = 64
B_OUT = B_HEADS_PER_GROUP * B_HEAD_DIM
B_WIDTH = B_GROUPS * B_OUT
B_BLK = 128
C_HEADS = 6
C_HEAD_DIM = 256
C_WIDTH = C_HEADS * C_HEAD_DIM
C_CHUNK = 128
C_HEADS_PER_STEP = 2
C_ROT_BASE = 10000.0
C_NORM_EPS = 1e-6
M_TOKENS = 256
M_HEADS = 4
M_HEAD_DIM = 128
M_WIDTH = M_HEADS * M_HEAD_DIM
EVEN_IN = A_SHIFT_W + A_WIDTH + 3 * B_WIDTH + B_OUT + 2 * M_WIDTH
ODD_IN = 4 * C_WIDTH + 2 * M_WIDTH

LANES = 128
SUBLANES = 8
VMEM_LIMIT = 48 * 1024 * 1024

EVEN_IN_PAD = 7168
U_GATE_A = A_SHIFT_W // LANES
U_QB = U_GATE_A + A_WIDTH // LANES
U_KB = U_QB + B_WIDTH // LANES
U_VB = U_KB + B_WIDTH // LANES
U_GATE_B = U_VB + B_WIDTH // LANES
U_QM_EVEN = U_GATE_B + B_OUT // LANES
U_GM_EVEN = U_QM_EVEN + M_WIDTH // LANES
U_QM_ODD = 4 * C_WIDTH // LANES
U_GM_ODD = U_QM_ODD + M_WIDTH // LANES
NU = EVEN_IN_PAD // LANES


def _cparams(sem):
    return pltpu.CompilerParams(dimension_semantics=sem, vmem_limit_bytes=VMEM_LIMIT)


def _sigmoid(z):
    return 1.0 / (1.0 + jnp.exp(-z))


def _silu(z):
    return z * _sigmoid(z)


def _mm_kernel(x_ref, w_ref, o_ref):
    o_ref[...] = jnp.dot(x_ref[...], w_ref[0], preferred_element_type=F32)


def matmul(x, w, layer, tn=512):
    m, k = x.shape
    n = w.shape[2]
    tm = min(m, 1024)
    return pl.pallas_call(
        _mm_kernel,
        grid=(m // tm, n // tn),
        in_specs=[pl.BlockSpec((tm, k), lambda i, j: (i, 0)),
                  pl.BlockSpec((1, k, tn), lambda i, j: (layer, 0, j))],
        out_specs=pl.BlockSpec((tm, tn), lambda i, j: (i, j)),
        out_shape=jax.ShapeDtypeStruct((m, n), F32),
        compiler_params=_cparams(("parallel", "parallel")),
        name="matmul",
    )(x, w)


def _outproj_kernel(n_u, *refs):
    u_refs = refs[:n_u]
    w_refs = refs[n_u:2 * n_u]
    x_ref, g_ref, b_ref, o_ref, obf_ref = refs[2 * n_u:]
    acc = jnp.dot(u_refs[0][...], w_refs[0][0], preferred_element_type=F32)
    for u_ref, w_ref in zip(u_refs[1:], w_refs[1:]):
        acc = acc + jnp.dot(u_ref[...], w_ref[0], preferred_element_type=F32)
    z = ALPHA * x_ref[...] + acc
    mu = jnp.mean(z, axis=-1, keepdims=True)
    zc = z - mu
    var = jnp.mean(zc * zc, axis=-1, keepdims=True)
    y = zc * lax.rsqrt(var + LN_EPS) * g_ref[...] + b_ref[...]
    o_ref[...] = y
    obf_ref[...] = y.astype(BF16)


def outproj_ln(us, w_out, layer, x, g, b):
    m, d = x.shape
    tm = min(m, 256)
    n_u = len(us)
    in_specs = [pl.BlockSpec((tm, u.shape[1]), lambda i: (i, 0)) for u in us]
    row = 0
    for u in us:
        kw = u.shape[1]
        assert row % kw == 0
        in_specs.append(pl.BlockSpec((1, kw, d), functools.partial(lambda i, r: (layer, r, 0), r=row // kw)))
        row += kw
    assert row == w_out.shape[1]
    in_specs += [pl.BlockSpec((tm, d), lambda i: (i, 0)),
                 pl.BlockSpec((1, d), lambda i: (0, 0)),
                 pl.BlockSpec((1, d), lambda i: (0, 0))]
    return pl.pallas_call(
        functools.partial(_outproj_kernel, n_u),
        grid=(m // tm,),
        in_specs=in_specs,
        out_specs=[pl.BlockSpec((tm, d), lambda i: (i, 0)), pl.BlockSpec((tm, d), lambda i: (i, 0))],
        out_shape=[jax.ShapeDtypeStruct((m, d), F32), jax.ShapeDtypeStruct((m, d), BF16)],
        compiler_params=_cparams(("parallel",)),
        name="outproj_ln",
    )(*us, *([w_out] * n_u), x, g.reshape(1, d), b.reshape(1, d))


def _seg_sum(x, ones_bf):
    hi = x.astype(BF16)
    r1 = x - hi.astype(F32)
    mid = r1.astype(BF16)
    lo = (r1 - mid.astype(F32)).astype(BF16)
    return (jnp.dot(hi, ones_bf, preferred_element_type=F32)
            + jnp.dot(mid, ones_bf, preferred_element_type=F32)
            + jnp.dot(lo, ones_bf, preferred_element_type=F32))


def _block_ones(width, seg):
    i = np.arange(width) // seg
    return jnp.asarray((i[:, None] == i[None, :]).astype(np.float32), dtype=BF16)


def _rwkv_prep_kernel(tt, tp, h_ref, prev_ref, sh0_ref, mu_ref, w0_ref, wup_ref, a0_ref, aup_ref,
                      kk_ref_p, ka_ref_p, rk_ref_p, ones_ref,
                      r_o, d_o, k_o, kk_o, b_o, vt_o, bonus_o, *scratch):
    i = pl.program_id(1)
    x = h_ref[0]
    pr = prev_ref[0]
    prev_last = pr[pr.shape[0] - 1:pr.shape[0], :]
    first = jnp.where(i == 0, sh0_ref[0], prev_last)
    row = lax.broadcasted_iota(jnp.int32, x.shape, 0)
    if tt % SUBLANES == 0:
        rolled = pltpu.roll(x, 1, axis=0)
    else:
        rolled = jnp.concatenate([x[tt - 1:tt], x[:tt - 1]], axis=0)
    prev = jnp.where(row == 0, first, rolled)
    hs = x + (prev - x) * mu_ref[...]
    r = hs[:, 0:A_WIDTH]
    k = hs[:, A_WIDTH:2 * A_WIDTH]
    v = hs[:, 2 * A_WIDTH:3 * A_WIDTH]
    hw = hs[:, 3 * A_WIDTH:3 * A_WIDTH + A_LORA]
    ha = hs[:, 3 * A_WIDTH + A_LORA:A_SHIFT_W]
    zw = w0_ref[...] + jnp.dot(jnp.tanh(hw).astype(BF16), wup_ref[...], preferred_element_type=F32)
    nz = -zw
    softplus = jnp.maximum(nz, 0.0) + jnp.log(1.0 + jnp.exp(-jnp.abs(nz)))
    w_log = -softplus - 0.5
    decay = jnp.exp(-jnp.exp(w_log))
    a = _sigmoid(a0_ref[...] + jnp.dot(ha.astype(BF16), aup_ref[...], preferred_element_type=F32))
    ones_bf = ones_ref[...]
    kk = k * kk_ref_p[...]
    kk = kk * lax.rsqrt(jnp.maximum(_seg_sum(kk * kk, ones_bf), 1e-24))
    k2 = k * (1.0 + (a - 1.0) * ka_ref_p[...])
    bb = kk * a
    bonus_o[0] = _seg_sum(r * k2 * rk_ref_p[...], ones_bf) * v
    for hd in range(A_HEADS):
        sl = slice(hd * A_HEAD_DIM, (hd + 1) * A_HEAD_DIM)
        r_o[0, hd] = r[:, sl]
        d_o[0, hd] = decay[:, sl]
        k_o[0, hd] = k2[:, sl]
        kk_o[0, hd] = kk[:, sl]
        b_o[0, hd] = bb[:, sl]
    for p in range(A_HEADS // 2):
        vp = v[:, p * LANES:(p + 1) * LANES]
        if tt == tp:
            vt_o[0, p] = vp.T
        else:
            pad = scratch[0]
            pad[...] = jnp.zeros_like(pad)
            pad[0:tt, :] = vp
            vt_o[0, p] = pad[...].T


def rwkv_prep(h3, shift0, prm, ones768):
    bsz, t, _ = h3.shape
    tt = min(t, 128)
    tp = max(tt, LANES)
    nt = t // tt
    pr_rows = min(t, SUBLANES)
    pb = tt // pr_rows
    mu, w0, wup, a0, aup, k_k, k_a, r_k = prm
    row_spec = pl.BlockSpec((1, A_WIDTH), lambda b, i: (0, 0))
    lora_spec = pl.BlockSpec((A_LORA, A_WIDTH), lambda b, i: (0, 0))
    head_spec = pl.BlockSpec((1, A_HEADS, tt, A_HEAD_DIM), lambda b, i: (b, 0, i, 0))
    head_shape = jax.ShapeDtypeStruct((bsz, A_HEADS, t, A_HEAD_DIM), F32)
    scratch = [] if tt == tp else [pltpu.VMEM((tp, LANES), F32)]
    return pl.pallas_call(
        functools.partial(_rwkv_prep_kernel, tt, tp),
        grid=(bsz, nt),
        in_specs=[pl.BlockSpec((1, tt, A_SHIFT_W), lambda b, i: (b, i, 0)),
                  pl.BlockSpec((1, pr_rows, A_SHIFT_W), lambda b, i: (b, jnp.maximum(i * pb - 1, 0), 0)),
                  pl.BlockSpec((1, 1, A_SHIFT_W), lambda b, i: (b, 0, 0)),
                  pl.BlockSpec((1, A_SHIFT_W), lambda b, i: (0, 0)),
                  row_spec, lora_spec, row_spec, lora_spec, row_spec, row_spec, row_spec,
                  pl.BlockSpec((A_WIDTH, A_WIDTH), lambda b, i: (0, 0))],
        out_specs=[head_spec] * 5 + [
            pl.BlockSpec((1, A_HEADS // 2, LANES, tp), lambda b, i: (b, 0, 0, i)),
            pl.BlockSpec((1, tt, A_WIDTH), lambda b, i: (b, i, 0))],
        out_shape=[head_shape] * 5 + [
            jax.ShapeDtypeStruct((bsz, A_HEADS // 2, LANES, nt * tp), F32),
            jax.ShapeDtypeStruct((bsz, t, A_WIDTH), F32)],
        scratch_shapes=scratch,
        compiler_params=_cparams(("parallel", "parallel")),
        name="rwkv_prep",
    )(h3, h3, shift0.reshape(bsz, 1, A_SHIFT_W), mu.reshape(1, -1), w0.reshape(1, -1), wup.astype(BF16),
      a0.reshape(1, -1), aup.astype(BF16), k_k.reshape(1, -1), k_a.reshape(1, -1), r_k.reshape(1, -1), ones768)


def _rwkv_scan_kernel(steps, r_ref, d_ref, k_ref, kk_ref, b_ref, vt_ref, s0_ref, yt_ref, sout_ref, s_scr, p_scr):
    c = pl.program_id(1)

    @pl.when(c == 0)
    def _():
        s_scr[...] = s0_ref[0]

    tp = vt_ref.shape[3]
    lane = lax.broadcasted_iota(jnp.int32, (A_HEAD_DIM, tp), 1)
    yt_ref[...] = jnp.zeros_like(yt_ref)
    p_scr[...] = jnp.zeros_like(p_scr)
    ones_bf = jnp.ones((A_HEAD_DIM, tp), BF16)

    def write_y(t):
        msk = lane == t
        for hd in range(A_HEADS):
            y = jnp.dot(p_scr[hd], ones_bf, preferred_element_type=F32)
            yt_ref[0, hd] = jnp.where(msk, y, yt_ref[0, hd])

    def body(t, carry):
        write_y(t - 1)
        msk = lane == t
        sas, vcols = [], []
        for hd in range(A_HEADS):
            sas.append(-jnp.sum(s_scr[hd] * kk_ref[0, hd, pl.ds(t, 1), :], axis=-1, keepdims=True))
            vcols.append(jnp.sum(jnp.where(msk, vt_ref[0, hd], 0.0), axis=-1, keepdims=True))
        for hd in range(A_HEADS):
            s = (s_scr[hd] * d_ref[0, hd, pl.ds(t, 1), :] + sas[hd] * b_ref[0, hd, pl.ds(t, 1), :]
                 + vcols[hd] * k_ref[0, hd, pl.ds(t, 1), :])
            s_scr[hd] = s
            p_scr[hd] = (s * r_ref[0, hd, pl.ds(t, 1), :]).astype(BF16)
        return carry

    lax.fori_loop(0, steps, body, 0, unroll=4)
    write_y(steps - 1)

    @pl.when(c == pl.num_programs(1) - 1)
    def _():
        sout_ref[0] = s_scr[...]


def rwkv_scan(ops, vt, s0):
    r, d, k, kk, b = ops
    bsz, _, t, _ = r.shape
    tt = min(t, 128)
    tp = max(tt, LANES)
    nt = t // tt
    vt = vt.reshape(bsz, A_HEADS, A_HEAD_DIM, nt * tp)
    head_spec = pl.BlockSpec((1, A_HEADS, tt, A_HEAD_DIM), lambda b_, i: (b_, 0, i, 0))
    vt_spec = pl.BlockSpec((1, A_HEADS, A_HEAD_DIM, tp), lambda b_, i: (b_, 0, 0, i))
    st_spec = pl.BlockSpec((1, A_HEADS, A_HEAD_DIM, A_HEAD_DIM), lambda b_, i: (b_, 0, 0, 0))
    yt, s_new = pl.pallas_call(
        functools.partial(_rwkv_scan_kernel, tt),
        grid=(bsz, nt),
        in_specs=[head_spec] * 5 + [vt_spec, st_spec],
        out_specs=[vt_spec, st_spec],
        out_shape=[jax.ShapeDtypeStruct(vt.shape, F32), jax.ShapeDtypeStruct(s0.shape, F32)],
        scratch_shapes=[pltpu.VMEM((A_HEADS, A_HEAD_DIM, A_HEAD_DIM), F32),
                        pltpu.VMEM((A_HEADS, A_HEAD_DIM, A_HEAD_DIM), BF16)],
        compiler_params=_cparams(("parallel", "arbitrary")),
        name="rwkv_scan",
    )(r, d, k, kk, b, vt, s0)
    return yt.reshape(bsz, A_HEADS // 2, LANES, nt * tp), s_new


def _dwa_prompt_kernel(dil, *refs):
    npair = B_OUT // LANES
    in_refs = [refs[5 * hp:5 * hp + 5] for hp in range(npair)]
    out_refs = refs[5 * npair:]
    n = pl.program_id(1)
    i = lax.broadcasted_iota(jnp.int32, (B_BLK, B_BLK), 0)
    j = lax.broadcasted_iota(jnp.int32, (B_BLK, B_BLK), 1)
    mask_prev = j >= i + jnp.where(n > 0, 0, B_BLK)
    mask_cur = j <= i
    neg = -jnp.inf
    scale = B_HEAD_DIM ** -0.5
    dn = (((1,), (1,)), ((), ()))

    def residue(r, carry):
        rows = pl.ds(r, B_BLK, stride=dil) if dil > 1 else pl.ds(0, B_BLK)
        for hp, pair_refs in enumerate(in_refs):
            q2, kc2, kp2, vc2, vp2 = (ref[0, rows, :] for ref in pair_refs)
            outs, lses = [], []
            for hh in range(LANES // B_HEAD_DIM):
                sl = slice(hh * B_HEAD_DIM, (hh + 1) * B_HEAD_DIM)
                q = q2[:, sl].astype(BF16)
                sp = lax.dot_general(q, kp2[:, sl].astype(BF16), dn, preferred_element_type=F32) * scale
                sc = lax.dot_general(q, kc2[:, sl].astype(BF16), dn, preferred_element_type=F32) * scale
                sp = jnp.where(mask_prev, sp, neg)
                sc = jnp.where(mask_cur, sc, neg)
                m = jnp.maximum(jnp.max(sp, axis=-1, keepdims=True), jnp.max(sc, axis=-1, keepdims=True))
                ep = jnp.exp(sp - m)
                ec = jnp.exp(sc - m)
                l = jnp.sum(ep, axis=-1, keepdims=True) + jnp.sum(ec, axis=-1, keepdims=True)
                o = (jnp.dot((ep / l).astype(BF16), vp2[:, sl].astype(BF16), preferred_element_type=F32)
                     + jnp.dot((ec / l).astype(BF16), vc2[:, sl].astype(BF16), preferred_element_type=F32))
                outs.append(o)
                lses.append(jnp.broadcast_to(m + jnp.log(l), (B_BLK, B_HEAD_DIM)))
            out_refs[2 * hp][0, rows, :] = jnp.concatenate(outs, axis=-1)
            out_refs[2 * hp + 1][0, rows, :] = jnp.concatenate(lses, axis=-1)
        return carry

    lax.fori_loop(0, dil, residue, 0, unroll=min(dil, 2))


def dwa_prompt(h3, g):
    bsz, t, _ = h3.shape
    win, dil = B_CONFIGS[g]
    assert win == B_BLK * dil and t % win == 0
    npair = B_OUT // LANES

    def in_spec(unit, prev, hp):
        def imap(b, n):
            nn = jnp.maximum(n - 1, 0) if prev else n
            return (b, nn, unit + npair * g + hp)
        return pl.BlockSpec((1, win, LANES), imap)

    in_specs = []
    for hp in range(npair):
        in_specs += [in_spec(U_QB, False, hp), in_spec(U_KB, False, hp), in_spec(U_KB, True, hp),
                     in_spec(U_VB, False, hp), in_spec(U_VB, True, hp)]
    out_spec = pl.BlockSpec((1, win, LANES), lambda b, n: (b, n, 0))
    out_shape = jax.ShapeDtypeStruct((bsz, t, LANES), F32)
    outs = pl.pallas_call(
        functools.partial(_dwa_prompt_kernel, dil),
        grid=(bsz, t // win),
        in_specs=in_specs,
        out_specs=[out_spec] * (2 * npair),
        out_shape=[out_shape] * (2 * npair),
        compiler_params=_cparams(("parallel", "parallel")),
        name="dwa_prompt",
    )(*([h3] * len(in_specs)))
    return list(outs)


def _dwa_decode_kernel(t_new, h_ref, c0_ref, c1_ref, c2_ref, *out_refs):
    kv_w = 2 * B_OUT
    scale = B_HEAD_DIM ** -0.5
    dn = (((1,), (1,)), ((), ()))
    head_of_lane = lax.broadcasted_iota(jnp.int32, (SUBLANES, B_OUT), 1) // B_HEAD_DIM
    sub = lax.broadcasted_iota(jnp.int32, (SUBLANES, B_OUT), 0)
    own = head_of_lane == sub
    m_idx = lax.broadcasted_iota(jnp.int32, (SUBLANES, B_BLK), 1)
    neg = -jnp.inf
    hrow = h_ref[0]
    for g, c_ref in enumerate((c0_ref, c1_ref, c2_ref)):
        _, dil = B_CONFIGS[g]
        g_refs = out_refs[4 * g:4 * g + 4]
        qs = hrow[:, U_QB * LANES + g * B_OUT:U_QB * LANES + (g + 1) * B_OUT]
        ks = hrow[:, U_KB * LANES + g * B_OUT:U_KB * LANES + (g + 1) * B_OUT]
        vs = hrow[:, U_VB * LANES + g * B_OUT:U_VB * LANES + (g + 1) * B_OUT]
        for t in range(t_new):
            res = t % dil
            kbuf = c_ref[0][:, res * kv_w:res * kv_w + B_OUT]
            vbuf = c_ref[0][:, res * kv_w + B_OUT:(res + 1) * kv_w]
            qbd = jnp.where(own, jnp.broadcast_to(qs[t:t + 1], (SUBLANES, B_OUT)), 0.0)
            s_buf = lax.dot_general(qbd.astype(BF16), kbuf.astype(BF16), dn, preferred_element_type=F32) * scale
            s_buf = jnp.where(m_idx * dil + res >= t, s_buf, neg)
            new_rows = [n for n in range(t + 1) if (t - n) % dil == 0]
            kq = qbd.astype(BF16).astype(F32)
            s_new = [jnp.sum(kq * ks[n:n + 1].astype(BF16).astype(F32), axis=-1, keepdims=True) * scale
                     for n in new_rows]
            m = jnp.max(s_buf, axis=-1, keepdims=True)
            for sn in s_new:
                m = jnp.maximum(m, sn)
            e_buf = jnp.exp(s_buf - m)
            e_new = [jnp.exp(sn - m) for sn in s_new]
            l = jnp.sum(e_buf, axis=-1, keepdims=True)
            for en in e_new:
                l = l + en
            o = jnp.dot((e_buf / l).astype(BF16), vbuf.astype(BF16), preferred_element_type=F32)
            for n, en in zip(new_rows, e_new):
                o = o + (en / l).astype(BF16).astype(F32) * vs[n:n + 1].astype(BF16).astype(F32)
            lse = jnp.broadcast_to(m + jnp.log(l), (SUBLANES, B_OUT))
            o_row = jnp.sum(jnp.where(own, o, 0.0), axis=0, keepdims=True)
            lse_row = jnp.sum(jnp.where(own, lse, 0.0), axis=0, keepdims=True)
            for hp in range(B_OUT // LANES):
                g_refs[2 * hp][0, t:t + 1, :] = o_row[:, hp * LANES:(hp + 1) * LANES]
                g_refs[2 * hp + 1][0, t:t + 1, :] = lse_row[:, hp * LANES:(hp + 1) * LANES]


def dwa_decode(h3, caches):
    bsz, t, _ = h3.shape
    views, specs = [], []
    for g, (win, dil) in enumerate(B_CONFIGS):
        assert caches[g].shape[1] == win and (dil == 1 or t <= dil)
        views.append(caches[g].reshape(bsz, win // dil, dil * 2 * B_OUT))
        used = min(dil, t) * 2 * B_OUT
        specs.append(pl.BlockSpec((1, B_BLK, used), lambda b: (b, 0, 0)))
    out_spec = pl.BlockSpec((1, t, LANES), lambda b: (b, 0, 0))
    out_shape = jax.ShapeDtypeStruct((bsz, t, LANES), F32)
    n_out = 4 * B_GROUPS
    outs = pl.pallas_call(
        functools.partial(_dwa_decode_kernel, t),
        grid=(bsz,),
        in_specs=[pl.BlockSpec((1, t, NU * LANES), lambda b: (b, 0, 0))] + specs,
        out_specs=[out_spec] * n_out,
        out_shape=[out_shape] * n_out,
        compiler_params=_cparams(("parallel",)),
        name="dwa_decode",
    )(h3, *views)
    return [list(outs[4 * g:4 * g + 4]) for g in range(B_GROUPS)]


def _even_post_kernel(tt, yt_ref, bonus_ref, *refs):
    n_ga = A_WIDTH // LANES
    ga_refs = refs[:n_ga]
    gb0_ref, gb1_ref, lnxg_ref, lnxb_ref, ones_ref = refs[n_ga:n_ga + 5]
    dwa_refs = refs[n_ga + 5:n_ga + 5 + 4 * B_GROUPS]
    ua_ref, ub_ref = refs[n_ga + 5 + 4 * B_GROUPS:]
    ys = []
    for p in range(A_HEADS // 2):
        ys.append(yt_ref[0, p].T[0:tt, :])
    y = jnp.concatenate(ys, axis=-1)
    ones_bf = ones_ref[...]
    inv = 1.0 / A_HEAD_DIM
    mu = _seg_sum(y, ones_bf) * inv
    yc = y - mu
    var = _seg_sum(yc * yc, ones_bf) * inv
    ya = yc * lax.rsqrt(var + A_GN_EPS) * lnxg_ref[...] + lnxb_ref[...] + bonus_ref[0]
    gate_a = jnp.concatenate([g_ref[0] for g_ref in ga_refs], axis=-1)
    ua_ref[0] = (ya * _silu(gate_a)).astype(BF16)
    ybs = []
    for hp, gb_ref in enumerate((gb0_ref, gb1_ref)):
        o0, l0, o1, l1, o2, l2 = (dwa_refs[4 * g + 2 * hp + k][0] for g in range(B_GROUPS) for k in range(2))
        m = jnp.maximum(jnp.maximum(l0, l1), l2)
        e0, e1, e2 = jnp.exp(l0 - m), jnp.exp(l1 - m), jnp.exp(l2 - m)
        den = e0 + e1 + e2
        yb = (e0 / den) * o0 + (e1 / den) * o1 + (e2 / den) * o2
        ybs.append(yb * _silu(gb_ref[0]))
    ub_ref[0] = jnp.concatenate(ybs, axis=-1).astype(BF16)


def even_post(yt, bonus, h3, lnx_g, lnx_b, ones768, dwa):
    bsz, t, _ = h3.shape
    tt = min(t, 128)
    tp = max(tt, LANES)
    nt = t // tt
    dwa_flat = [a for grp in dwa for a in grp]
    bspec = pl.BlockSpec((1, tt, B_OUT), lambda b, i: (b, i, 0))
    pspec = pl.BlockSpec((1, tt, LANES), lambda b, i: (b, i, 0))
    row_spec = pl.BlockSpec((1, A_WIDTH), lambda b, i: (0, 0))
    return pl.pallas_call(
        functools.partial(_even_post_kernel, tt),
        grid=(bsz, nt),
        in_specs=[pl.BlockSpec((1, A_HEADS // 2, LANES, tp), lambda b, i: (b, 0, 0, i)),
                  pl.BlockSpec((1, tt, A_WIDTH), lambda b, i: (b, i, 0)),
                  *[pl.BlockSpec((1, tt, LANES), functools.partial(lambda b, i, u: (b, i, u), u=U_GATE_A + u))
                    for u in range(A_WIDTH // LANES)],
                  pl.BlockSpec((1, tt, LANES), lambda b, i: (b, i, U_GATE_B)),
                  pl.BlockSpec((1, tt, LANES), lambda b, i: (b, i, U_GATE_B + 1)),
                  row_spec, row_spec,
                  pl.BlockSpec((A_WIDTH, A_WIDTH), lambda b, i: (0, 0))] + [pspec] * len(dwa_flat),
        out_specs=[pl.BlockSpec((1, tt, A_WIDTH), lambda b, i: (b, i, 0)), bspec],
        out_shape=[jax.ShapeDtypeStruct((bsz, t, A_WIDTH), BF16), jax.ShapeDtypeStruct((bsz, t, B_OUT), BF16)],
        compiler_params=_cparams(("parallel", "parallel")),
        name="even_post",
    )(yt, bonus, *([h3] * (A_WIDTH // LANES + 2)), lnx_g.reshape(1, -1), lnx_b.reshape(1, -1), ones768, *dwa_flat)


def _mem_attn_kernel(q_ref, g_ref, k_ref, v_ref, u_ref):
    dn = (((1,), (1,)), ((), ()))
    s = lax.dot_general(q_ref[0].astype(BF16), k_ref[0].astype(BF16), dn,
                        preferred_element_type=F32) * (M_HEAD_DIM ** -0.5)
    m = jnp.max(s, axis=-1, keepdims=True)
    e = jnp.exp(s - m)
    p = e / jnp.sum(e, axis=-1, keepdims=True)
    o = jnp.dot(p.astype(BF16), v_ref[0].astype(BF16), preferred_element_type=F32)
    u_ref[0] = (o * _silu(g_ref[0])).astype(BF16)


def mem_attn(h3, mkv, u_q, u_g):
    bsz, t, _ = h3.shape
    tq = min(t, 512)
    return pl.pallas_call(
        _mem_attn_kernel,
        grid=(bsz, t // tq, M_HEADS),
        in_specs=[pl.BlockSpec((1, tq, LANES), lambda b, i, hd: (b, i, u_q + hd)),
                  pl.BlockSpec((1, tq, LANES), lambda b, i, hd: (b, i, u_g + hd)),
                  pl.BlockSpec((1, M_TOKENS, LANES), lambda b, i, hd: (b, 0, hd)),
                  pl.BlockSpec((1, M_TOKENS, LANES), lambda b, i, hd: (b, 0, M_HEADS + hd))],
        out_specs=pl.BlockSpec((1, tq, LANES), lambda b, i, hd: (b, i, hd)),
        out_shape=jax.ShapeDtypeStruct((bsz, t, M_WIDTH), BF16),
        compiler_params=_cparams(("parallel", "parallel", "parallel")),
        name="mem_attn",
    )(h3, h3, mkv, mkv)


def _rope_kernel(pos0, ang_ref, cos_ref, sin_ref):
    rows = cos_ref.shape[0]
    base = pl.program_id(0) * rows
    pos = (lax.broadcasted_iota(jnp.int32, cos_ref.shape, 0) + base).astype(F32) + pos0
    ph = pos * ang_ref[...]
    cos_ref[...] = jnp.cos(ph)
    sin_ref[...] = jnp.sin(ph)


def rope_tables(rows, pos0):
    angle = 1.0 / (C_ROT_BASE ** jnp.linspace(0.0, 1.0, C_HEAD_DIM // 2, dtype=F32))
    ang = jnp.repeat(angle, 2).reshape(1, C_HEAD_DIM)
    tr = min(rows, 512)
    spec = pl.BlockSpec((tr, C_HEAD_DIM), lambda i: (i, 0))
    shape = jax.ShapeDtypeStruct((rows, C_HEAD_DIM), F32)
    return pl.pallas_call(
        functools.partial(_rope_kernel, float(pos0)),
        grid=(rows // tr,),
        in_specs=[pl.BlockSpec((1, C_HEAD_DIM), lambda i: (0, 0))],
        out_specs=[spec, spec],
        out_shape=[shape, shape],
        compiler_params=_cparams(("parallel",)),
        name="rope_tables",
    )(ang)


def _rot_pairs(z):
    even = lax.broadcasted_iota(jnp.int32, (z.shape[0], LANES), 1) % 2 == 0
    parts = []
    for blk in range(z.shape[1] // LANES):
        zb = z[:, blk * LANES:(blk + 1) * LANES]
        nxt = pltpu.roll(zb, LANES - 1, axis=1)
        prv = pltpu.roll(zb, 1, axis=1)
        parts.append(jnp.where(even, -nxt, prv))
    return jnp.concatenate(parts, axis=-1)


def _retention_kernel(tb, chunk, lg_ref, q_ref, k_ref, v_ref, g_ref, cos_ref, sin_ref, r0_ref,
                      u_ref, rout_ref, r_scr, *pad):
    c = pl.program_id(2)

    @pl.when(c == 0)
    def _():
        r_scr[...] = r0_ref[0]

    if tb == C_CHUNK:
        q2, k2, v2 = q_ref[0], k_ref[0], v_ref[0]
    else:
        vals = []
        for src, buf in zip((q_ref, k_ref, v_ref), pad):
            buf[...] = jnp.zeros_like(buf)
            buf[0:tb, :] = src[0]
            vals.append(buf[...])
        q2, k2, v2 = vals
    cos, sin = cos_ref[...], sin_ref[...]
    ii = lax.broadcasted_iota(jnp.int32, (C_CHUNK, C_CHUNK), 0)
    jj = lax.broadcasted_iota(jnp.int32, (C_CHUNK, C_CHUNK), 1)
    diff = (ii - jj).astype(F32)
    idx = lax.broadcasted_iota(jnp.int32, (C_CHUNK, 1), 0).astype(F32)
    dn = (((1,), (1,)), ((), ()))
    us = []
    for hh in range(C_HEADS_PER_STEP):
        sl = slice(hh * C_HEAD_DIM, (hh + 1) * C_HEAD_DIM)
        q, k, v = q2[:, sl], k2[:, sl], v2[:, sl]
        lg = lg_ref[pl.program_id(1) * C_HEADS_PER_STEP + hh]
        qr = q * cos + _rot_pairs(q) * sin
        kr = (k * cos + _rot_pairs(k) * sin) * (C_HEAD_DIM ** -0.5)
        dmat = jnp.where(diff >= 0, jnp.exp(lg * jnp.maximum(diff, 0.0)), 0.0)
        xi = jnp.exp(lg * (idx + 1.0))
        zeta = jnp.exp(lg * (chunk - 1.0 - idx))
        g_chunk = jnp.exp(lg * jnp.full((1, 1), chunk, F32))
        qb = qr.astype(BF16)
        vb = v.astype(BF16)
        sc = lax.dot_general(qb, kr.astype(BF16), dn, preferred_element_type=F32) * dmat
        r_old = r_scr[hh]
        o = (jnp.dot(sc.astype(BF16), vb, preferred_element_type=F32)
             + jnp.dot(qb, r_old.astype(BF16), preferred_element_type=F32) * xi)
        kz_t = (kr * zeta).T.astype(BF16)
        r_scr[hh] = r_old * g_chunk + jnp.dot(kz_t, vb, preferred_element_type=F32)
        o = o[0:tb]
        y = o * lax.rsqrt(jnp.mean(o * o, axis=-1, keepdims=True) + C_NORM_EPS)
        us.append(y * _silu(g_ref[0][:, sl]))
    u_ref[0] = jnp.concatenate(us, axis=-1).astype(BF16)

    @pl.when(c == pl.num_programs(2) - 1)
    def _():
        rout_ref[0] = r_scr[...]


def retention(h3, cos, sin, r0):
    bsz, t, _ = h3.shape
    tb = min(t, C_CHUNK)
    nc = t // tb
    chunk = float(tb)
    lg = jnp.log(1.0 - 2.0 ** (-5.0 - jnp.arange(C_HEADS, dtype=F32)))
    hps = C_HEADS_PER_STEP
    width = hps * C_HEAD_DIM
    nq = C_WIDTH // width

    def col_spec(off):
        return pl.BlockSpec((1, tb, width), lambda b, hd, c: (b, c, off + hd))

    tab_spec = pl.BlockSpec((C_CHUNK, C_HEAD_DIM), lambda b, hd, c: (c, 0))
    st_spec = pl.BlockSpec((1, hps, C_HEAD_DIM, C_HEAD_DIM), lambda b, hd, c: (b, hd, 0, 0))
    scratch = [pltpu.VMEM((hps, C_HEAD_DIM, C_HEAD_DIM), F32)]
    if tb != C_CHUNK:
        scratch += [pltpu.VMEM((C_CHUNK, width), F32)] * 3
    return pl.pallas_call(
        functools.partial(_retention_kernel, tb, chunk),
        grid=(bsz, C_HEADS // hps, nc),
        in_specs=[pl.BlockSpec(memory_space=pltpu.SMEM),
                  col_spec(0), col_spec(nq), col_spec(2 * nq), col_spec(3 * nq),
                  tab_spec, tab_spec, st_spec],
        out_specs=[pl.BlockSpec((1, tb, width), lambda b, hd, c: (b, c, hd)), st_spec],
        out_shape=[jax.ShapeDtypeStruct((bsz, t, C_WIDTH), BF16), jax.ShapeDtypeStruct(r0.shape, F32)],
        scratch_shapes=scratch,
        compiler_params=_cparams(("parallel", "parallel", "arbitrary")),
        name="retention",
    )(lg, h3, h3, h3, h3, cos, sin, r0)


def _even_layer(x, x_bf, mkv, dwa_bufs, s0, shift0, w_in_bf, w_out_bf, e, ln_g, ln_b, rw, ones768):
    bsz, t, d = x.shape
    h = matmul(x_bf.reshape(bsz * t, d), w_in_bf, e)
    h3 = h.reshape(bsz, t, NU * LANES)
    mu, w0, wup, a0, aup, k_k, k_a, r_k, lnx_g, lnx_b = rw
    r, dcy, k2, kk, bb, vt, bonus = rwkv_prep(h3, shift0, (mu, w0, wup, a0, aup, k_k, k_a, r_k), ones768)
    yt, s_new = rwkv_scan((r, dcy, k2, kk, bb), vt, s0)
    if dwa_bufs is None:
        dwa = [dwa_prompt(h3, g) for g in range(B_GROUPS)]
    else:
        dwa = dwa_decode(h3, dwa_bufs)
    u_a, u_b = even_post(yt, bonus, h3, lnx_g, lnx_b, ones768, dwa)
    u_m = mem_attn(h3, mkv, U_QM_EVEN, U_GM_EVEN)
    x2, x2_bf = outproj_ln([u_a.reshape(bsz * t, -1), u_b.reshape(bsz * t, -1), u_m.reshape(bsz * t, -1)],
                           w_out_bf, e, x.reshape(bsz * t, d), ln_g, ln_b)
    rows = []
    for g, (win, _) in enumerate(B_CONFIGS):
        keep = t if dwa_bufs is not None else min(win, t)
        kg = h3[:, t - keep:, U_KB * LANES + g * B_OUT:U_KB * LANES + (g + 1) * B_OUT]
        vg = h3[:, t - keep:, U_VB * LANES + g * B_OUT:U_VB * LANES + (g + 1) * B_OUT]
        rows.append(jnp.stack([kg.reshape(bsz, keep, B_HEADS_PER_GROUP, B_HEAD_DIM),
                               vg.reshape(bsz, keep, B_HEADS_PER_GROUP, B_HEAD_DIM)], axis=2))
    return x2.reshape(bsz, t, d), x2_bf.reshape(bsz, t, d), s_new, h3[:, t - 1, :A_SHIFT_W], rows


def _odd_layer(x, x_bf, mkv, r0, tabs, w_in_bf, w_out_bf, o, ln_g, ln_b):
    bsz, t, d = x.shape
    h = matmul(x_bf.reshape(bsz * t, d), w_in_bf, o)
    h3 = h.reshape(bsz, t, NU * LANES)
    u_c, r_new = retention(h3, tabs[0], tabs[1], r0)
    u_m = mem_attn(h3, mkv, U_QM_ODD, U_GM_ODD)
    x2, x2_bf = outproj_ln([u_c.reshape(bsz * t, -1), u_m.reshape(bsz * t, -1)],
                           w_out_bf, o, x.reshape(bsz * t, d), ln_g, ln_b)
    return x2.reshape(bsz, t, d), x2_bf.reshape(bsz, t, d), r_new


def kernel(x_prompt, x_sample, state_rwkv, state_rwkv_shift, cache_dwa_g0, cache_dwa_g1, cache_dwa_g2, state_ret, cache_mem_kv, mem_prompt, w_in_even, w_out_even, w_in_odd, w_out_odd, w_mem_kv, ln_g, ln_b, rwkv_mu, rwkv_w0, rwkv_w_up, rwkv_a0, rwkv_a_up, rwkv_k_k, rwkv_k_a, rwkv_r_k, rwkv_lnx_g, rwkv_lnx_b):
    xp, xs = x_prompt, x_sample
    xp_bf, xs_bf = xp.astype(BF16), xs.astype(BF16)
    bp, tp_len, d = xp.shape
    bs, ts_len, _ = xs.shape
    dwa_cache = (cache_dwa_g0, cache_dwa_g1, cache_dwa_g2)
    ones768 = _block_ones(A_WIDTH, A_HEAD_DIM)
    mem_bf = mem_prompt.reshape(bp * M_TOKENS, d).astype(BF16)
    tabs_p = rope_tables(max(tp_len, C_CHUNK), 0)
    tabs_s = rope_tables(max(ts_len, C_CHUNK), PAST_LEN)
    rwkv_p, rwkv_s, shift_p, shift_s, ret_p, ret_s, mem_p = [], [], [], [], [], [], []
    dwa_p = [[] for _ in B_CONFIGS]
    dwa_s = [[] for _ in B_CONFIGS]
    w_in_even_bf = jnp.pad(w_in_even.astype(BF16), ((0, 0), (0, 0), (0, EVEN_IN_PAD - EVEN_IN)))
    w_out_even_bf = w_out_even.astype(BF16)
    w_in_odd_bf = w_in_odd.astype(BF16)
    w_out_odd_bf = w_out_odd.astype(BF16)
    w_mem_bf = w_mem_kv.astype(BF16)
    for l in range(DEPTH):
        mkv_p = matmul(mem_bf, w_mem_bf, l).reshape(bp, M_TOKENS, 2 * M_WIDTH)
        mem_p.append(mkv_p.reshape(bp, M_TOKENS, 2, M_HEADS, M_HEAD_DIM))
        mkv_s = cache_mem_kv[l].reshape(bs, M_TOKENS, 2 * M_WIDTH)
        if l % 2 == 0:
            e = l // 2
            w_in_bf, w_out_bf = w_in_even_bf, w_out_even_bf
            rw = (rwkv_mu[e], rwkv_w0[e], rwkv_w_up[e], rwkv_a0[e], rwkv_a_up[e], rwkv_k_k[e], rwkv_k_a[e],
                  rwkv_r_k[e], rwkv_lnx_g[e], rwkv_lnx_b[e])
            s0 = jnp.zeros((bp, A_HEADS, A_HEAD_DIM, A_HEAD_DIM), F32)
            sh0 = jnp.zeros((bp, A_SHIFT_W), F32)
            xp, xp_bf, st, sh, rows = _even_layer(xp, xp_bf, mkv_p, None, s0, sh0, w_in_bf, w_out_bf, e,
                                                  ln_g[l], ln_b[l], rw, ones768)
            rwkv_p.append(st)
            shift_p.append(sh)
            for g in range(B_GROUPS):
                dwa_p[g].append(rows[g])
            bufs = tuple(c[e] for c in dwa_cache)
            xs, xs_bf, st, sh, rows = _even_layer(xs, xs_bf, mkv_s, bufs, state_rwkv[e], state_rwkv_shift[e],
                                                  w_in_bf, w_out_bf, e, ln_g[l], ln_b[l], rw, ones768)
            rwkv_s.append(st)
            shift_s.append(sh)
            for g in range(B_GROUPS):
                dwa_s[g].append(rows[g])
        else:
            o = l // 2
            r0 = jnp.zeros((bp, C_HEADS, C_HEAD_DIM, C_HEAD_DIM), F32)
            xp, xp_bf, st = _odd_layer(xp, xp_bf, mkv_p, r0, tabs_p, w_in_odd_bf, w_out_odd_bf, o, ln_g[l], ln_b[l])
            ret_p.append(st)
            xs, xs_bf, st = _odd_layer(xs, xs_bf, mkv_s, state_ret[o], tabs_s, w_in_odd_bf, w_out_odd_bf, o,
                                       ln_g[l], ln_b[l])
            ret_s.append(st)
    return (xp, xs, jnp.stack(rwkv_p), jnp.stack(rwkv_s), jnp.stack(shift_p), jnp.stack(shift_s),
            jnp.stack(dwa_p[0]), jnp.stack(dwa_s[0]), jnp.stack(dwa_p[1]), jnp.stack(dwa_s[1]),
            jnp.stack(dwa_p[2]), jnp.stack(dwa_s[2]), jnp.stack(ret_p), jnp.stack(ret_s), jnp.stack(mem_p))
```

```python
import functools

import numpy as np
import jax
import jax.numpy as jnp
from jax import lax
from jax.experimental import pallas as pl
from jax.experimental.pallas import tpu as pltpu

F32 = jnp.float32
BF16 = jnp.bfloat16

D_MODEL = 2048
DEPTH = 4
PAST_LEN = 16384
ALPHA = (2 * DEPTH) ** 0.25
LN_EPS = 1e-5
A_HEADS = 12
A_HEAD_DIM = 64
A_WIDTH = A_HEADS * A_HEAD_DIM
A_LORA = 64
A_SHIFT_W = 3 * A_WIDTH + 2 * A_LORA
A_GN_EPS = 64e-5
B_CONFIGS = ((128, 1), (512, 4), (2048, 16))
B_GROUPS = 3
B_HEADS_PER_GROUP = 4
B_HEAD_DIM = 64
B_OUT = B_HEADS_PER_GROUP * B_HEAD_DIM
B_WIDTH = B_GROUPS * B_OUT
B_BLK = 128
C_HEADS = 6
C_HEAD_DIM = 256
C_WIDTH = C_HEADS * C_HEAD_DIM
C_CHUNK = 128
C_HEADS_PER_STEP = 2
C_ROT_BASE = 10000.0
C_NORM_EPS = 1e-6
M_TOKENS = 256
M_HEADS = 4
M_HEAD_DIM = 128
M_WIDTH = M_HEADS * M_HEAD_DIM
EVEN_IN = A_SHIFT_W + A_WIDTH + 3 * B_WIDTH + B_OUT + 2 * M_WIDTH
ODD_IN = 4 * C_WIDTH + 2 * M_WIDTH

LANES = 128
SUBLANES = 8
VMEM_LIMIT = 48 * 1024 * 1024

EVEN_IN_PAD = 7168
U_GATE_A = A_SHIFT_W // LANES
U_QB = U_GATE_A + A_WIDTH // LANES
U_KB = U_QB + B_WIDTH // LANES
U_VB = U_KB + B_WIDTH // LANES
U_GATE_B = U_VB + B_WIDTH // LANES
U_QM_EVEN = U_GATE_B + B_OUT // LANES
U_GM_EVEN = U_QM_EVEN + M_WIDTH // LANES
U_QM_ODD = 4 * C_WIDTH // LANES
U_GM_ODD = U_QM_ODD + M_WIDTH // LANES
NU = EVEN_IN_PAD // LANES


def _cparams(sem):
    return pltpu.CompilerParams(dimension_semantics=sem, vmem_limit_bytes=VMEM_LIMIT)


def _sigmoid(z):
    return 1.0 / (1.0 + jnp.exp(-z))


def _silu(z):
    return z * _sigmoid(z)


def _mm_kernel(x_ref, w_ref, o_ref):
    o_ref[...] = jnp.dot(x_ref[...], w_ref[0], preferred_element_type=F32)


def matmul(x, w, layer, tn=512):
    m, k = x.shape
    n = w.shape[2]
    tm = min(m, 1024)
    return pl.pallas_call(
        _mm_kernel,
        grid=(m // tm, n // tn),
        in_specs=[pl.BlockSpec((tm, k), lambda i, j: (i, 0)),
                  pl.BlockSpec((1, k, tn), lambda i, j: (layer, 0, j))],
        out_specs=pl.BlockSpec((tm, tn), lambda i, j: (i, j)),
        out_shape=jax.ShapeDtypeStruct((m, n), F32),
        compiler_params=_cparams(("parallel", "parallel")),
        name="matmul",
    )(x, w)


def _outproj_kernel(n_u, *refs):
    u_refs = refs[:n_u]
    w_refs = refs[n_u:2 * n_u]
    x_ref, g_ref, b_ref, o_ref, obf_ref = refs[2 * n_u:]
    acc = jnp.dot(u_refs[0][...], w_refs[0][0], preferred_element_type=F32)
    for u_ref, w_ref in zip(u_refs[1:], w_refs[1:]):
        acc = acc + jnp.dot(u_ref[...], w_ref[0], preferred_element_type=F32)
    z = ALPHA * x_ref[...] + acc
    mu = jnp.mean(z, axis=-1, keepdims=True)
    zc = z - mu
    var = jnp.mean(zc * zc, axis=-1, keepdims=True)
    y = zc * lax.rsqrt(var + LN_EPS) * g_ref[...] + b_ref[...]
    o_ref[...] = y
    obf_ref[...] = y.astype(BF16)


def outproj_ln(us, w_out, layer, x, g, b):
    m, d = x.shape
    tm = min(m, 256)
    n_u = len(us)
    in_specs = [pl.BlockSpec((tm, u.shape[1]), lambda i: (i, 0)) for u in us]
    row = 0
    for u in us:
        kw = u.shape[1]
        assert row % kw == 0
        in_specs.append(pl.BlockSpec((1, kw, d), functools.partial(lambda i, r: (layer, r, 0), r=row // kw)))
        row += kw
    assert row == w_out.shape[1]
    in_specs += [pl.BlockSpec((tm, d), lambda i: (i, 0)),
                 pl.BlockSpec((1, d), lambda i: (0, 0)),
                 pl.BlockSpec((1, d), lambda i: (0, 0))]
    return pl.pallas_call(
        functools.partial(_outproj_kernel, n_u),
        grid=(m // tm,),
        in_specs=in_specs,
        out_specs=[pl.BlockSpec((tm, d), lambda i: (i, 0)), pl.BlockSpec((tm, d), lambda i: (i, 0))],
        out_shape=[jax.ShapeDtypeStruct((m, d), F32), jax.ShapeDtypeStruct((m, d), BF16)],
        compiler_params=_cparams(("parallel",)),
        name="outproj_ln",
    )(*us, *([w_out] * n_u), x, g.reshape(1, d), b.reshape(1, d))


def _seg_sum(x, ones_bf):
    hi = x.astype(BF16)
    r1 = x - hi.astype(F32)
    mid = r1.astype(BF16)
    lo = (r1 - mid.astype(F32)).astype(BF16)
    return (jnp.dot(hi, ones_bf, preferred_element_type=F32)
            + jnp.dot(mid, ones_bf, preferred_element_type=F32)
            + jnp.dot(lo, ones_bf, preferred_element_type=F32))


def _block_ones(width, seg):
    i = np.arange(width) // seg
    return jnp.asarray((i[:, None] == i[None, :]).astype(np.float32), dtype=BF16)


def _rwkv_prep_kernel(tt, tp, h_ref, prev_ref, sh0_ref, mu_ref, w0_ref, wup_ref, a0_ref, aup_ref,
                      kk_ref_p, ka_ref_p, rk_ref_p, ones_ref,
                      r_o, d_o, k_o, kk_o, b_o, vt_o, bonus_o, *scratch):
    i = pl.program_id(1)
    x = h_ref[0]
    pr = prev_ref[0]
    prev_last = pr[pr.shape[0] - 1:pr.shape[0], :]
    first = jnp.where(i == 0, sh0_ref[0], prev_last)
    row = lax.broadcasted_iota(jnp.int32, x.shape, 0)
    if tt % SUBLANES == 0:
        rolled = pltpu.roll(x, 1, axis=0)
    else:
        rolled = jnp.concatenate([x[tt - 1:tt], x[:tt - 1]], axis=0)
    prev = jnp.where(row == 0, first, rolled)
    hs = x + (prev - x) * mu_ref[...]
    r = hs[:, 0:A_WIDTH]
    k = hs[:, A_WIDTH:2 * A_WIDTH]
    v = hs[:, 2 * A_WIDTH:3 * A_WIDTH]
    hw = hs[:, 3 * A_WIDTH:3 * A_WIDTH + A_LORA]
    ha = hs[:, 3 * A_WIDTH + A_LORA:A_SHIFT_W]
    zw = w0_ref[...] + jnp.dot(jnp.tanh(hw).astype(BF16), wup_ref[...], preferred_element_type=F32)
    nz = -zw
    softplus = jnp.maximum(nz, 0.0) + jnp.log(1.0 + jnp.exp(-jnp.abs(nz)))
    w_log = -softplus - 0.5
    neg_log_decay = jnp.exp(w_log)
    a = _sigmoid(a0_ref[...] + jnp.dot(ha.astype(BF16), aup_ref[...], preferred_element_type=F32))
    ones_bf = ones_ref[...]
    kk = k * kk_ref_p[...]
    kk = kk * lax.rsqrt(jnp.maximum(_seg_sum(kk * kk, ones_bf), 1e-24))
    k2 = k * (1.0 + (a - 1.0) * ka_ref_p[...])
    bb = kk * a
    bonus_o[0] = _seg_sum(r * k2 * rk_ref_p[...], ones_bf) * v
    for hd in range(A_HEADS):
        sl = slice(hd * A_HEAD_DIM, (hd + 1) * A_HEAD_DIM)
        r_o[0, hd] = r[:, sl]
        d_o[0, hd] = neg_log_decay[:, sl]
        k_o[0, hd] = k2[:, sl]
        kk_o[0, hd] = kk[:, sl]
        b_o[0, hd] = bb[:, sl]
    for p in range(A_HEADS // 2):
        vp = v[:, p * LANES:(p + 1) * LANES]
        if tt == tp:
            vt_o[0, p] = vp.T
        else:
            pad = scratch[0]
            pad[...] = jnp.zeros_like(pad)
            pad[0:tt, :] = vp
            vt_o[0, p] = pad[...].T


def rwkv_prep(h3, shift0, prm, ones768):
    bsz, t, _ = h3.shape
    tt = min(t, 128)
    tp = max(tt, LANES)
    nt = t // tt
    pr_rows = min(t, SUBLANES)
    pb = tt // pr_rows
    mu, w0, wup, a0, aup, k_k, k_a, r_k = prm
    row_spec = pl.BlockSpec((1, A_WIDTH), lambda b, i: (0, 0))
    lora_spec = pl.BlockSpec((A_LORA, A_WIDTH), lambda b, i: (0, 0))
    head_spec = pl.BlockSpec((1, A_HEADS, tt, A_HEAD_DIM), lambda b, i: (b, 0, i, 0))
    head_shape = jax.ShapeDtypeStruct((bsz, A_HEADS, t, A_HEAD_DIM), F32)
    scratch = [] if tt == tp else [pltpu.VMEM((tp, LANES), F32)]
    return pl.pallas_call(
        functools.partial(_rwkv_prep_kernel, tt, tp),
        grid=(bsz, nt),
        in_specs=[pl.BlockSpec((1, tt, A_SHIFT_W), lambda b, i: (b, i, 0)),
                  pl.BlockSpec((1, pr_rows, A_SHIFT_W), lambda b, i: (b, jnp.maximum(i * pb - 1, 0), 0)),
                  pl.BlockSpec((1, 1, A_SHIFT_W), lambda b, i: (b, 0, 0)),
                  pl.BlockSpec((1, A_SHIFT_W), lambda b, i: (0, 0)),
                  row_spec, lora_spec, row_spec, lora_spec, row_spec, row_spec, row_spec,
                  pl.BlockSpec((A_WIDTH, A_WIDTH), lambda b, i: (0, 0))],
        out_specs=[head_spec] * 5 + [
            pl.BlockSpec((1, A_HEADS // 2, LANES, tp), lambda b, i: (b, 0, 0, i)),
            pl.BlockSpec((1, tt, A_WIDTH), lambda b, i: (b, i, 0))],
        out_shape=[head_shape] * 5 + [
            jax.ShapeDtypeStruct((bsz, A_HEADS // 2, LANES, nt * tp), F32),
            jax.ShapeDtypeStruct((bsz, t, A_WIDTH), F32)],
        scratch_shapes=scratch,
        compiler_params=_cparams(("parallel", "parallel")),
        name="rwkv_prep",
    )(h3, h3, shift0.reshape(bsz, 1, A_SHIFT_W), mu.reshape(1, -1), w0.reshape(1, -1), wup.astype(BF16),
      a0.reshape(1, -1), aup.astype(BF16), k_k.reshape(1, -1), k_a.reshape(1, -1), r_k.reshape(1, -1), ones768)


def _rwkv_scan_kernel(steps, r_ref, d_ref, k_ref, kk_ref, b_ref, vt_ref, s0_ref, yt_ref, sout_ref, s_scr, p_scr):
    c = pl.program_id(1)

    @pl.when(c == 0)
    def _():
        s_scr[...] = s0_ref[0]

    tp = vt_ref.shape[3]
    lane = lax.broadcasted_iota(jnp.int32, (A_HEAD_DIM, tp), 1)
    yt_ref[...] = jnp.zeros_like(yt_ref)
    p_scr[...] = jnp.zeros_like(p_scr)
    ones_bf = jnp.ones((A_HEAD_DIM, tp), BF16)

    def write_y(t):
        msk = lane == t
        for hd in range(A_HEADS):
            y = jnp.dot(p_scr[hd], ones_bf, preferred_element_type=F32)
            yt_ref[0, hd] = jnp.where(msk, y, yt_ref[0, hd])

    def body(t, carry):
        write_y(t - 1)
        msk = lane == t
        sas, vcols = [], []
        for hd in range(A_HEADS):
            sas.append(-jnp.sum(s_scr[hd] * kk_ref[0, hd, pl.ds(t, 1), :], axis=-1, keepdims=True))
            vcols.append(jnp.sum(jnp.where(msk, vt_ref[0, hd], 0.0), axis=-1, keepdims=True))
        for hd in range(A_HEADS):
            s = (s_scr[hd] * jnp.exp(-d_ref[0, hd, pl.ds(t, 1), :]) + sas[hd] * b_ref[0, hd, pl.ds(t, 1), :]
                 + vcols[hd] * k_ref[0, hd, pl.ds(t, 1), :])
            s_scr[hd] = s
            p_scr[hd] = (s * r_ref[0, hd, pl.ds(t, 1), :]).astype(BF16)
        return carry

    lax.fori_loop(0, steps, body, 0, unroll=4)
    write_y(steps - 1)

    @pl.when(c == pl.num_programs(1) - 1)
    def _():
        sout_ref[0] = s_scr[...]


A_CHUNK = 128
A_CHUNK_HEADS = 6
_NT = (((1,), (1,)), ((), ()))


def _split2(x):
    hi = x.astype(BF16)
    return hi, (x - hi.astype(F32)).astype(BF16)


def _mm3(a, b):
    ah, al = _split2(a)
    bh, bl = _split2(b)
    return jnp.dot(jnp.concatenate([ah, ah, al], axis=1), jnp.concatenate([bh, bl, bh], axis=0),
                   preferred_element_type=F32)


def _mm3_nt(a, b):
    ah, al = _split2(a)
    bh, bl = _split2(b)
    return lax.dot_general(jnp.concatenate([ah, ah, al], axis=1), jnp.concatenate([bh, bl, bh], axis=1), _NT,
                           preferred_element_type=F32)


def _pad_lanes(x):
    return jnp.concatenate([x, jnp.zeros_like(x)], axis=1)


def _inv_unit_lower(lbs, row, col):
    def blk(s):
        return (row // s) == (col // s)
    eye = jnp.where(row == col, 1.0, 0.0)
    ds = [jnp.where(blk(8), lb, 0.0) for lb in lbs]
    d2s = [_mm3(d, d) for d in ds]
    d4s = [_mm3(d2, d2) for d2 in d2s]
    xs = [eye + d for d in ds]
    xs = [x + _mm3(x, d2) for x, d2 in zip(xs, d2s)]
    xs = [x + _mm3(x, d4) for x, d4 in zip(xs, d4s)]
    s = 8
    while s < A_CHUNK:
        msk = blk(2 * s) & jnp.logical_not(blk(s))
        ts = [_mm3(x, jnp.where(msk, lb, 0.0)) for x, lb in zip(xs, lbs)]
        xs = [x + _mm3(t, x) for x, t in zip(xs, ts)]
        s *= 2
    return xs


def _rwkv_chunk_heads(es, rs, kks, bs, k2s, vs, s0s, tril, row, col):
    n_tok, n = A_CHUNK, A_HEAD_DIM
    idx = range(len(es))
    cs = []
    for e in es:
        e_hi = e.astype(BF16)
        e_r1 = e - e_hi.astype(F32)
        e_mid = e_r1.astype(BF16)
        e_lo = (e_r1 - e_mid.astype(F32)).astype(BF16)
        cs.append(jnp.dot(jnp.concatenate([tril, tril, tril], axis=1),
                          jnp.concatenate([e_hi, e_mid, e_lo], axis=0), preferred_element_type=F32))
    ms = [c[n_tok // 2 - 1:n_tok // 2] for c in cs]
    ccs = [c - m for c, m in zip(cs, ms)]
    gbs = [jnp.exp(cc) for cc in ccs]
    ats = [_pad_lanes(-kks[i] * jnp.exp(es[i] - ccs[i])) for i in idx]
    bts = [_pad_lanes(bs[i] * gbs[i]) for i in idx]
    kts = [_pad_lanes(k2s[i] * gbs[i]) for i in idx]
    rts = [_pad_lanes(rs[i] * jnp.exp(-ccs[i])) for i in idx]
    gls = [jnp.exp(-cc[n_tok - 1:n_tok]) for cc in ccs]
    bks = [jnp.concatenate([bts[i], kts[i]], axis=0) for i in idx]
    zs = [_mm3_nt(ats[i], bks[i]) for i in idx]
    m2s = [_mm3_nt(bks[i], rts[i]) for i in idx]
    low = row > col
    upp = row <= col
    lbs = [jnp.where(low, z[:, :n_tok], 0.0) for z in zs]
    lks = [jnp.where(low, z[:, n_tok:], 0.0) for z in zs]
    mbks = [jnp.concatenate([jnp.where(upp, m2[:n_tok], 0.0), jnp.where(upp, m2[n_tok:], 0.0)], axis=0)
            for m2 in m2s]
    ws = _inv_unit_lower(lbs, row, col)
    ptqs = [_mm3(ws[i], jnp.concatenate([ats[i], lks[i]], axis=1)) for i in idx]
    s0ps = [_pad_lanes(s0s[i] * jnp.exp(-ms[i])) for i in idx]
    srps = [_mm3_nt(s0ps[i], jnp.concatenate([rts[i], ptqs[i][:, :2 * n]], axis=0)) for i in idx]
    us = [srps[i][:, n_tok:] + _mm3_nt(vs[i], ptqs[i][:, 2 * n:]) for i in idx]
    yss = [_mm3(jnp.concatenate([us[i], vs[i]], axis=1), jnp.concatenate([mbks[i], bks[i]], axis=1)) for i in idx]
    ys = [srps[i][:, :n_tok] + yss[i][:, :n_tok] for i in idx]
    s1s = [(s0ps[i][:, :n] + yss[i][:, n_tok:n_tok + n]) * gls[i] for i in idx]
    return ys, s1s


def _rwkv_chunk_kernel(r_ref, e_ref, k_ref, kk_ref, b_ref, vt_ref, s0_ref, yt_ref, sout_ref, s_scr):
    c = pl.program_id(1)

    @pl.when(c == 0)
    def _():
        s_scr[...] = s0_ref[0]

    row = lax.broadcasted_iota(jnp.int32, (A_CHUNK, A_CHUNK), 0)
    col = lax.broadcasted_iota(jnp.int32, (A_CHUNK, A_CHUNK), 1)
    tril = jnp.where(row >= col, 1.0, 0.0).astype(BF16)

    def group(gi, carry):
        hds = [gi * A_CHUNK_HEADS + j for j in range(A_CHUNK_HEADS)]
        ys, s1s = _rwkv_chunk_heads([e_ref[0, hd] for hd in hds], [r_ref[0, hd] for hd in hds],
                                    [kk_ref[0, hd] for hd in hds], [b_ref[0, hd] for hd in hds],
                                    [k_ref[0, hd] for hd in hds], [vt_ref[0, hd] for hd in hds],
                                    [s_scr[hd] for hd in hds], tril, row, col)
        for hd, y, s1 in zip(hds, ys, s1s):
            yt_ref[0, hd] = y
            s_scr[hd] = s1
        return carry

    lax.fori_loop(0, A_HEADS // A_CHUNK_HEADS, group, 0)

    @pl.when(c == pl.num_programs(1) - 1)
    def _():
        sout_ref[0] = s_scr[...]


def rwkv_scan(ops, vt, s0):
    r, d, k, kk, b = ops
    bsz, _, t, _ = r.shape
    tt = min(t, 128)
    tp = max(tt, LANES)
    nt = t // tt
    vt = vt.reshape(bsz, A_HEADS, A_HEAD_DIM, nt * tp)
    head_spec = pl.BlockSpec((1, A_HEADS, tt, A_HEAD_DIM), lambda b_, i: (b_, 0, i, 0))
    vt_spec = pl.BlockSpec((1, A_HEADS, A_HEAD_DIM, tp), lambda b_, i: (b_, 0, 0, i))
    st_spec = pl.BlockSpec((1, A_HEADS, A_HEAD_DIM, A_HEAD_DIM), lambda b_, i: (b_, 0, 0, 0))
    chunked = t % A_CHUNK == 0
    scratch = [pltpu.VMEM((A_HEADS, A_HEAD_DIM, A_HEAD_DIM), F32)]
    if not chunked:
        scratch.append(pltpu.VMEM((A_HEADS, A_HEAD_DIM, A_HEAD_DIM), BF16))
    yt, s_new = pl.pallas_call(
        _rwkv_chunk_kernel if chunked else functools.partial(_rwkv_scan_kernel, tt),
        grid=(bsz, nt),
        in_specs=[head_spec] * 5 + [vt_spec, st_spec],
        out_specs=[vt_spec, st_spec],
        out_shape=[jax.ShapeDtypeStruct(vt.shape, F32), jax.ShapeDtypeStruct(s0.shape, F32)],
        scratch_shapes=scratch,
        compiler_params=_cparams(("parallel", "arbitrary")),
        name="rwkv_chunk" if chunked else "rwkv_scan",
    )(r, d, k, kk, b, vt, s0)
    return yt.reshape(bsz, A_HEADS // 2, LANES, nt * tp), s_new


def _dwa_prompt_kernel(dil, *refs):
    npair = B_OUT // LANES
    in_refs = [refs[5 * hp:5 * hp + 5] for hp in range(npair)]
    out_refs = refs[5 * npair:]
    n = pl.program_id(1)
    i = lax.broadcasted_iota(jnp.int32, (B_BLK, B_BLK), 0)
    j = lax.broadcasted_iota(jnp.int32, (B_BLK, B_BLK), 1)
    mask_prev = j >= i + jnp.where(n > 0, 0, B_BLK)
    mask_cur = j <= i
    neg = -jnp.inf
    scale = B_HEAD_DIM ** -0.5
    dn = (((1,), (1,)), ((), ()))

    def residue(r, carry):
        rows = pl.ds(r, B_BLK, stride=dil) if dil > 1 else pl.ds(0, B_BLK)
        for hp, pair_refs in enumerate(in_refs):
            q2, kc2, kp2, vc2, vp2 = (ref[0, rows, :] for ref in pair_refs)
            outs, lses = [], []
            for hh in range(LANES // B_HEAD_DIM):
                sl = slice(hh * B_HEAD_DIM, (hh + 1) * B_HEAD_DIM)
                q = q2[:, sl].astype(BF16)
                sp = lax.dot_general(q, kp2[:, sl].astype(BF16), dn, preferred_element_type=F32) * scale
                sc = lax.dot_general(q, kc2[:, sl].astype(BF16), dn, preferred_element_type=F32) * scale
                sp = jnp.where(mask_prev, sp, neg)
                sc = jnp.where(mask_cur, sc, neg)
                m = jnp.maximum(jnp.max(sp, axis=-1, keepdims=True), jnp.max(sc, axis=-1, keepdims=True))
                ep = jnp.exp(sp - m)
                ec = jnp.exp(sc - m)
                l = jnp.sum(ep, axis=-1, keepdims=True) + jnp.sum(ec, axis=-1, keepdims=True)
                o = (jnp.dot((ep / l).astype(BF16), vp2[:, sl].astype(BF16), preferred_element_type=F32)
                     + jnp.dot((ec / l).astype(BF16), vc2[:, sl].astype(BF16), preferred_element_type=F32))
                outs.append(o)
                lses.append(jnp.broadcast_to(m + jnp.log(l), (B_BLK, B_HEAD_DIM)))
            out_refs[2 * hp][0, rows, :] = jnp.concatenate(outs, axis=-1)
            out_refs[2 * hp + 1][0, rows, :] = jnp.concatenate(lses, axis=-1)
        return carry

    lax.fori_loop(0, dil, residue, 0, unroll=min(dil, 2))


def dwa_prompt(h3, g):
    bsz, t, _ = h3.shape
    win, dil = B_CONFIGS[g]
    assert win == B_BLK * dil and t % win == 0
    npair = B_OUT // LANES

    def in_spec(unit, prev, hp):
        def imap(b, n):
            nn = jnp.maximum(n - 1, 0) if prev else n
            return (b, nn, unit + npair * g + hp)
        return pl.BlockSpec((1, win, LANES), imap)

    in_specs = []
    for hp in range(npair):
        in_specs += [in_spec(U_QB, False, hp), in_spec(U_KB, False, hp), in_spec(U_KB, True, hp),
                     in_spec(U_VB, False, hp), in_spec(U_VB, True, hp)]
    out_spec = pl.BlockSpec((1, win, LANES), lambda b, n: (b, n, 0))
    out_shape = jax.ShapeDtypeStruct((bsz, t, LANES), F32)
    outs = pl.pallas_call(
        functools.partial(_dwa_prompt_kernel, dil),
        grid=(bsz, t // win),
        in_specs=in_specs,
        out_specs=[out_spec] * (2 * npair),
        out_shape=[out_shape] * (2 * npair),
        compiler_params=_cparams(("parallel", "parallel")),
        name="dwa_prompt",
    )(*([h3] * len(in_specs)))
    return list(outs)


def _dwa_decode_kernel(t_new, h_ref, c0_ref, c1_ref, c2_ref, *out_refs):
    kv_w = 2 * B_OUT
    scale = B_HEAD_DIM ** -0.5
    dn = (((1,), (1,)), ((), ()))
    head_of_lane = lax.broadcasted_iota(jnp.int32, (SUBLANES, B_OUT), 1) // B_HEAD_DIM
    sub = lax.broadcasted_iota(jnp.int32, (SUBLANES, B_OUT), 0)
    own = head_of_lane == sub
    m_idx = lax.broadcasted_iota(jnp.int32, (SUBLANES, B_BLK), 1)
    neg = -jnp.inf
    hrow = h_ref[0]
    for g, c_ref in enumerate((c0_ref, c1_ref, c2_ref)):
        _, dil = B_CONFIGS[g]
        g_refs = out_refs[4 * g:4 * g + 4]
        qs = hrow[:, U_QB * LANES + g * B_OUT:U_QB * LANES + (g + 1) * B_OUT]
        ks = hrow[:, U_KB * LANES + g * B_OUT:U_KB * LANES + (g + 1) * B_OUT]
        vs = hrow[:, U_VB * LANES + g * B_OUT:U_VB * LANES + (g + 1) * B_OUT]
        for t in range(t_new):
            res = t % dil
            kbuf = c_ref[0][:, res * kv_w:res * kv_w + B_OUT]
            vbuf = c_ref[0][:, res * kv_w + B_OUT:(res + 1) * kv_w]
            qbd = jnp.where(own, jnp.broadcast_to(qs[t:t + 1], (SUBLANES, B_OUT)), 0.0)
            s_buf = lax.dot_general(qbd.astype(BF16), kbuf.astype(BF16), dn, preferred_element_type=F32) * scale
            s_buf = jnp.where(m_idx * dil + res >= t, s_buf, neg)
            new_rows = [n for n in range(t + 1) if (t - n) % dil == 0]
            kq = qbd.astype(BF16).astype(F32)
            s_new = [jnp.sum(kq * ks[n:n + 1].astype(BF16).astype(F32), axis=-1, keepdims=True) * scale
                     for n in new_rows]
            m = jnp.max(s_buf, axis=-1, keepdims=True)
            for sn in s_new:
                m = jnp.maximum(m, sn)
            e_buf = jnp.exp(s_buf - m)
            e_new = [jnp.exp(sn - m) for sn in s_new]
            l = jnp.sum(e_buf, axis=-1, keepdims=True)
            for en in e_new:
                l = l + en
            o = jnp.dot((e_buf / l).astype(BF16), vbuf.astype(BF16), preferred_element_type=F32)
            for n, en in zip(new_rows, e_new):
                o = o + (en / l).astype(BF16).astype(F32) * vs[n:n + 1].astype(BF16).astype(F32)
            lse = jnp.broadcast_to(m + jnp.log(l), (SUBLANES, B_OUT))
            o_row = jnp.sum(jnp.where(own, o, 0.0), axis=0, keepdims=True)
            lse_row = jnp.sum(jnp.where(own, lse, 0.0), axis=0, keepdims=True)
            for hp in range(B_OUT // LANES):
                g_refs[2 * hp][0, t:t + 1, :] = o_row[:, hp * LANES:(hp + 1) * LANES]
                g_refs[2 * hp + 1][0, t:t + 1, :] = lse_row[:, hp * LANES:(hp + 1) * LANES]


def dwa_decode(h3, caches):
    bsz, t, _ = h3.shape
    views, specs = [], []
    for g, (win, dil) in enumerate(B_CONFIGS):
        assert caches[g].shape[1] == win and (dil == 1 or t <= dil)
        views.append(caches[g].reshape(bsz, win // dil, dil * 2 * B_OUT))
        used = min(dil, t) * 2 * B_OUT
        specs.append(pl.BlockSpec((1, B_BLK, used), lambda b: (b, 0, 0)))
    out_spec = pl.BlockSpec((1, t, LANES), lambda b: (b, 0, 0))
    out_shape = jax.ShapeDtypeStruct((bsz, t, LANES), F32)
    n_out = 4 * B_GROUPS
    outs = pl.pallas_call(
        functools.partial(_dwa_decode_kernel, t),
        grid=(bsz,),
        in_specs=[pl.BlockSpec((1, t, NU * LANES), lambda b: (b, 0, 0))] + specs,
        out_specs=[out_spec] * n_out,
        out_shape=[out_shape] * n_out,
        compiler_params=_cparams(("parallel",)),
        name="dwa_decode",
    )(h3, *views)
    return [list(outs[4 * g:4 * g + 4]) for g in range(B_GROUPS)]


def _even_post_kernel(tt, yt_ref, bonus_ref, *refs):
    n_ga = A_WIDTH // LANES
    ga_refs = refs[:n_ga]
    gb0_ref, gb1_ref, lnxg_ref, lnxb_ref, ones_ref = refs[n_ga:n_ga + 5]
    dwa_refs = refs[n_ga + 5:n_ga + 5 + 4 * B_GROUPS]
    ua_ref, ub_ref = refs[n_ga + 5 + 4 * B_GROUPS:]
    ys = []
    for p in range(A_HEADS // 2):
        ys.append(yt_ref[0, p].T[0:tt, :])
    y = jnp.concatenate(ys, axis=-1)
    ones_bf = ones_ref[...]
    inv = 1.0 / A_HEAD_DIM
    mu = _seg_sum(y, ones_bf) * inv
    yc = y - mu
    var = _seg_sum(yc * yc, ones_bf) * inv
    ya = yc * lax.rsqrt(var + A_GN_EPS) * lnxg_ref[...] + lnxb_ref[...] + bonus_ref[0]
    gate_a = jnp.concatenate([g_ref[0] for g_ref in ga_refs], axis=-1)
    ua_ref[0] = (ya * _silu(gate_a)).astype(BF16)
    ybs = []
    for hp, gb_ref in enumerate((gb0_ref, gb1_ref)):
        o0, l0, o1, l1, o2, l2 = (dwa_refs[4 * g + 2 * hp + k][0] for g in range(B_GROUPS) for k in range(2))
        m = jnp.maximum(jnp.maximum(l0, l1), l2)
        e0, e1, e2 = jnp.exp(l0 - m), jnp.exp(l1 - m), jnp.exp(l2 - m)
        den = e0 + e1 + e2
        yb = (e0 / den) * o0 + (e1 / den) * o1 + (e2 / den) * o2
        ybs.append(yb * _silu(gb_ref[0]))
    ub_ref[0] = jnp.concatenate(ybs, axis=-1).astype(BF16)


def even_post(yt, bonus, h3, lnx_g, lnx_b, ones768, dwa):
    bsz, t, _ = h3.shape
    tt = min(t, 128)
    tp = max(tt, LANES)
    nt = t // tt
    dwa_flat = [a for grp in dwa for a in grp]
    bspec = pl.BlockSpec((1, tt, B_OUT), lambda b, i: (b, i, 0))
    pspec = pl.BlockSpec((1, tt, LANES), lambda b, i: (b, i, 0))
    row_spec = pl.BlockSpec((1, A_WIDTH), lambda b, i: (0, 0))
    return pl.pallas_call(
        functools.partial(_even_post_kernel, tt),
        grid=(bsz, nt),
        in_specs=[pl.BlockSpec((1, A_HEADS // 2, LANES, tp), lambda b, i: (b, 0, 0, i)),
                  pl.BlockSpec((1, tt, A_WIDTH), lambda b, i: (b, i, 0)),
                  *[pl.BlockSpec((1, tt, LANES), functools.partial(lambda b, i, u: (b, i, u), u=U_GATE_A + u))
                    for u in range(A_WIDTH // LANES)],
                  pl.BlockSpec((1, tt, LANES), lambda b, i: (b, i, U_GATE_B)),
                  pl.BlockSpec((1, tt, LANES), lambda b, i: (b, i, U_GATE_B + 1)),
                  row_spec, row_spec,
                  pl.BlockSpec((A_WIDTH, A_WIDTH), lambda b, i: (0, 0))] + [pspec] * len(dwa_flat),
        out_specs=[pl.BlockSpec((1, tt, A_WIDTH), lambda b, i: (b, i, 0)), bspec],
        out_shape=[jax.ShapeDtypeStruct((bsz, t, A_WIDTH), BF16), jax.ShapeDtypeStruct((bsz, t, B_OUT), BF16)],
        compiler_params=_cparams(("parallel", "parallel")),
        name="even_post",
    )(yt, bonus, *([h3] * (A_WIDTH // LANES + 2)), lnx_g.reshape(1, -1), lnx_b.reshape(1, -1), ones768, *dwa_flat)


def _mem_attn_kernel(q_ref, g_ref, k_ref, v_ref, u_ref):
    dn = (((1,), (1,)), ((), ()))
    s = lax.dot_general(q_ref[0].astype(BF16), k_ref[0].astype(BF16), dn,
                        preferred_element_type=F32) * (M_HEAD_DIM ** -0.5)
    m = jnp.max(s, axis=-1, keepdims=True)
    e = jnp.exp(s - m)
    p = e / jnp.sum(e, axis=-1, keepdims=True)
    o = jnp.dot(p.astype(BF16), v_ref[0].astype(BF16), preferred_element_type=F32)
    u_ref[0] = (o * _silu(g_ref[0])).astype(BF16)


def mem_attn(h3, mkv, u_q, u_g):
    bsz, t, _ = h3.shape
    tq = min(t, 512)
    return pl.pallas_call(
        _mem_attn_kernel,
        grid=(bsz, t // tq, M_HEADS),
        in_specs=[pl.BlockSpec((1, tq, LANES), lambda b, i, hd: (b, i, u_q + hd)),
                  pl.BlockSpec((1, tq, LANES), lambda b, i, hd: (b, i, u_g + hd)),
                  pl.BlockSpec((1, M_TOKENS, LANES), lambda b, i, hd: (b, 0, hd)),
                  pl.BlockSpec((1, M_TOKENS, LANES), lambda b, i, hd: (b, 0, M_HEADS + hd))],
        out_specs=pl.BlockSpec((1, tq, LANES), lambda b, i, hd: (b, i, hd)),
        out_shape=jax.ShapeDtypeStruct((bsz, t, M_WIDTH), BF16),
        compiler_params=_cparams(("parallel", "parallel", "parallel")),
        name="mem_attn",
    )(h3, h3, mkv, mkv)


def _rope_kernel(pos0, ang_ref, cos_ref, sin_ref):
    rows = cos_ref.shape[0]
    base = pl.program_id(0) * rows
    pos = (lax.broadcasted_iota(jnp.int32, cos_ref.shape, 0) + base).astype(F32) + pos0
    ph = pos * ang_ref[...]
    cos_ref[...] = jnp.cos(ph)
    sin_ref[...] = jnp.sin(ph)


def rope_tables(rows, pos0):
    angle = 1.0 / (C_ROT_BASE ** jnp.linspace(0.0, 1.0, C_HEAD_DIM // 2, dtype=F32))
    ang = jnp.repeat(angle, 2).reshape(1, C_HEAD_DIM)
    tr = min(rows, 512)
    spec = pl.BlockSpec((tr, C_HEAD_DIM), lambda i: (i, 0))
    shape = jax.ShapeDtypeStruct((rows, C_HEAD_DIM), F32)
    return pl.pallas_call(
        functools.partial(_rope_kernel, float(pos0)),
        grid=(rows // tr,),
        in_specs=[pl.BlockSpec((1, C_HEAD_DIM), lambda i: (0, 0))],
        out_specs=[spec, spec],
        out_shape=[shape, shape],
        compiler_params=_cparams(("parallel",)),
        name="rope_tables",
    )(ang)


def _rot_pairs(z):
    even = lax.broadcasted_iota(jnp.int32, (z.shape[0], LANES), 1) % 2 == 0
    parts = []
    for blk in range(z.shape[1] // LANES):
        zb = z[:, blk * LANES:(blk + 1) * LANES]
        nxt = pltpu.roll(zb, LANES - 1, axis=1)
        prv = pltpu.roll(zb, 1, axis=1)
        parts.append(jnp.where(even, -nxt, prv))
    return jnp.concatenate(parts, axis=-1)


def _retention_kernel(tb, chunk, lg_ref, q_ref, k_ref, v_ref, g_ref, cos_ref, sin_ref, r0_ref,
                      u_ref, rout_ref, r_scr, *pad):
    c = pl.program_id(2)

    @pl.when(c == 0)
    def _():
        r_scr[...] = r0_ref[0]

    if tb == C_CHUNK:
        q2, k2, v2 = q_ref[0], k_ref[0], v_ref[0]
    else:
        vals = []
        for src, buf in zip((q_ref, k_ref, v_ref), pad):
            buf[...] = jnp.zeros_like(buf)
            buf[0:tb, :] = src[0]
            vals.append(buf[...])
        q2, k2, v2 = vals
    cos, sin = cos_ref[...], sin_ref[...]
    ii = lax.broadcasted_iota(jnp.int32, (C_CHUNK, C_CHUNK), 0)
    jj = lax.broadcasted_iota(jnp.int32, (C_CHUNK, C_CHUNK), 1)
    diff = (ii - jj).astype(F32)
    idx = lax.broadcasted_iota(jnp.int32, (C_CHUNK, 1), 0).astype(F32)
    dn = (((1,), (1,)), ((), ()))
    us = []
    for hh in range(C_HEADS_PER_STEP):
        sl = slice(hh * C_HEAD_DIM, (hh + 1) * C_HEAD_DIM)
        q, k, v = q2[:, sl], k2[:, sl], v2[:, sl]
        lg = lg_ref[pl.program_id(1) * C_HEADS_PER_STEP + hh]
        qr = q * cos + _rot_pairs(q) * sin
        kr = (k * cos + _rot_pairs(k) * sin) * (C_HEAD_DIM ** -0.5)
        dmat = jnp.where(diff >= 0, jnp.exp(lg * jnp.maximum(diff, 0.0)), 0.0)
        xi = jnp.exp(lg * (idx + 1.0))
        zeta = jnp.exp(lg * (chunk - 1.0 - idx))
        g_chunk = jnp.exp(lg * jnp.full((1, 1), chunk, F32))
        qb = qr.astype(BF16)
        vb = v.astype(BF16)
        sc = lax.dot_general(qb, kr.astype(BF16), dn, preferred_element_type=F32) * dmat
        r_old = r_scr[hh]
        o = (jnp.dot(sc.astype(BF16), vb, preferred_element_type=F32)
             + jnp.dot(qb, r_old.astype(BF16), preferred_element_type=F32) * xi)
        kz_t = (kr * zeta).T.astype(BF16)
        r_scr[hh] = r_old * g_chunk + jnp.dot(kz_t, vb, preferred_element_type=F32)
        o = o[0:tb]
        y = o * lax.rsqrt(jnp.mean(o * o, axis=-1, keepdims=True) + C_NORM_EPS)
        us.append(y * _silu(g_ref[0][:, sl]))
    u_ref[0] = jnp.concatenate(us, axis=-1).astype(BF16)

    @pl.when(c == pl.num_programs(2) - 1)
    def _():
        rout_ref[0] = r_scr[...]


def retention(h3, cos, sin, r0):
    bsz, t, _ = h3.shape
    tb = min(t, C_CHUNK)
    nc = t // tb
    chunk = float(tb)
    lg = jnp.log(1.0 - 2.0 ** (-5.0 - jnp.arange(C_HEADS, dtype=F32)))
    hps = C_HEADS_PER_STEP
    width = hps * C_HEAD_DIM
    nq = C_WIDTH // width

    def col_spec(off):
        return pl.BlockSpec((1, tb, width), lambda b, hd, c: (b, c, off + hd))

    tab_spec = pl.BlockSpec((C_CHUNK, C_HEAD_DIM), lambda b, hd, c: (c, 0))
    st_spec = pl.BlockSpec((1, hps, C_HEAD_DIM, C_HEAD_DIM), lambda b, hd, c: (b, hd, 0, 0))
    scratch = [pltpu.VMEM((hps, C_HEAD_DIM, C_HEAD_DIM), F32)]
    if tb != C_CHUNK:
        scratch += [pltpu.VMEM((C_CHUNK, width), F32)] * 3
    return pl.pallas_call(
        functools.partial(_retention_kernel, tb, chunk),
        grid=(bsz, C_HEADS // hps, nc),
        in_specs=[pl.BlockSpec(memory_space=pltpu.SMEM),
                  col_spec(0), col_spec(nq), col_spec(2 * nq), col_spec(3 * nq),
                  tab_spec, tab_spec, st_spec],
        out_specs=[pl.BlockSpec((1, tb, width), lambda b, hd, c: (b, c, hd)), st_spec],
        out_shape=[jax.ShapeDtypeStruct((bsz, t, C_WIDTH), BF16), jax.ShapeDtypeStruct(r0.shape, F32)],
        scratch_shapes=scratch,
        compiler_params=_cparams(("parallel", "parallel", "arbitrary")),
        name="retention",
    )(lg, h3, h3, h3, h3, cos, sin, r0)


def _even_layer(x, x_bf, mkv, dwa_bufs, s0, shift0, w_in_bf, w_out_bf, e, ln_g, ln_b, rw, ones768):
    bsz, t, d = x.shape
    h = matmul(x_bf.reshape(bsz * t, d), w_in_bf, e)
    h3 = h.reshape(bsz, t, NU * LANES)
    mu, w0, wup, a0, aup, k_k, k_a, r_k, lnx_g, lnx_b = rw
    r, dcy, k2, kk, bb, vt, bonus = rwkv_prep(h3, shift0, (mu, w0, wup, a0, aup, k_k, k_a, r_k), ones768)
    yt, s_new = rwkv_scan((r, dcy, k2, kk, bb), vt, s0)
    if dwa_bufs is None:
        dwa = [dwa_prompt(h3, g) for g in range(B_GROUPS)]
    else:
        dwa = dwa_decode(h3, dwa_bufs)
    u_a, u_b = even_post(yt, bonus, h3, lnx_g, lnx_b, ones768, dwa)
    u_m = mem_attn(h3, mkv, U_QM_EVEN, U_GM_EVEN)
    x2, x2_bf = outproj_ln([u_a.reshape(bsz * t, -1), u_b.reshape(bsz * t, -1), u_m.reshape(bsz * t, -1)],
                           w_out_bf, e, x.reshape(bsz * t, d), ln_g, ln_b)
    rows = []
    for g, (win, _) in enumerate(B_CONFIGS):
        keep = t if dwa_bufs is not None else min(win, t)
        kg = h3[:, t - keep:, U_KB * LANES + g * B_OUT:U_KB * LANES + (g + 1) * B_OUT]
        vg = h3[:, t - keep:, U_VB * LANES + g * B_OUT:U_VB * LANES + (g + 1) * B_OUT]
        rows.append(jnp.stack([kg.reshape(bsz, keep, B_HEADS_PER_GROUP, B_HEAD_DIM),
                               vg.reshape(bsz, keep, B_HEADS_PER_GROUP, B_HEAD_DIM)], axis=2))
    return x2.reshape(bsz, t, d), x2_bf.reshape(bsz, t, d), s_new, h3[:, t - 1, :A_SHIFT_W], rows


def _odd_layer(x, x_bf, mkv, r0, tabs, w_in_bf, w_out_bf, o, ln_g, ln_b):
    bsz, t, d = x.shape
    h = matmul(x_bf.reshape(bsz * t, d), w_in_bf, o)
    h3 = h.reshape(bsz, t, NU * LANES)
    u_c, r_new = retention(h3, tabs[0], tabs[1], r0)
    u_m = mem_attn(h3, mkv, U_QM_ODD, U_GM_ODD)
    x2, x2_bf = outproj_ln([u_c.reshape(bsz * t, -1), u_m.reshape(bsz * t, -1)],
                           w_out_bf, o, x.reshape(bsz * t, d), ln_g, ln_b)
    return x2.reshape(bsz, t, d), x2_bf.reshape(bsz, t, d), r_new


def kernel(x_prompt, x_sample, state_rwkv, state_rwkv_shift, cache_dwa_g0, cache_dwa_g1, cache_dwa_g2, state_ret, cache_mem_kv, mem_prompt, w_in_even, w_out_even, w_in_odd, w_out_odd, w_mem_kv, ln_g, ln_b, rwkv_mu, rwkv_w0, rwkv_w_up, rwkv_a0, rwkv_a_up, rwkv_k_k, rwkv_k_a, rwkv_r_k, rwkv_lnx_g, rwkv_lnx_b):
    xp, xs = x_prompt, x_sample
    xp_bf, xs_bf = xp.astype(BF16), xs.astype(BF16)
    bp, tp_len, d = xp.shape
    bs, ts_len, _ = xs.shape
    dwa_cache = (cache_dwa_g0, cache_dwa_g1, cache_dwa_g2)
    ones768 = _block_ones(A_WIDTH, A_HEAD_DIM)
    mem_bf = mem_prompt.reshape(bp * M_TOKENS, d).astype(BF16)
    tabs_p = rope_tables(max(tp_len, C_CHUNK), 0)
    tabs_s = rope_tables(max(ts_len, C_CHUNK), PAST_LEN)
    rwkv_p, rwkv_s, shift_p, shift_s, ret_p, ret_s, mem_p = [], [], [], [], [], [], []
    dwa_p = [[] for _ in B_CONFIGS]
    dwa_s = [[] for _ in B_CONFIGS]
    w_in_even_bf = jnp.pad(w_in_even.astype(BF16), ((0, 0), (0, 0), (0, EVEN_IN_PAD - EVEN_IN)))
    w_out_even_bf = w_out_even.astype(BF16)
    w_in_odd_bf = w_in_odd.astype(BF16)
    w_out_odd_bf = w_out_odd.astype(BF16)
    w_mem_bf = w_mem_kv.astype(BF16)
    for l in range(DEPTH):
        mkv_p = matmul(mem_bf, w_mem_bf, l).reshape(bp, M_TOKENS, 2 * M_WIDTH)
        mem_p.append(mkv_p.reshape(bp, M_TOKENS, 2, M_HEADS, M_HEAD_DIM))
        mkv_s = cache_mem_kv[l].reshape(bs, M_TOKENS, 2 * M_WIDTH)
        if l % 2 == 0:
            e = l // 2
            w_in_bf, w_out_bf = w_in_even_bf, w_out_even_bf
            rw = (rwkv_mu[e], rwkv_w0[e], rwkv_w_up[e], rwkv_a0[e], rwkv_a_up[e], rwkv_k_k[e], rwkv_k_a[e],
                  rwkv_r_k[e], rwkv_lnx_g[e], rwkv_lnx_b[e])
            s0 = jnp.zeros((bp, A_HEADS, A_HEAD_DIM, A_HEAD_DIM), F32)
            sh0 = jnp.zeros((bp, A_SHIFT_W), F32)
            xp, xp_bf, st, sh, rows = _even_layer(xp, xp_bf, mkv_p, None, s0, sh0, w_in_bf, w_out_bf, e,
                                                  ln_g[l], ln_b[l], rw, ones768)
            rwkv_p.append(st)
            shift_p.append(sh)
            for g in range(B_GROUPS):
                dwa_p[g].append(rows[g])
            bufs = tuple(c[e] for c in dwa_cache)
            xs, xs_bf, st, sh, rows = _even_layer(xs, xs_bf, mkv_s, bufs, state_rwkv[e], state_rwkv_shift[e],
                                                  w_in_bf, w_out_bf, e, ln_g[l], ln_b[l], rw, ones768)
            rwkv_s.append(st)
            shift_s.append(sh)
            for g in range(B_GROUPS):
                dwa_s[g].append(rows[g])
        else:
            o = l // 2
            r0 = jnp.zeros((bp, C_HEADS, C_HEAD_DIM, C_HEAD_DIM), F32)
            xp, xp_bf, st = _odd_layer(xp, xp_bf, mkv_p, r0, tabs_p, w_in_odd_bf, w_out_odd_bf, o, ln_g[l], ln_b[l])
            ret_p.append(st)
            xs, xs_bf, st = _odd_layer(xs, xs_bf, mkv_s, state_ret[o], tabs_s, w_in_odd_bf, w_out_odd_bf, o,
                                       ln_g[l], ln_b[l])
            ret_s.append(st)
    return (xp, xs, jnp.stack(rwkv_p), jnp.stack(rwkv_s), jnp.stack(shift_p), jnp.stack(shift_s),
            jnp.stack(dwa_p[0]), jnp.stack(dwa_s[0]), jnp.stack(dwa_p[1]), jnp.stack(dwa_s[1]),
            jnp.stack(dwa_p[2]), jnp.stack(dwa_s[2]), jnp.stack(ret_p), jnp.stack(ret_s), jnp.stack(mem_p))
```

```python
import functools

import numpy as np
import jax
import jax.numpy as jnp
from jax import lax
from jax.experimental import pallas as pl
from jax.experimental.pallas import tpu as pltpu

F32 = jnp.float32
BF16 = jnp.bfloat16

D_MODEL = 2048
DEPTH = 4
PAST_LEN = 16384
ALPHA = (2 * DEPTH) ** 0.25
LN_EPS = 1e-5
A_HEADS = 12
A_HEAD_DIM = 64
A_WIDTH = A_HEADS * A_HEAD_DIM
A_LORA = 64
A_SHIFT_W = 3 * A_WIDTH + 2 * A_LORA
A_GN_EPS = 64e-5
B_CONFIGS = ((128, 1), (512, 4), (2048, 16))
B_GROUPS = 3
B_HEADS_PER_GROUP = 4
B_HEAD_DIM = 64
B_OUT = B_HEADS_PER_GROUP * B_HEAD_DIM
B_WIDTH = B_GROUPS * B_OUT
B_BLK = 128
B_RES_PER_ITER = 2
C_HEADS = 6
C_HEAD_DIM = 256
C_WIDTH = C_HEADS * C_HEAD_DIM
C_CHUNK = 128
C_HEADS_PER_STEP = 2
C_ROT_BASE = 10000.0
C_NORM_EPS = 1e-6
M_TOKENS = 256
M_HEADS = 4
M_HEAD_DIM = 128
M_WIDTH = M_HEADS * M_HEAD_DIM
EVEN_IN = A_SHIFT_W + A_WIDTH + 3 * B_WIDTH + B_OUT + 2 * M_WIDTH
ODD_IN = 4 * C_WIDTH + 2 * M_WIDTH

LANES = 128
SUBLANES = 8
VMEM_LIMIT = 48 * 1024 * 1024

EVEN_IN_PAD = 7168
U_GATE_A = A_SHIFT_W // LANES
U_QB = U_GATE_A + A_WIDTH // LANES
U_KB = U_QB + B_WIDTH // LANES
U_VB = U_KB + B_WIDTH // LANES
U_GATE_B = U_VB + B_WIDTH // LANES
U_QM_EVEN = U_GATE_B + B_OUT // LANES
U_GM_EVEN = U_QM_EVEN + M_WIDTH // LANES
U_QM_ODD = 4 * C_WIDTH // LANES
U_GM_ODD = U_QM_ODD + M_WIDTH // LANES
NU = EVEN_IN_PAD // LANES


def _cparams(sem):
    return pltpu.CompilerParams(dimension_semantics=sem, vmem_limit_bytes=VMEM_LIMIT)


def _sigmoid(z):
    return 1.0 / (1.0 + jnp.exp(-z))


def _silu(z):
    return z * _sigmoid(z)


def _mm_kernel(x_ref, w_ref, o_ref):
    o_ref[...] = jnp.dot(x_ref[...], w_ref[0], preferred_element_type=F32)


def matmul(x, w, layer):
    m, k = x.shape
    n = w.shape[2]
    tm = min(m, 1024)
    tn = 1024 if n % 1024 == 0 else 512
    return pl.pallas_call(
        _mm_kernel,
        grid=(m // tm, n // tn),
        in_specs=[pl.BlockSpec((tm, k), lambda i, j: (i, 0)),
                  pl.BlockSpec((1, k, tn), lambda i, j: (layer, 0, j))],
        out_specs=pl.BlockSpec((tm, tn), lambda i, j: (i, j)),
        out_shape=jax.ShapeDtypeStruct((m, n), F32),
        compiler_params=_cparams(("parallel", "parallel")),
        name="matmul",
    )(x, w)


def _outproj_kernel(n_u, *refs):
    u_refs = refs[:n_u]
    w_refs = refs[n_u:2 * n_u]
    x_ref, g_ref, b_ref, o_ref, obf_ref = refs[2 * n_u:]
    acc = jnp.dot(u_refs[0][...], w_refs[0][0], preferred_element_type=F32)
    for u_ref, w_ref in zip(u_refs[1:], w_refs[1:]):
        acc = acc + jnp.dot(u_ref[...], w_ref[0], preferred_element_type=F32)
    z = ALPHA * x_ref[...] + acc
    mu = jnp.mean(z, axis=-1, keepdims=True)
    zc = z - mu
    var = jnp.mean(zc * zc, axis=-1, keepdims=True)
    y = zc * lax.rsqrt(var + LN_EPS) * g_ref[...] + b_ref[...]
    o_ref[...] = y
    obf_ref[...] = y.astype(BF16)


def outproj_ln(us, w_out, layer, x, g, b):
    m, d = x.shape
    tm = min(m, 512)
    n_u = len(us)
    in_specs = [pl.BlockSpec((tm, u.shape[1]), lambda i: (i, 0)) for u in us]
    row = 0
    for u in us:
        kw = u.shape[1]
        assert row % kw == 0
        in_specs.append(pl.BlockSpec((1, kw, d), functools.partial(lambda i, r: (layer, r, 0), r=row // kw)))
        row += kw
    assert row == w_out.shape[1]
    in_specs += [pl.BlockSpec((tm, d), lambda i: (i, 0)),
                 pl.BlockSpec((1, d), lambda i: (0, 0)),
                 pl.BlockSpec((1, d), lambda i: (0, 0))]
    return pl.pallas_call(
        functools.partial(_outproj_kernel, n_u),
        grid=(m // tm,),
        in_specs=in_specs,
        out_specs=[pl.BlockSpec((tm, d), lambda i: (i, 0)), pl.BlockSpec((tm, d), lambda i: (i, 0))],
        out_shape=[jax.ShapeDtypeStruct((m, d), F32), jax.ShapeDtypeStruct((m, d), BF16)],
        compiler_params=_cparams(("parallel",)),
        name="outproj_ln",
    )(*us, *([w_out] * n_u), x, g.reshape(1, d), b.reshape(1, d))


def _seg_sum(x, ones_bf):
    hi = x.astype(BF16)
    r1 = x - hi.astype(F32)
    mid = r1.astype(BF16)
    lo = (r1 - mid.astype(F32)).astype(BF16)
    outs = []
    for p in range(x.shape[1] // LANES):
        sl = slice(p * LANES, (p + 1) * LANES)
        terms = jnp.concatenate([hi[:, sl], mid[:, sl], lo[:, sl]], axis=1)
        outs.append(jnp.dot(terms, jnp.concatenate([ones_bf] * 3, axis=0), preferred_element_type=F32))
    return jnp.concatenate(outs, axis=1)


def _block_ones(width, seg):
    i = np.arange(width) // seg
    return jnp.asarray((i[:, None] == i[None, :]).astype(np.float32), dtype=BF16)


def _rwkv_prep_kernel(tt, tp, h_ref, prev_ref, sh0_ref, mu_ref, w0_ref, wup_ref, a0_ref, aup_ref,
                      kk_ref_p, ka_ref_p, rk_ref_p, ones_ref,
                      r_o, d_o, k_o, kk_o, b_o, vt_o, bonus_o, *scratch):
    i = pl.program_id(1)
    x = h_ref[0]
    pr = prev_ref[0]
    prev_last = pr[pr.shape[0] - 1:pr.shape[0], :]
    first = jnp.where(i == 0, sh0_ref[0], prev_last)
    row = lax.broadcasted_iota(jnp.int32, x.shape, 0)
    if tt % SUBLANES == 0:
        rolled = pltpu.roll(x, 1, axis=0)
    else:
        rolled = jnp.concatenate([x[tt - 1:tt], x[:tt - 1]], axis=0)
    prev = jnp.where(row == 0, first, rolled)
    hs = x + (prev - x) * mu_ref[...]
    r = hs[:, 0:A_WIDTH]
    k = hs[:, A_WIDTH:2 * A_WIDTH]
    v = hs[:, 2 * A_WIDTH:3 * A_WIDTH]
    hw = hs[:, 3 * A_WIDTH:3 * A_WIDTH + A_LORA]
    ha = hs[:, 3 * A_WIDTH + A_LORA:A_SHIFT_W]
    zw = w0_ref[...] + jnp.dot(jnp.tanh(hw).astype(BF16), wup_ref[...], preferred_element_type=F32)
    nz = -zw
    softplus = jnp.maximum(nz, 0.0) + jnp.log(1.0 + jnp.exp(-jnp.abs(nz)))
    w_log = -softplus - 0.5
    neg_log_decay = jnp.exp(w_log)
    a = _sigmoid(a0_ref[...] + jnp.dot(ha.astype(BF16), aup_ref[...], preferred_element_type=F32))
    ones_bf = ones_ref[...]
    kk = k * kk_ref_p[...]
    kk = kk * lax.rsqrt(jnp.maximum(_seg_sum(kk * kk, ones_bf), 1e-24))
    k2 = k * (1.0 + (a - 1.0) * ka_ref_p[...])
    bb = kk * a
    bonus_o[0] = _seg_sum(r * k2 * rk_ref_p[...], ones_bf) * v
    for hd in range(A_HEADS):
        sl = slice(hd * A_HEAD_DIM, (hd + 1) * A_HEAD_DIM)
        r_o[0, hd] = r[:, sl]
        d_o[0, hd] = neg_log_decay[:, sl]
        k_o[0, hd] = k2[:, sl]
        kk_o[0, hd] = kk[:, sl]
        b_o[0, hd] = bb[:, sl]
    for p in range(A_HEADS // 2):
        vp = v[:, p * LANES:(p + 1) * LANES]
        if tt == tp:
            vt_o[0, p] = vp.T
        else:
            pad = scratch[0]
            pad[...] = jnp.zeros_like(pad)
            pad[0:tt, :] = vp
            vt_o[0, p] = pad[...].T


def rwkv_prep(h3, shift0, prm, head_ones):
    bsz, t, _ = h3.shape
    tt = min(t, 128)
    tp = max(tt, LANES)
    nt = t // tt
    pr_rows = min(t, SUBLANES)
    pb = tt // pr_rows
    mu, w0, wup, a0, aup, k_k, k_a, r_k = prm
    row_spec = pl.BlockSpec((1, A_WIDTH), lambda b, i: (0, 0))
    lora_spec = pl.BlockSpec((A_LORA, A_WIDTH), lambda b, i: (0, 0))
    head_spec = pl.BlockSpec((1, A_HEADS, tt, A_HEAD_DIM), lambda b, i: (b, 0, i, 0))
    head_shape = jax.ShapeDtypeStruct((bsz, A_HEADS, t, A_HEAD_DIM), F32)
    scratch = [] if tt == tp else [pltpu.VMEM((tp, LANES), F32)]
    return pl.pallas_call(
        functools.partial(_rwkv_prep_kernel, tt, tp),
        grid=(bsz, nt),
        in_specs=[pl.BlockSpec((1, tt, A_SHIFT_W), lambda b, i: (b, i, 0)),
                  pl.BlockSpec((1, pr_rows, A_SHIFT_W), lambda b, i: (b, jnp.maximum(i * pb - 1, 0), 0)),
                  pl.BlockSpec((1, 1, A_SHIFT_W), lambda b, i: (b, 0, 0)),
                  pl.BlockSpec((1, A_SHIFT_W), lambda b, i: (0, 0)),
                  row_spec, lora_spec, row_spec, lora_spec, row_spec, row_spec, row_spec,
                  pl.BlockSpec((LANES, LANES), lambda b, i: (0, 0))],
        out_specs=[head_spec] * 5 + [
            pl.BlockSpec((1, A_HEADS // 2, LANES, tp), lambda b, i: (b, 0, 0, i)),
            pl.BlockSpec((1, tt, A_WIDTH), lambda b, i: (b, i, 0))],
        out_shape=[head_shape] * 5 + [
            jax.ShapeDtypeStruct((bsz, A_HEADS // 2, LANES, nt * tp), F32),
            jax.ShapeDtypeStruct((bsz, t, A_WIDTH), F32)],
        scratch_shapes=scratch,
        compiler_params=_cparams(("parallel", "parallel")),
        name="rwkv_prep",
    )(h3, h3, shift0.reshape(bsz, 1, A_SHIFT_W), mu.reshape(1, -1), w0.reshape(1, -1), wup.astype(BF16),
      a0.reshape(1, -1), aup.astype(BF16), k_k.reshape(1, -1), k_a.reshape(1, -1), r_k.reshape(1, -1), head_ones)


def _rwkv_scan_kernel(steps, r_ref, d_ref, k_ref, kk_ref, b_ref, vt_ref, s0_ref, yt_ref, sout_ref, s_scr, p_scr):
    c = pl.program_id(1)

    @pl.when(c == 0)
    def _():
        s_scr[...] = s0_ref[0]

    tp = vt_ref.shape[3]
    lane = lax.broadcasted_iota(jnp.int32, (A_HEAD_DIM, tp), 1)
    yt_ref[...] = jnp.zeros_like(yt_ref)
    p_scr[...] = jnp.zeros_like(p_scr)
    ones_bf = jnp.ones((A_HEAD_DIM, tp), BF16)

    def write_y(t):
        msk = lane == t
        for hd in range(A_HEADS):
            y = jnp.dot(p_scr[hd], ones_bf, preferred_element_type=F32)
            yt_ref[0, hd] = jnp.where(msk, y, yt_ref[0, hd])

    def body(t, carry):
        write_y(t - 1)
        msk = lane == t
        sas, vcols = [], []
        for hd in range(A_HEADS):
            sas.append(-jnp.sum(s_scr[hd] * kk_ref[0, hd, pl.ds(t, 1), :], axis=-1, keepdims=True))
            vcols.append(jnp.sum(jnp.where(msk, vt_ref[0, hd], 0.0), axis=-1, keepdims=True))
        for hd in range(A_HEADS):
            s = (s_scr[hd] * jnp.exp(-d_ref[0, hd, pl.ds(t, 1), :]) + sas[hd] * b_ref[0, hd, pl.ds(t, 1), :]
                 + vcols[hd] * k_ref[0, hd, pl.ds(t, 1), :])
            s_scr[hd] = s
            p_scr[hd] = (s * r_ref[0, hd, pl.ds(t, 1), :]).astype(BF16)
        return carry

    lax.fori_loop(0, steps, body, 0, unroll=4)
    write_y(steps - 1)

    @pl.when(c == pl.num_programs(1) - 1)
    def _():
        sout_ref[0] = s_scr[...]


A_CHUNK = 128
A_CHUNK_HEADS = 6
_NT = (((1,), (1,)), ((), ()))


def _split2(x):
    hi = x.astype(BF16)
    return hi, (x - hi.astype(F32)).astype(BF16)


def _mm3(a, b):
    ah, al = _split2(a)
    bh, bl = _split2(b)
    return jnp.dot(jnp.concatenate([ah, ah, al], axis=1), jnp.concatenate([bh, bl, bh], axis=0),
                   preferred_element_type=F32)


def _mm3_nt(a, b):
    ah, al = _split2(a)
    bh, bl = _split2(b)
    return lax.dot_general(jnp.concatenate([ah, ah, al], axis=1), jnp.concatenate([bh, bl, bh], axis=1), _NT,
                           preferred_element_type=F32)


def _pad_lanes(x):
    return jnp.concatenate([x, jnp.zeros_like(x)], axis=1)


def _inv_unit_lower(lbs, row, col):
    def blk(s):
        return (row // s) == (col // s)
    eye = jnp.where(row == col, 1.0, 0.0)
    ds = [jnp.where(blk(8), lb, 0.0) for lb in lbs]
    d2s = [_mm3(d, d) for d in ds]
    d4s = [_mm3(d2, d2) for d2 in d2s]
    xs = [eye + d for d in ds]
    xs = [x + _mm3(x, d2) for x, d2 in zip(xs, d2s)]
    xs = [x + _mm3(x, d4) for x, d4 in zip(xs, d4s)]
    s = 8
    while s < A_CHUNK:
        msk = blk(2 * s) & jnp.logical_not(blk(s))
        ts = [_mm3(x, jnp.where(msk, lb, 0.0)) for x, lb in zip(xs, lbs)]
        xs = [x + _mm3(t, x) for x, t in zip(xs, ts)]
        s *= 2
    return xs


def _rwkv_chunk_heads(es, rs, kks, bs, k2s, vs, s0s, tril, row, col):
    n_tok, n = A_CHUNK, A_HEAD_DIM
    idx = range(len(es))
    cs = []
    for e in es:
        e_hi = e.astype(BF16)
        e_r1 = e - e_hi.astype(F32)
        e_mid = e_r1.astype(BF16)
        e_lo = (e_r1 - e_mid.astype(F32)).astype(BF16)
        cs.append(jnp.dot(jnp.concatenate([tril, tril, tril], axis=1),
                          jnp.concatenate([e_hi, e_mid, e_lo], axis=0), preferred_element_type=F32))
    ms = [c[n_tok // 2 - 1:n_tok // 2] for c in cs]
    ccs = [c - m for c, m in zip(cs, ms)]
    gbs = [jnp.exp(cc) for cc in ccs]
    ats = [_pad_lanes(-kks[i] * jnp.exp(es[i] - ccs[i])) for i in idx]
    bts = [_pad_lanes(bs[i] * gbs[i]) for i in idx]
    kts = [_pad_lanes(k2s[i] * gbs[i]) for i in idx]
    rts = [_pad_lanes(rs[i] * jnp.exp(-ccs[i])) for i in idx]
    gls = [jnp.exp(-cc[n_tok - 1:n_tok]) for cc in ccs]
    bks = [jnp.concatenate([bts[i], kts[i]], axis=0) for i in idx]
    zs = [_mm3_nt(ats[i], bks[i]) for i in idx]
    m2s = [_mm3_nt(bks[i], rts[i]) for i in idx]
    low = row > col
    upp = row <= col
    lbs = [jnp.where(low, z[:, :n_tok], 0.0) for z in zs]
    lks = [jnp.where(low, z[:, n_tok:], 0.0) for z in zs]
    mbks = [jnp.concatenate([jnp.where(upp, m2[:n_tok], 0.0), jnp.where(upp, m2[n_tok:], 0.0)], axis=0)
            for m2 in m2s]
    ws = _inv_unit_lower(lbs, row, col)
    ptqs = [_mm3(ws[i], jnp.concatenate([ats[i], lks[i]], axis=1)) for i in idx]
    s0ps = [_pad_lanes(s0s[i] * jnp.exp(-ms[i])) for i in idx]
    srps = [_mm3_nt(s0ps[i], jnp.concatenate([rts[i], ptqs[i][:, :2 * n]], axis=0)) for i in idx]
    us = [srps[i][:, n_tok:] + _mm3_nt(vs[i], ptqs[i][:, 2 * n:]) for i in idx]
    yss = [_mm3(jnp.concatenate([us[i], vs[i]], axis=1), jnp.concatenate([mbks[i], bks[i]], axis=1)) for i in idx]
    ys = [srps[i][:, :n_tok] + yss[i][:, :n_tok] for i in idx]
    s1s = [(s0ps[i][:, :n] + yss[i][:, n_tok:n_tok + n]) * gls[i] for i in idx]
    return ys, s1s


def _rwkv_chunk_kernel(r_ref, e_ref, k_ref, kk_ref, b_ref, vt_ref, s0_ref, yt_ref, sout_ref, s_scr):
    c = pl.program_id(1)

    @pl.when(c == 0)
    def _():
        s_scr[...] = s0_ref[0]

    row = lax.broadcasted_iota(jnp.int32, (A_CHUNK, A_CHUNK), 0)
    col = lax.broadcasted_iota(jnp.int32, (A_CHUNK, A_CHUNK), 1)
    tril = jnp.where(row >= col, 1.0, 0.0).astype(BF16)

    def group(gi, carry):
        hds = [gi * A_CHUNK_HEADS + j for j in range(A_CHUNK_HEADS)]
        ys, s1s = _rwkv_chunk_heads([e_ref[0, hd] for hd in hds], [r_ref[0, hd] for hd in hds],
                                    [kk_ref[0, hd] for hd in hds], [b_ref[0, hd] for hd in hds],
                                    [k_ref[0, hd] for hd in hds], [vt_ref[0, hd] for hd in hds],
                                    [s_scr[hd] for hd in hds], tril, row, col)
        for hd, y, s1 in zip(hds, ys, s1s):
            yt_ref[0, hd] = y
            s_scr[hd] = s1
        return carry

    lax.fori_loop(0, A_HEADS // A_CHUNK_HEADS, group, 0)

    @pl.when(c == pl.num_programs(1) - 1)
    def _():
        sout_ref[0] = s_scr[...]


def rwkv_scan(ops, vt, s0):
    r, d, k, kk, b = ops
    bsz, _, t, _ = r.shape
    tt = min(t, 128)
    tp = max(tt, LANES)
    nt = t // tt
    vt = vt.reshape(bsz, A_HEADS, A_HEAD_DIM, nt * tp)
    head_spec = pl.BlockSpec((1, A_HEADS, tt, A_HEAD_DIM), lambda b_, i: (b_, 0, i, 0))
    vt_spec = pl.BlockSpec((1, A_HEADS, A_HEAD_DIM, tp), lambda b_, i: (b_, 0, 0, i))
    st_spec = pl.BlockSpec((1, A_HEADS, A_HEAD_DIM, A_HEAD_DIM), lambda b_, i: (b_, 0, 0, 0))
    chunked = t % A_CHUNK == 0
    scratch = [pltpu.VMEM((A_HEADS, A_HEAD_DIM, A_HEAD_DIM), F32)]
    if not chunked:
        scratch.append(pltpu.VMEM((A_HEADS, A_HEAD_DIM, A_HEAD_DIM), BF16))
    yt, s_new = pl.pallas_call(
        _rwkv_chunk_kernel if chunked else functools.partial(_rwkv_scan_kernel, tt),
        grid=(bsz, nt),
        in_specs=[head_spec] * 5 + [vt_spec, st_spec],
        out_specs=[vt_spec, st_spec],
        out_shape=[jax.ShapeDtypeStruct(vt.shape, F32), jax.ShapeDtypeStruct(s0.shape, F32)],
        scratch_shapes=scratch,
        compiler_params=_cparams(("parallel", "arbitrary")),
        name="rwkv_chunk" if chunked else "rwkv_scan",
    )(r, d, k, kk, b, vt, s0)
    return yt.reshape(bsz, A_HEADS // 2, LANES, nt * tp), s_new


def _dwa_prompt_kernel(dil, *refs):
    npair = B_OUT // LANES
    in_refs = [refs[5 * hp:5 * hp + 5] for hp in range(npair)]
    out_refs = refs[5 * npair:]
    n = pl.program_id(1)
    i = lax.broadcasted_iota(jnp.int32, (B_BLK, B_BLK), 0)
    j = lax.broadcasted_iota(jnp.int32, (B_BLK, B_BLK), 1)
    mask_prev = j >= i + jnp.where(n > 0, 0, B_BLK)
    mask_cur = j <= i
    neg = -jnp.inf
    scale = B_HEAD_DIM ** -0.5
    dn = (((1,), (1,)), ((), ()))

    n_res = min(dil, B_RES_PER_ITER)
    n_hh = LANES // B_HEAD_DIM

    def residues(it, carry):
        rows = [pl.ds(it * n_res + k, B_BLK, stride=dil) if dil > 1 else pl.ds(0, B_BLK) for k in range(n_res)]
        units = [(k, hp, hh) for k in range(n_res) for hp in range(npair) for hh in range(n_hh)]
        loaded = {(k, hp): [ref[0, rows[k], :] for ref in in_refs[hp]] for k in range(n_res) for hp in range(npair)}

        def part(u, which):
            k, hp, hh = u
            return loaded[(k, hp)][which][:, hh * B_HEAD_DIM:(hh + 1) * B_HEAD_DIM].astype(BF16)

        qs = [part(u, 0) for u in units]
        sps = [jnp.where(mask_prev, lax.dot_general(q, part(u, 2), dn, preferred_element_type=F32) * scale, neg)
               for q, u in zip(qs, units)]
        scs = [jnp.where(mask_cur, lax.dot_general(q, part(u, 1), dn, preferred_element_type=F32) * scale, neg)
               for q, u in zip(qs, units)]
        ms = [jnp.maximum(jnp.max(sp, axis=-1, keepdims=True), jnp.max(sc, axis=-1, keepdims=True))
              for sp, sc in zip(sps, scs)]
        eps = [jnp.exp(sp - m) for sp, m in zip(sps, ms)]
        ecs = [jnp.exp(sc - m) for sc, m in zip(scs, ms)]
        ls = [jnp.sum(ep, axis=-1, keepdims=True) + jnp.sum(ec, axis=-1, keepdims=True) for ep, ec in zip(eps, ecs)]
        os_ = [jnp.dot((ep / l).astype(BF16), part(u, 4), preferred_element_type=F32)
               + jnp.dot((ec / l).astype(BF16), part(u, 3), preferred_element_type=F32)
               for ep, ec, l, u in zip(eps, ecs, ls, units)]
        lses = [jnp.broadcast_to(m + jnp.log(l), (B_BLK, B_HEAD_DIM)) for m, l in zip(ms, ls)]
        for k in range(n_res):
            for hp in range(npair):
                sel = [i_u for i_u, u in enumerate(units) if u[0] == k and u[1] == hp]
                out_refs[2 * hp][0, rows[k], :] = jnp.concatenate([os_[i_u] for i_u in sel], axis=-1)
                out_refs[2 * hp + 1][0, rows[k], :] = jnp.concatenate([lses[i_u] for i_u in sel], axis=-1)
        return carry

    lax.fori_loop(0, dil // n_res, residues, 0)


def dwa_prompt(h3, g):
    bsz, t, _ = h3.shape
    win, dil = B_CONFIGS[g]
    assert win == B_BLK * dil and t % win == 0
    npair = B_OUT // LANES

    def in_spec(unit, prev, hp):
        def imap(b, n):
            nn = jnp.maximum(n - 1, 0) if prev else n
            return (b, nn, unit + npair * g + hp)
        return pl.BlockSpec((1, win, LANES), imap)

    in_specs = []
    for hp in range(npair):
        in_specs += [in_spec(U_QB, False, hp), in_spec(U_KB, False, hp), in_spec(U_KB, True, hp),
                     in_spec(U_VB, False, hp), in_spec(U_VB, True, hp)]
    out_spec = pl.BlockSpec((1, win, LANES), lambda b, n: (b, n, 0))
    out_shape = jax.ShapeDtypeStruct((bsz, t, LANES), F32)
    outs = pl.pallas_call(
        functools.partial(_dwa_prompt_kernel, dil),
        grid=(bsz, t // win),
        in_specs=in_specs,
        out_specs=[out_spec] * (2 * npair),
        out_shape=[out_shape] * (2 * npair),
        compiler_params=_cparams(("parallel", "parallel")),
        name="dwa_prompt",
    )(*([h3] * len(in_specs)))
    return list(outs)


def _dwa_decode_kernel(t_new, h_ref, c0_ref, c1_ref, c2_ref, *out_refs):
    kv_w = 2 * B_OUT
    scale = B_HEAD_DIM ** -0.5
    dn = (((1,), (1,)), ((), ()))
    head_of_lane = lax.broadcasted_iota(jnp.int32, (SUBLANES, B_OUT), 1) // B_HEAD_DIM
    sub = lax.broadcasted_iota(jnp.int32, (SUBLANES, B_OUT), 0)
    own = head_of_lane == sub
    m_idx = lax.broadcasted_iota(jnp.int32, (SUBLANES, B_BLK), 1)
    neg = -jnp.inf
    hrow = h_ref[0]
    for g, c_ref in enumerate((c0_ref, c1_ref, c2_ref)):
        _, dil = B_CONFIGS[g]
        g_refs = out_refs[4 * g:4 * g + 4]
        qs = hrow[:, U_QB * LANES + g * B_OUT:U_QB * LANES + (g + 1) * B_OUT]
        ks = hrow[:, U_KB * LANES + g * B_OUT:U_KB * LANES + (g + 1) * B_OUT]
        vs = hrow[:, U_VB * LANES + g * B_OUT:U_VB * LANES + (g + 1) * B_OUT]
        for t in range(t_new):
            res = t % dil
            kbuf = c_ref[0][:, res * kv_w:res * kv_w + B_OUT]
            vbuf = c_ref[0][:, res * kv_w + B_OUT:(res + 1) * kv_w]
            qbd = jnp.where(own, jnp.broadcast_to(qs[t:t + 1], (SUBLANES, B_OUT)), 0.0)
            s_buf = lax.dot_general(qbd.astype(BF16), kbuf.astype(BF16), dn, preferred_element_type=F32) * scale
            s_buf = jnp.where(m_idx * dil + res >= t, s_buf, neg)
            new_rows = [n for n in range(t + 1) if (t - n) % dil == 0]
            kq = qbd.astype(BF16).astype(F32)
            s_new = [jnp.sum(kq * ks[n:n + 1].astype(BF16).astype(F32), axis=-1, keepdims=True) * scale
                     for n in new_rows]
            m = jnp.max(s_buf, axis=-1, keepdims=True)
            for sn in s_new:
                m = jnp.maximum(m, sn)
            e_buf = jnp.exp(s_buf - m)
            e_new = [jnp.exp(sn - m) for sn in s_new]
            l = jnp.sum(e_buf, axis=-1, keepdims=True)
            for en in e_new:
                l = l + en
            o = jnp.dot((e_buf / l).astype(BF16), vbuf.astype(BF16), preferred_element_type=F32)
            for n, en in zip(new_rows, e_new):
                o = o + (en / l).astype(BF16).astype(F32) * vs[n:n + 1].astype(BF16).astype(F32)
            lse = jnp.broadcast_to(m + jnp.log(l), (SUBLANES, B_OUT))
            o_row = jnp.sum(jnp.where(own, o, 0.0), axis=0, keepdims=True)
            lse_row = jnp.sum(jnp.where(own, lse, 0.0), axis=0, keepdims=True)
            for hp in range(B_OUT // LANES):
                g_refs[2 * hp][0, t:t + 1, :] = o_row[:, hp * LANES:(hp + 1) * LANES]
                g_refs[2 * hp + 1][0, t:t + 1, :] = lse_row[:, hp * LANES:(hp + 1) * LANES]


def dwa_decode(h3, caches):
    bsz, t, _ = h3.shape
    views, specs = [], []
    for g, (win, dil) in enumerate(B_CONFIGS):
        assert caches[g].shape[1] == win and (dil == 1 or t <= dil)
        views.append(caches[g].reshape(bsz, win // dil, dil * 2 * B_OUT))
        used = min(dil, t) * 2 * B_OUT
        specs.append(pl.BlockSpec((1, B_BLK, used), lambda b: (b, 0, 0)))
    out_spec = pl.BlockSpec((1, t, LANES), lambda b: (b, 0, 0))
    out_shape = jax.ShapeDtypeStruct((bsz, t, LANES), F32)
    n_out = 4 * B_GROUPS
    outs = pl.pallas_call(
        functools.partial(_dwa_decode_kernel, t),
        grid=(bsz,),
        in_specs=[pl.BlockSpec((1, t, NU * LANES), lambda b: (b, 0, 0))] + specs,
        out_specs=[out_spec] * n_out,
        out_shape=[out_shape] * n_out,
        compiler_params=_cparams(("parallel",)),
        name="dwa_decode",
    )(h3, *views)
    return [list(outs[4 * g:4 * g + 4]) for g in range(B_GROUPS)]


def _even_post_kernel(tt, yt_ref, bonus_ref, *refs):
    n_ga = A_WIDTH // LANES
    ga_refs = refs[:n_ga]
    gb0_ref, gb1_ref, lnxg_ref, lnxb_ref, ones_ref = refs[n_ga:n_ga + 5]
    dwa_refs = refs[n_ga + 5:n_ga + 5 + 4 * B_GROUPS]
    ua_ref, ub_ref = refs[n_ga + 5 + 4 * B_GROUPS:]
    ys = []
    for p in range(A_HEADS // 2):
        ys.append(yt_ref[0, p].T[0:tt, :])
    y = jnp.concatenate(ys, axis=-1)
    ones_bf = ones_ref[...]
    inv = 1.0 / A_HEAD_DIM
    mu = _seg_sum(y, ones_bf) * inv
    yc = y - mu
    var = _seg_sum(yc * yc, ones_bf) * inv
    ya = yc * lax.rsqrt(var + A_GN_EPS) * lnxg_ref[...] + lnxb_ref[...] + bonus_ref[0]
    gate_a = jnp.concatenate([g_ref[0] for g_ref in ga_refs], axis=-1)
    ua_ref[0] = (ya * _silu(gate_a)).astype(BF16)
    ybs = []
    for hp, gb_ref in enumerate((gb0_ref, gb1_ref)):
        o0, l0, o1, l1, o2, l2 = (dwa_refs[4 * g + 2 * hp + k][0] for g in range(B_GROUPS) for k in range(2))
        m = jnp.maximum(jnp.maximum(l0, l1), l2)
        e0, e1, e2 = jnp.exp(l0 - m), jnp.exp(l1 - m), jnp.exp(l2 - m)
        den = e0 + e1 + e2
        yb = (e0 / den) * o0 + (e1 / den) * o1 + (e2 / den) * o2
        ybs.append(yb * _silu(gb_ref[0]))
    ub_ref[0] = jnp.concatenate(ybs, axis=-1).astype(BF16)


def even_post(yt, bonus, h3, lnx_g, lnx_b, head_ones, dwa):
    bsz, t, _ = h3.shape
    tt = min(t, 128)
    tp = max(tt, LANES)
    nt = t // tt
    dwa_flat = [a for grp in dwa for a in grp]
    bspec = pl.BlockSpec((1, tt, B_OUT), lambda b, i: (b, i, 0))
    pspec = pl.BlockSpec((1, tt, LANES), lambda b, i: (b, i, 0))
    row_spec = pl.BlockSpec((1, A_WIDTH), lambda b, i: (0, 0))
    return pl.pallas_call(
        functools.partial(_even_post_kernel, tt),
        grid=(bsz, nt),
        in_specs=[pl.BlockSpec((1, A_HEADS // 2, LANES, tp), lambda b, i: (b, 0, 0, i)),
                  pl.BlockSpec((1, tt, A_WIDTH), lambda b, i: (b, i, 0)),
                  *[pl.BlockSpec((1, tt, LANES), functools.partial(lambda b, i, u: (b, i, u), u=U_GATE_A + u))
                    for u in range(A_WIDTH // LANES)],
                  pl.BlockSpec((1, tt, LANES), lambda b, i: (b, i, U_GATE_B)),
                  pl.BlockSpec((1, tt, LANES), lambda b, i: (b, i, U_GATE_B + 1)),
                  row_spec, row_spec,
                  pl.BlockSpec((LANES, LANES), lambda b, i: (0, 0))] + [pspec] * len(dwa_flat),
        out_specs=[pl.BlockSpec((1, tt, A_WIDTH), lambda b, i: (b, i, 0)), bspec],
        out_shape=[jax.ShapeDtypeStruct((bsz, t, A_WIDTH), BF16), jax.ShapeDtypeStruct((bsz, t, B_OUT), BF16)],
        compiler_params=_cparams(("parallel", "parallel")),
        name="even_post",
    )(yt, bonus, *([h3] * (A_WIDTH // LANES + 2)), lnx_g.reshape(1, -1), lnx_b.reshape(1, -1), head_ones, *dwa_flat)


def _mem_attn_kernel(*refs):
    q_refs, g_refs = refs[:M_HEADS], refs[M_HEADS:2 * M_HEADS]
    k_ref, v_ref, u_ref = refs[2 * M_HEADS:]
    dn = (((1,), (1,)), ((), ()))
    hds = range(M_HEADS)
    sls = [slice(hd * M_HEAD_DIM, (hd + 1) * M_HEAD_DIM) for hd in hds]
    ss = [lax.dot_general(q_refs[hd][0].astype(BF16), k_ref[0][:, sls[hd]].astype(BF16), dn,
                          preferred_element_type=F32) * (M_HEAD_DIM ** -0.5) for hd in hds]
    ms = [jnp.max(s, axis=-1, keepdims=True) for s in ss]
    es = [jnp.exp(s - m) for s, m in zip(ss, ms)]
    ps = [e / jnp.sum(e, axis=-1, keepdims=True) for e in es]
    os_ = [jnp.dot(ps[hd].astype(BF16), v_ref[0][:, sls[hd]].astype(BF16), preferred_element_type=F32) for hd in hds]
    u_ref[0] = jnp.concatenate([os_[hd] * _silu(g_refs[hd][0]) for hd in hds], axis=-1).astype(BF16)


def mem_attn(h3, mkv, u_q, u_g):
    bsz, t, _ = h3.shape
    tq = min(t, 256)

    def col_spec(u):
        return pl.BlockSpec((1, tq, LANES), lambda b, i: (b, i, u))

    return pl.pallas_call(
        _mem_attn_kernel,
        grid=(bsz, t // tq),
        in_specs=[col_spec(u_q + hd) for hd in range(M_HEADS)] + [col_spec(u_g + hd) for hd in range(M_HEADS)] + [
            pl.BlockSpec((1, M_TOKENS, M_WIDTH), lambda b, i: (b, 0, 0)),
            pl.BlockSpec((1, M_TOKENS, M_WIDTH), lambda b, i: (b, 0, 1))],
        out_specs=pl.BlockSpec((1, tq, M_WIDTH), lambda b, i: (b, i, 0)),
        out_shape=jax.ShapeDtypeStruct((bsz, t, M_WIDTH), BF16),
        compiler_params=_cparams(("parallel", "parallel")),
        name="mem_attn",
    )(*([h3] * (2 * M_HEADS)), mkv, mkv)


def _rope_kernel(pos0, ang_ref, cos_ref, sin_ref):
    rows = cos_ref.shape[0]
    base = pl.program_id(0) * rows
    pos = (lax.broadcasted_iota(jnp.int32, cos_ref.shape, 0) + base).astype(F32) + pos0
    ph = pos * ang_ref[...]
    cos_ref[...] = jnp.cos(ph)
    sin_ref[...] = jnp.sin(ph)


def rope_tables(rows, pos0):
    angle = 1.0 / (C_ROT_BASE ** jnp.linspace(0.0, 1.0, C_HEAD_DIM // 2, dtype=F32))
    ang = jnp.repeat(angle, 2).reshape(1, C_HEAD_DIM)
    tr = min(rows, 512)
    spec = pl.BlockSpec((tr, C_HEAD_DIM), lambda i: (i, 0))
    shape = jax.ShapeDtypeStruct((rows, C_HEAD_DIM), F32)
    return pl.pallas_call(
        functools.partial(_rope_kernel, float(pos0)),
        grid=(rows // tr,),
        in_specs=[pl.BlockSpec((1, C_HEAD_DIM), lambda i: (0, 0))],
        out_specs=[spec, spec],
        out_shape=[shape, shape],
        compiler_params=_cparams(("parallel",)),
        name="rope_tables",
    )(ang)


def _rot_pairs(z):
    even = lax.broadcasted_iota(jnp.int32, (z.shape[0], LANES), 1) % 2 == 0
    parts = []
    for blk in range(z.shape[1] // LANES):
        zb = z[:, blk * LANES:(blk + 1) * LANES]
        nxt = pltpu.roll(zb, LANES - 1, axis=1)
        prv = pltpu.roll(zb, 1, axis=1)
        parts.append(jnp.where(even, -nxt, prv))
    return jnp.concatenate(parts, axis=-1)


def _retention_kernel(tb, chunk, lg_ref, q_ref, k_ref, v_ref, g_ref, cos_ref, sin_ref, r0_ref,
                      u_ref, rout_ref, r_scr, *pad):
    c = pl.program_id(2)

    @pl.when(c == 0)
    def _():
        r_scr[...] = r0_ref[0]

    if tb == C_CHUNK:
        q2, k2, v2 = q_ref[0], k_ref[0], v_ref[0]
    else:
        vals = []
        for src, buf in zip((q_ref, k_ref, v_ref), pad):
            buf[...] = jnp.zeros_like(buf)
            buf[0:tb, :] = src[0]
            vals.append(buf[...])
        q2, k2, v2 = vals
    cos, sin = cos_ref[...], sin_ref[...]
    ii = lax.broadcasted_iota(jnp.int32, (C_CHUNK, C_CHUNK), 0)
    jj = lax.broadcasted_iota(jnp.int32, (C_CHUNK, C_CHUNK), 1)
    diff = (ii - jj).astype(F32)
    idx = lax.broadcasted_iota(jnp.int32, (C_CHUNK, 1), 0).astype(F32)
    dn = (((1,), (1,)), ((), ()))
    hh_r = range(C_HEADS_PER_STEP)
    sls = [slice(hh * C_HEAD_DIM, (hh + 1) * C_HEAD_DIM) for hh in hh_r]
    lgs = [lg_ref[pl.program_id(1) * C_HEADS_PER_STEP + hh] for hh in hh_r]
    qrs = [q2[:, sl] * cos + _rot_pairs(q2[:, sl]) * sin for sl in sls]
    krs = [(k2[:, sl] * cos + _rot_pairs(k2[:, sl]) * sin) * (C_HEAD_DIM ** -0.5) for sl in sls]
    qbs = [qr.astype(BF16) for qr in qrs]
    vbs = [v2[:, sl].astype(BF16) for sl in sls]
    scs = [lax.dot_general(qb, kr.astype(BF16), dn, preferred_element_type=F32)
           * jnp.where(diff >= 0, jnp.exp(lg * jnp.maximum(diff, 0.0)), 0.0) for qb, kr, lg in zip(qbs, krs, lgs)]
    r_olds = [r_scr[hh] for hh in hh_r]
    kz_ts = [(kr * jnp.exp(lg * (chunk - 1.0 - idx))).T.astype(BF16) for kr, lg in zip(krs, lgs)]
    os_ = [jnp.dot(sc.astype(BF16), vb, preferred_element_type=F32)
           + jnp.dot(qb, r_old.astype(BF16), preferred_element_type=F32) * jnp.exp(lg * (idx + 1.0))
           for sc, vb, qb, r_old, lg in zip(scs, vbs, qbs, r_olds, lgs)]
    for hh in hh_r:
        r_scr[hh] = (r_olds[hh] * jnp.exp(lgs[hh] * jnp.full((1, 1), chunk, F32))
                     + jnp.dot(kz_ts[hh], vbs[hh], preferred_element_type=F32))
    us = []
    for hh in hh_r:
        o = os_[hh][0:tb]
        y = o * lax.rsqrt(jnp.mean(o * o, axis=-1, keepdims=True) + C_NORM_EPS)
        us.append(y * _silu(g_ref[0][:, sls[hh]]))
    u_ref[0] = jnp.concatenate(us, axis=-1).astype(BF16)

    @pl.when(c == pl.num_programs(2) - 1)
    def _():
        rout_ref[0] = r_scr[...]


def retention(h3, cos, sin, r0):
    bsz, t, _ = h3.shape
    tb = min(t, C_CHUNK)
    nc = t // tb
    chunk = float(tb)
    lg = jnp.log(1.0 - 2.0 ** (-5.0 - jnp.arange(C_HEADS, dtype=F32)))
    hps = C_HEADS_PER_STEP
    width = hps * C_HEAD_DIM
    nq = C_WIDTH // width

    def col_spec(off):
        return pl.BlockSpec((1, tb, width), lambda b, hd, c: (b, c, off + hd))

    tab_spec = pl.BlockSpec((C_CHUNK, C_HEAD_DIM), lambda b, hd, c: (c, 0))
    st_spec = pl.BlockSpec((1, hps, C_HEAD_DIM, C_HEAD_DIM), lambda b, hd, c: (b, hd, 0, 0))
    scratch = [pltpu.VMEM((hps, C_HEAD_DIM, C_HEAD_DIM), F32)]
    if tb != C_CHUNK:
        scratch += [pltpu.VMEM((C_CHUNK, width), F32)] * 3
    return pl.pallas_call(
        functools.partial(_retention_kernel, tb, chunk),
        grid=(bsz, C_HEADS // hps, nc),
        in_specs=[pl.BlockSpec(memory_space=pltpu.SMEM),
                  col_spec(0), col_spec(nq), col_spec(2 * nq), col_spec(3 * nq),
                  tab_spec, tab_spec, st_spec],
        out_specs=[pl.BlockSpec((1, tb, width), lambda b, hd, c: (b, c, hd)), st_spec],
        out_shape=[jax.ShapeDtypeStruct((bsz, t, C_WIDTH), BF16), jax.ShapeDtypeStruct(r0.shape, F32)],
        scratch_shapes=scratch,
        compiler_params=_cparams(("parallel", "parallel", "arbitrary")),
        name="retention",
    )(lg, h3, h3, h3, h3, cos, sin, r0)


def _even_layer(x, x_bf, mkv, dwa_bufs, s0, shift0, w_in_bf, w_out_bf, e, ln_g, ln_b, rw, head_ones):
    bsz, t, d = x.shape
    h = matmul(x_bf.reshape(bsz * t, d), w_in_bf, e)
    h3 = h.reshape(bsz, t, NU * LANES)
    mu, w0, wup, a0, aup, k_k, k_a, r_k, lnx_g, lnx_b = rw
    r, dcy, k2, kk, bb, vt, bonus = rwkv_prep(h3, shift0, (mu, w0, wup, a0, aup, k_k, k_a, r_k), head_ones)
    yt, s_new = rwkv_scan((r, dcy, k2, kk, bb), vt, s0)
    if dwa_bufs is None:
        dwa = [dwa_prompt(h3, g) for g in range(B_GROUPS)]
    else:
        dwa = dwa_decode(h3, dwa_bufs)
    u_a, u_b = even_post(yt, bonus, h3, lnx_g, lnx_b, head_ones, dwa)
    u_m = mem_attn(h3, mkv, U_QM_EVEN, U_GM_EVEN)
    x2, x2_bf = outproj_ln([u_a.reshape(bsz * t, -1), u_b.reshape(bsz * t, -1), u_m.reshape(bsz * t, -1)],
                           w_out_bf, e, x.reshape(bsz * t, d), ln_g, ln_b)
    rows = []
    for g, (win, _) in enumerate(B_CONFIGS):
        keep = t if dwa_bufs is not None else min(win, t)
        kg = h3[:, t - keep:, U_KB * LANES + g * B_OUT:U_KB * LANES + (g + 1) * B_OUT]
        vg = h3[:, t - keep:, U_VB * LANES + g * B_OUT:U_VB * LANES + (g + 1) * B_OUT]
        rows.append(jnp.stack([kg.reshape(bsz, keep, B_HEADS_PER_GROUP, B_HEAD_DIM),
                               vg.reshape(bsz, keep, B_HEADS_PER_GROUP, B_HEAD_DIM)], axis=2))
    return x2.reshape(bsz, t, d), x2_bf.reshape(bsz, t, d), s_new, h3[:, t - 1, :A_SHIFT_W], rows


def _odd_layer(x, x_bf, mkv, r0, tabs, w_in_bf, w_out_bf, o, ln_g, ln_b):
    bsz, t, d = x.shape
    h = matmul(x_bf.reshape(bsz * t, d), w_in_bf, o)
    h3 = h.reshape(bsz, t, NU * LANES)
    u_c, r_new = retention(h3, tabs[0], tabs[1], r0)
    u_m = mem_attn(h3, mkv, U_QM_ODD, U_GM_ODD)
    x2, x2_bf = outproj_ln([u_c.reshape(bsz * t, -1), u_m.reshape(bsz * t, -1)],
                           w_out_bf, o, x.reshape(bsz * t, d), ln_g, ln_b)
    return x2.reshape(bsz, t, d), x2_bf.reshape(bsz, t, d), r_new


def kernel(x_prompt, x_sample, state_rwkv, state_rwkv_shift, cache_dwa_g0, cache_dwa_g1, cache_dwa_g2, state_ret, cache_mem_kv, mem_prompt, w_in_even, w_out_even, w_in_odd, w_out_odd, w_mem_kv, ln_g, ln_b, rwkv_mu, rwkv_w0, rwkv_w_up, rwkv_a0, rwkv_a_up, rwkv_k_k, rwkv_k_a, rwkv_r_k, rwkv_lnx_g, rwkv_lnx_b):
    xp, xs = x_prompt, x_sample
    xp_bf, xs_bf = xp.astype(BF16), xs.astype(BF16)
    bp, tp_len, d = xp.shape
    bs, ts_len, _ = xs.shape
    dwa_cache = (cache_dwa_g0, cache_dwa_g1, cache_dwa_g2)
    head_ones = _block_ones(LANES, A_HEAD_DIM)
    mem_bf = mem_prompt.reshape(bp * M_TOKENS, d).astype(BF16)
    tabs_p = rope_tables(max(tp_len, C_CHUNK), 0)
    tabs_s = rope_tables(max(ts_len, C_CHUNK), PAST_LEN)
    rwkv_p, rwkv_s, shift_p, shift_s, ret_p, ret_s, mem_p = [], [], [], [], [], [], []
    dwa_p = [[] for _ in B_CONFIGS]
    dwa_s = [[] for _ in B_CONFIGS]
    w_in_even_bf = jnp.concatenate(
        [w_in_even.astype(BF16), jnp.zeros(w_in_even.shape[:2] + (EVEN_IN_PAD - EVEN_IN,), BF16)], axis=-1)
    w_out_even_bf = w_out_even.astype(BF16)
    w_in_odd_bf = w_in_odd.astype(BF16)
    w_out_odd_bf = w_out_odd.astype(BF16)
    w_mem_bf = w_mem_kv.astype(BF16)
    for l in range(DEPTH):
        mkv_p = matmul(mem_bf, w_mem_bf, l).reshape(bp, M_TOKENS, 2 * M_WIDTH)
        mem_p.append(mkv_p.reshape(bp, M_TOKENS, 2, M_HEADS, M_HEAD_DIM))
        mkv_s = cache_mem_kv[l].reshape(bs, M_TOKENS, 2 * M_WIDTH)
        if l % 2 == 0:
            e = l // 2
            w_in_bf, w_out_bf = w_in_even_bf, w_out_even_bf
            rw = (rwkv_mu[e], rwkv_w0[e], rwkv_w_up[e], rwkv_a0[e], rwkv_a_up[e], rwkv_k_k[e], rwkv_k_a[e],
                  rwkv_r_k[e], rwkv_lnx_g[e], rwkv_lnx_b[e])
            s0 = jnp.zeros((bp, A_HEADS, A_HEAD_DIM, A_HEAD_DIM), F32)
            sh0 = jnp.zeros((bp, A_SHIFT_W), F32)
            xp, xp_bf, st, sh, rows = _even_layer(xp, xp_bf, mkv_p, None, s0, sh0, w_in_bf, w_out_bf, e,
                                                  ln_g[l], ln_b[l], rw, head_ones)
            rwkv_p.append(st)
            shift_p.append(sh)
            for g in range(B_GROUPS):
                dwa_p[g].append(rows[g])
            bufs = tuple(c[e] for c in dwa_cache)
            xs, xs_bf, st, sh, rows = _even_layer(xs, xs_bf, mkv_s, bufs, state_rwkv[e], state_rwkv_shift[e],
                                                  w_in_bf, w_out_bf, e, ln_g[l], ln_b[l], rw, head_ones)
            rwkv_s.append(st)
            shift_s.append(sh)
            for g in range(B_GROUPS):
                dwa_s[g].append(rows[g])
        else:
            o = l // 2
            r0 = jnp.zeros((bp, C_HEADS, C_HEAD_DIM, C_HEAD_DIM), F32)
            xp, xp_bf, st = _odd_layer(xp, xp_bf, mkv_p, r0, tabs_p, w_in_odd_bf, w_out_odd_bf, o, ln_g[l], ln_b[l])
            ret_p.append(st)
            xs, xs_bf, st = _odd_layer(xs, xs_bf, mkv_s, state_ret[o], tabs_s, w_in_odd_bf, w_out_odd_bf, o,
                                       ln_g[l], ln_b[l])
            ret_s.append(st)
    return (xp, xs, jnp.stack(rwkv_p), jnp.stack(rwkv_s), jnp.stack(shift_p), jnp.stack(shift_s),
            jnp.stack(dwa_p[0]), jnp.stack(dwa_s[0]), jnp.stack(dwa_p[1]), jnp.stack(dwa_s[1]),
            jnp.stack(dwa_p[2]), jnp.stack(dwa_s[2]), jnp.stack(ret_p), jnp.stack(ret_s), jnp.stack(mem_p))
```

```python
import functools

import numpy as np
import jax
import jax.numpy as jnp
from jax import lax
from jax.experimental import pallas as pl
from jax.experimental.pallas import tpu as pltpu

F32 = jnp.float32
BF16 = jnp.bfloat16

D_MODEL = 2048
DEPTH = 4
PAST_LEN = 16384
ALPHA = (2 * DEPTH) ** 0.25
LN_EPS = 1e-5
A_HEADS = 12
A_HEAD_DIM = 64
A_WIDTH = A_HEADS * A_HEAD_DIM
A_LORA = 64
A_SHIFT_W = 3 * A_WIDTH + 2 * A_LORA
A_GN_EPS = 64e-5
B_CONFIGS = ((128, 1), (512, 4), (2048, 16))
B_GROUPS = 3
B_HEADS_PER_GROUP = 4
B_HEAD_DIM = 64
B_OUT = B_HEADS_PER_GROUP * B_HEAD_DIM
B_WIDTH = B_GROUPS * B_OUT
B_BLK = 128
B_RES_PER_ITER = 2
C_HEADS = 6
C_HEAD_DIM = 256
C_WIDTH = C_HEADS * C_HEAD_DIM
C_CHUNK = 128
C_HEADS_PER_STEP = 3
C_ROT_BASE = 10000.0
C_NORM_EPS = 1e-6
M_TOKENS = 256
M_HEADS = 4
M_HEAD_DIM = 128
M_WIDTH = M_HEADS * M_HEAD_DIM
EVEN_IN = A_SHIFT_W + A_WIDTH + 3 * B_WIDTH + B_OUT + 2 * M_WIDTH
ODD_IN = 4 * C_WIDTH + 2 * M_WIDTH

LANES = 128
SUBLANES = 8
VMEM_LIMIT = 48 * 1024 * 1024

EVEN_IN_PAD = 7168
U_GATE_A = A_SHIFT_W // LANES
U_QB = U_GATE_A + A_WIDTH // LANES
U_KB = U_QB + B_WIDTH // LANES
U_VB = U_KB + B_WIDTH // LANES
U_GATE_B = U_VB + B_WIDTH // LANES
U_QM_EVEN = U_GATE_B + B_OUT // LANES
U_GM_EVEN = U_QM_EVEN + M_WIDTH // LANES
U_QM_ODD = 4 * C_WIDTH // LANES
U_GM_ODD = U_QM_ODD + M_WIDTH // LANES
NU = EVEN_IN_PAD // LANES


def _cparams(sem):
    return pltpu.CompilerParams(dimension_semantics=sem, vmem_limit_bytes=VMEM_LIMIT)


def _sigmoid(z):
    return 1.0 / (1.0 + jnp.exp(-z))


def _silu(z):
    return z * _sigmoid(z)


def _mm_kernel(x_ref, w_ref, o_ref):
    o_ref[...] = jnp.dot(x_ref[...], w_ref[0], preferred_element_type=F32)


def matmul(x, w, layer):
    m, k = x.shape
    n = w.shape[2]
    tm = min(m, 1024)
    tn = 1024 if n % 1024 == 0 else 512
    return pl.pallas_call(
        _mm_kernel,
        grid=(m // tm, n // tn),
        in_specs=[pl.BlockSpec((tm, k), lambda i, j: (i, 0)),
                  pl.BlockSpec((1, k, tn), lambda i, j: (layer, 0, j))],
        out_specs=pl.BlockSpec((tm, tn), lambda i, j: (i, j)),
        out_shape=jax.ShapeDtypeStruct((m, n), F32),
        compiler_params=_cparams(("parallel", "parallel")),
        name="matmul",
    )(x, w)


def _outproj_kernel(n_u, *refs):
    u_refs = refs[:n_u]
    w_refs = refs[n_u:2 * n_u]
    x_ref, g_ref, b_ref, o_ref, obf_ref = refs[2 * n_u:]
    acc = jnp.dot(u_refs[0][...], w_refs[0][0], preferred_element_type=F32)
    for u_ref, w_ref in zip(u_refs[1:], w_refs[1:]):
        acc = acc + jnp.dot(u_ref[...], w_ref[0], preferred_element_type=F32)
    z = ALPHA * x_ref[...] + acc
    mu = jnp.mean(z, axis=-1, keepdims=True)
    zc = z - mu
    var = jnp.mean(zc * zc, axis=-1, keepdims=True)
    y = zc * lax.rsqrt(var + LN_EPS) * g_ref[...] + b_ref[...]
    o_ref[...] = y
    obf_ref[...] = y.astype(BF16)


def outproj_ln(us, w_out, layer, x, g, b):
    m, d = x.shape
    tm = min(m, 512)
    n_u = len(us)
    in_specs = [pl.BlockSpec((tm, u.shape[1]), lambda i: (i, 0)) for u in us]
    row = 0
    for u in us:
        kw = u.shape[1]
        assert row % kw == 0
        in_specs.append(pl.BlockSpec((1, kw, d), functools.partial(lambda i, r: (layer, r, 0), r=row // kw)))
        row += kw
    assert row == w_out.shape[1]
    in_specs += [pl.BlockSpec((tm, d), lambda i: (i, 0)),
                 pl.BlockSpec((1, d), lambda i: (0, 0)),
                 pl.BlockSpec((1, d), lambda i: (0, 0))]
    return pl.pallas_call(
        functools.partial(_outproj_kernel, n_u),
        grid=(m // tm,),
        in_specs=in_specs,
        out_specs=[pl.BlockSpec((tm, d), lambda i: (i, 0)), pl.BlockSpec((tm, d), lambda i: (i, 0))],
        out_shape=[jax.ShapeDtypeStruct((m, d), F32), jax.ShapeDtypeStruct((m, d), BF16)],
        compiler_params=_cparams(("parallel",)),
        name="outproj_ln",
    )(*us, *([w_out] * n_u), x, g.reshape(1, d), b.reshape(1, d))


def _seg_sum(x, ones_bf):
    hi = x.astype(BF16)
    r1 = x - hi.astype(F32)
    mid = r1.astype(BF16)
    lo = (r1 - mid.astype(F32)).astype(BF16)
    outs = []
    for p in range(x.shape[1] // LANES):
        sl = slice(p * LANES, (p + 1) * LANES)
        terms = jnp.concatenate([hi[:, sl], mid[:, sl], lo[:, sl]], axis=1)
        outs.append(jnp.dot(terms, jnp.concatenate([ones_bf] * 3, axis=0), preferred_element_type=F32))
    return jnp.concatenate(outs, axis=1)


def _block_ones(width, seg):
    i = np.arange(width) // seg
    return jnp.asarray((i[:, None] == i[None, :]).astype(np.float32), dtype=BF16)


def _rwkv_prep_kernel(tt, tp, h_ref, prev_ref, sh0_ref, mu_ref, w0_ref, wup_ref, a0_ref, aup_ref,
                      kk_ref_p, ka_ref_p, rk_ref_p, ones_ref,
                      r_o, d_o, k_o, kk_o, b_o, vt_o, bonus_o, *scratch):
    i = pl.program_id(1)
    x = h_ref[0]
    pr = prev_ref[0]
    prev_last = pr[pr.shape[0] - 1:pr.shape[0], :]
    first = jnp.where(i == 0, sh0_ref[0], prev_last)
    row = lax.broadcasted_iota(jnp.int32, x.shape, 0)
    if tt % SUBLANES == 0:
        rolled = pltpu.roll(x, 1, axis=0)
    else:
        rolled = jnp.concatenate([x[tt - 1:tt], x[:tt - 1]], axis=0)
    prev = jnp.where(row == 0, first, rolled)
    hs = x + (prev - x) * mu_ref[...]
    r = hs[:, 0:A_WIDTH]
    k = hs[:, A_WIDTH:2 * A_WIDTH]
    v = hs[:, 2 * A_WIDTH:3 * A_WIDTH]
    hw = hs[:, 3 * A_WIDTH:3 * A_WIDTH + A_LORA]
    ha = hs[:, 3 * A_WIDTH + A_LORA:A_SHIFT_W]
    zw = w0_ref[...] + jnp.dot(jnp.tanh(hw).astype(BF16), wup_ref[...], preferred_element_type=F32)
    nz = -zw
    softplus = jnp.maximum(nz, 0.0) + jnp.log(1.0 + jnp.exp(-jnp.abs(nz)))
    w_log = -softplus - 0.5
    neg_log_decay = jnp.exp(w_log)
    a = _sigmoid(a0_ref[...] + jnp.dot(ha.astype(BF16), aup_ref[...], preferred_element_type=F32))
    ones_bf = ones_ref[...]
    kk = k * kk_ref_p[...]
    kk = kk * lax.rsqrt(jnp.maximum(_seg_sum(kk * kk, ones_bf), 1e-24))
    k2 = k * (1.0 + (a - 1.0) * ka_ref_p[...])
    bb = kk * a
    bonus_o[0] = _seg_sum(r * k2 * rk_ref_p[...], ones_bf) * v
    for hd in range(A_HEADS):
        sl = slice(hd * A_HEAD_DIM, (hd + 1) * A_HEAD_DIM)
        r_o[0, hd] = r[:, sl]
        d_o[0, hd] = neg_log_decay[:, sl]
        k_o[0, hd] = k2[:, sl]
        kk_o[0, hd] = kk[:, sl]
        b_o[0, hd] = bb[:, sl]
    for p in range(A_HEADS // 2):
        vp = v[:, p * LANES:(p + 1) * LANES]
        if tt == tp:
            vt_o[0, p] = vp.T
        else:
            pad = scratch[0]
            pad[...] = jnp.zeros_like(pad)
            pad[0:tt, :] = vp
            vt_o[0, p] = pad[...].T


def rwkv_prep(h3, shift0, prm, head_ones):
    bsz, t, _ = h3.shape
    tt = min(t, 128)
    tp = max(tt, LANES)
    nt = t // tt
    pr_rows = min(t, SUBLANES)
    pb = tt // pr_rows
    mu, w0, wup, a0, aup, k_k, k_a, r_k = prm
    row_spec = pl.BlockSpec((1, A_WIDTH), lambda b, i: (0, 0))
    lora_spec = pl.BlockSpec((A_LORA, A_WIDTH), lambda b, i: (0, 0))
    head_spec = pl.BlockSpec((1, A_HEADS, tt, A_HEAD_DIM), lambda b, i: (b, 0, i, 0))
    head_shape = jax.ShapeDtypeStruct((bsz, A_HEADS, t, A_HEAD_DIM), F32)
    scratch = [] if tt == tp else [pltpu.VMEM((tp, LANES), F32)]
    return pl.pallas_call(
        functools.partial(_rwkv_prep_kernel, tt, tp),
        grid=(bsz, nt),
        in_specs=[pl.BlockSpec((1, tt, A_SHIFT_W), lambda b, i: (b, i, 0)),
                  pl.BlockSpec((1, pr_rows, A_SHIFT_W), lambda b, i: (b, jnp.maximum(i * pb - 1, 0), 0)),
                  pl.BlockSpec((1, 1, A_SHIFT_W), lambda b, i: (b, 0, 0)),
                  pl.BlockSpec((1, A_SHIFT_W), lambda b, i: (0, 0)),
                  row_spec, lora_spec, row_spec, lora_spec, row_spec, row_spec, row_spec,
                  pl.BlockSpec((LANES, LANES), lambda b, i: (0, 0))],
        out_specs=[head_spec] * 5 + [
            pl.BlockSpec((1, A_HEADS // 2, LANES, tp), lambda b, i: (b, 0, 0, i)),
            pl.BlockSpec((1, tt, A_WIDTH), lambda b, i: (b, i, 0))],
        out_shape=[head_shape] * 5 + [
            jax.ShapeDtypeStruct((bsz, A_HEADS // 2, LANES, nt * tp), F32),
            jax.ShapeDtypeStruct((bsz, t, A_WIDTH), F32)],
        scratch_shapes=scratch,
        compiler_params=_cparams(("parallel", "parallel")),
        name="rwkv_prep",
    )(h3, h3, shift0.reshape(bsz, 1, A_SHIFT_W), mu.reshape(1, -1), w0.reshape(1, -1), wup.astype(BF16),
      a0.reshape(1, -1), aup.astype(BF16), k_k.reshape(1, -1), k_a.reshape(1, -1), r_k.reshape(1, -1), head_ones)


def _rwkv_scan_kernel(steps, r_ref, d_ref, k_ref, kk_ref, b_ref, vt_ref, s0_ref, yt_ref, sout_ref, s_scr, p_scr):
    c = pl.program_id(1)

    @pl.when(c == 0)
    def _():
        s_scr[...] = s0_ref[0]

    tp = vt_ref.shape[3]
    lane = lax.broadcasted_iota(jnp.int32, (A_HEAD_DIM, tp), 1)
    yt_ref[...] = jnp.zeros_like(yt_ref)
    p_scr[...] = jnp.zeros_like(p_scr)
    ones_bf = jnp.ones((A_HEAD_DIM, tp), BF16)

    def write_y(t):
        msk = lane == t
        for hd in range(A_HEADS):
            y = jnp.dot(p_scr[hd], ones_bf, preferred_element_type=F32)
            yt_ref[0, hd] = jnp.where(msk, y, yt_ref[0, hd])

    def body(t, carry):
        write_y(t - 1)
        msk = lane == t
        sas, vcols = [], []
        for hd in range(A_HEADS):
            sas.append(-jnp.sum(s_scr[hd] * kk_ref[0, hd, pl.ds(t, 1), :], axis=-1, keepdims=True))
            vcols.append(jnp.sum(jnp.where(msk, vt_ref[0, hd], 0.0), axis=-1, keepdims=True))
        for hd in range(A_HEADS):
            s = (s_scr[hd] * jnp.exp(-d_ref[0, hd, pl.ds(t, 1), :]) + sas[hd] * b_ref[0, hd, pl.ds(t, 1), :]
                 + vcols[hd] * k_ref[0, hd, pl.ds(t, 1), :])
            s_scr[hd] = s
            p_scr[hd] = (s * r_ref[0, hd, pl.ds(t, 1), :]).astype(BF16)
        return carry

    lax.fori_loop(0, steps, body, 0, unroll=4)
    write_y(steps - 1)

    @pl.when(c == pl.num_programs(1) - 1)
    def _():
        sout_ref[0] = s_scr[...]


A_CHUNK = 128
A_CHUNK_HEADS = 6
_NT = (((1,), (1,)), ((), ()))


def _split2(x):
    hi = x.astype(BF16)
    return hi, (x - hi.astype(F32)).astype(BF16)


def _mm3(a, b):
    ah, al = _split2(a)
    bh, bl = _split2(b)
    return jnp.dot(jnp.concatenate([ah, ah, al], axis=1), jnp.concatenate([bh, bl, bh], axis=0),
                   preferred_element_type=F32)


def _mm3_nt(a, b):
    ah, al = _split2(a)
    bh, bl = _split2(b)
    return lax.dot_general(jnp.concatenate([ah, ah, al], axis=1), jnp.concatenate([bh, bl, bh], axis=1), _NT,
                           preferred_element_type=F32)


def _pad_lanes(x):
    return jnp.concatenate([x, jnp.zeros_like(x)], axis=1)


def _inv_unit_lower(lbs, row, col):
    def blk(s):
        return (row // s) == (col // s)
    eye = jnp.where(row == col, 1.0, 0.0)
    ds = [jnp.where(blk(8), lb, 0.0) for lb in lbs]
    d2s = [_mm3(d, d) for d in ds]
    d4s = [_mm3(d2, d2) for d2 in d2s]
    xs = [eye + d for d in ds]
    xs = [x + _mm3(x, d2) for x, d2 in zip(xs, d2s)]
    xs = [x + _mm3(x, d4) for x, d4 in zip(xs, d4s)]
    s = 8
    while s < A_CHUNK:
        msk = blk(2 * s) & jnp.logical_not(blk(s))
        ts = [_mm3(x, jnp.where(msk, lb, 0.0)) for x, lb in zip(xs, lbs)]
        xs = [x + _mm3(t, x) for x, t in zip(xs, ts)]
        s *= 2
    return xs


def _rwkv_chunk_heads(es, rs, kks, bs, k2s, vs, s0s, tril, row, col):
    n_tok, n = A_CHUNK, A_HEAD_DIM
    idx = range(len(es))
    cs = []
    for e in es:
        e_hi = e.astype(BF16)
        e_r1 = e - e_hi.astype(F32)
        e_mid = e_r1.astype(BF16)
        e_lo = (e_r1 - e_mid.astype(F32)).astype(BF16)
        cs.append(jnp.dot(jnp.concatenate([tril, tril, tril], axis=1),
                          jnp.concatenate([e_hi, e_mid, e_lo], axis=0), preferred_element_type=F32))
    ms = [c[n_tok // 2 - 1:n_tok // 2] for c in cs]
    ccs = [c - m for c, m in zip(cs, ms)]
    gbs = [jnp.exp(cc) for cc in ccs]
    ats = [_pad_lanes(-kks[i] * jnp.exp(es[i] - ccs[i])) for i in idx]
    bts = [_pad_lanes(bs[i] * gbs[i]) for i in idx]
    kts = [_pad_lanes(k2s[i] * gbs[i]) for i in idx]
    rts = [_pad_lanes(rs[i] * jnp.exp(-ccs[i])) for i in idx]
    gls = [jnp.exp(-cc[n_tok - 1:n_tok]) for cc in ccs]
    bks = [jnp.concatenate([bts[i], kts[i]], axis=0) for i in idx]
    zs = [_mm3_nt(ats[i], bks[i]) for i in idx]
    m2s = [_mm3_nt(bks[i], rts[i]) for i in idx]
    low = row > col
    upp = row <= col
    lbs = [jnp.where(low, z[:, :n_tok], 0.0) for z in zs]
    lks = [jnp.where(low, z[:, n_tok:], 0.0) for z in zs]
    mbks = [jnp.concatenate([jnp.where(upp, m2[:n_tok], 0.0), jnp.where(upp, m2[n_tok:], 0.0)], axis=0)
            for m2 in m2s]
    ws = _inv_unit_lower(lbs, row, col)
    ptqs = [_mm3(ws[i], jnp.concatenate([ats[i], lks[i]], axis=1)) for i in idx]
    s0ps = [_pad_lanes(s0s[i] * jnp.exp(-ms[i])) for i in idx]
    srps = [_mm3_nt(s0ps[i], jnp.concatenate([rts[i], ptqs[i][:, :2 * n]], axis=0)) for i in idx]
    us = [srps[i][:, n_tok:] + _mm3_nt(vs[i], ptqs[i][:, 2 * n:]) for i in idx]
    yss = [_mm3(jnp.concatenate([us[i], vs[i]], axis=1), jnp.concatenate([mbks[i], bks[i]], axis=1)) for i in idx]
    ys = [srps[i][:, :n_tok] + yss[i][:, :n_tok] for i in idx]
    s1s = [(s0ps[i][:, :n] + yss[i][:, n_tok:n_tok + n]) * gls[i] for i in idx]
    return ys, s1s


def _rwkv_chunk_kernel(r_ref, e_ref, k_ref, kk_ref, b_ref, vt_ref, s0_ref, yt_ref, sout_ref, s_scr):
    c = pl.program_id(1)

    @pl.when(c == 0)
    def _():
        s_scr[...] = s0_ref[0]

    row = lax.broadcasted_iota(jnp.int32, (A_CHUNK, A_CHUNK), 0)
    col = lax.broadcasted_iota(jnp.int32, (A_CHUNK, A_CHUNK), 1)
    tril = jnp.where(row >= col, 1.0, 0.0).astype(BF16)

    def group(gi, carry):
        hds = [gi * A_CHUNK_HEADS + j for j in range(A_CHUNK_HEADS)]
        ys, s1s = _rwkv_chunk_heads([e_ref[0, hd] for hd in hds], [r_ref[0, hd] for hd in hds],
                                    [kk_ref[0, hd] for hd in hds], [b_ref[0, hd] for hd in hds],
                                    [k_ref[0, hd] for hd in hds], [vt_ref[0, hd] for hd in hds],
                                    [s_scr[hd] for hd in hds], tril, row, col)
        for hd, y, s1 in zip(hds, ys, s1s):
            yt_ref[0, hd] = y
            s_scr[hd] = s1
        return carry

    lax.fori_loop(0, A_HEADS // A_CHUNK_HEADS, group, 0)

    @pl.when(c == pl.num_programs(1) - 1)
    def _():
        sout_ref[0] = s_scr[...]


def rwkv_scan(ops, vt, s0):
    r, d, k, kk, b = ops
    bsz, _, t, _ = r.shape
    tt = min(t, 128)
    tp = max(tt, LANES)
    nt = t // tt
    vt = vt.reshape(bsz, A_HEADS, A_HEAD_DIM, nt * tp)
    head_spec = pl.BlockSpec((1, A_HEADS, tt, A_HEAD_DIM), lambda b_, i: (b_, 0, i, 0))
    vt_spec = pl.BlockSpec((1, A_HEADS, A_HEAD_DIM, tp), lambda b_, i: (b_, 0, 0, i))
    st_spec = pl.BlockSpec((1, A_HEADS, A_HEAD_DIM, A_HEAD_DIM), lambda b_, i: (b_, 0, 0, 0))
    chunked = t % A_CHUNK == 0
    scratch = [pltpu.VMEM((A_HEADS, A_HEAD_DIM, A_HEAD_DIM), F32)]
    if not chunked:
        scratch.append(pltpu.VMEM((A_HEADS, A_HEAD_DIM, A_HEAD_DIM), BF16))
    yt, s_new = pl.pallas_call(
        _rwkv_chunk_kernel if chunked else functools.partial(_rwkv_scan_kernel, tt),
        grid=(bsz, nt),
        in_specs=[head_spec] * 5 + [vt_spec, st_spec],
        out_specs=[vt_spec, st_spec],
        out_shape=[jax.ShapeDtypeStruct(vt.shape, F32), jax.ShapeDtypeStruct(s0.shape, F32)],
        scratch_shapes=scratch,
        compiler_params=_cparams(("parallel", "arbitrary")),
        name="rwkv_chunk" if chunked else "rwkv_scan",
    )(r, d, k, kk, b, vt, s0)
    return yt.reshape(bsz, A_HEADS // 2, LANES, nt * tp), s_new


def _dwa_prompt_kernel(dil, *refs):
    npair = B_OUT // LANES
    in_refs = [refs[5 * hp:5 * hp + 5] for hp in range(npair)]
    out_refs = refs[5 * npair:]
    n = pl.program_id(1)
    i = lax.broadcasted_iota(jnp.int32, (B_BLK, B_BLK), 0)
    j = lax.broadcasted_iota(jnp.int32, (B_BLK, B_BLK), 1)
    mask_prev = j >= i + jnp.where(n > 0, 0, B_BLK)
    mask_cur = j <= i
    neg = -jnp.inf
    scale = B_HEAD_DIM ** -0.5
    dn = (((1,), (1,)), ((), ()))

    n_res = min(dil, B_RES_PER_ITER)
    n_hh = LANES // B_HEAD_DIM

    def residues(it, carry):
        rows = [pl.ds(it * n_res + k, B_BLK, stride=dil) if dil > 1 else pl.ds(0, B_BLK) for k in range(n_res)]
        units = [(k, hp, hh) for k in range(n_res) for hp in range(npair) for hh in range(n_hh)]
        loaded = {(k, hp): [ref[0, rows[k], :] for ref in in_refs[hp]] for k in range(n_res) for hp in range(npair)}

        def part(u, which):
            k, hp, hh = u
            return loaded[(k, hp)][which][:, hh * B_HEAD_DIM:(hh + 1) * B_HEAD_DIM].astype(BF16)

        qs = [part(u, 0) for u in units]
        sps = [jnp.where(mask_prev, lax.dot_general(q, part(u, 2), dn, preferred_element_type=F32) * scale, neg)
               for q, u in zip(qs, units)]
        scs = [jnp.where(mask_cur, lax.dot_general(q, part(u, 1), dn, preferred_element_type=F32) * scale, neg)
               for q, u in zip(qs, units)]
        ms = [jnp.maximum(jnp.max(sp, axis=-1, keepdims=True), jnp.max(sc, axis=-1, keepdims=True))
              for sp, sc in zip(sps, scs)]
        eps = [jnp.exp(sp - m) for sp, m in zip(sps, ms)]
        ecs = [jnp.exp(sc - m) for sc, m in zip(scs, ms)]
        ls = [jnp.sum(ep, axis=-1, keepdims=True) + jnp.sum(ec, axis=-1, keepdims=True) for ep, ec in zip(eps, ecs)]
        os_ = [jnp.dot((ep / l).astype(BF16), part(u, 4), preferred_element_type=F32)
               + jnp.dot((ec / l).astype(BF16), part(u, 3), preferred_element_type=F32)
               for ep, ec, l, u in zip(eps, ecs, ls, units)]
        lses = [jnp.broadcast_to(m + jnp.log(l), (B_BLK, B_HEAD_DIM)) for m, l in zip(ms, ls)]
        for k in range(n_res):
            for hp in range(npair):
                sel = [i_u for i_u, u in enumerate(units) if u[0] == k and u[1] == hp]
                out_refs[2 * hp][0, rows[k], :] = jnp.concatenate([os_[i_u] for i_u in sel], axis=-1)
                out_refs[2 * hp + 1][0, rows[k], :] = jnp.concatenate([lses[i_u] for i_u in sel], axis=-1)
        return carry

    lax.fori_loop(0, dil // n_res, residues, 0)


def dwa_prompt(h3, g):
    bsz, t, _ = h3.shape
    win, dil = B_CONFIGS[g]
    assert win == B_BLK * dil and t % win == 0
    npair = B_OUT // LANES

    def in_spec(unit, prev, hp):
        def imap(b, n):
            nn = jnp.maximum(n - 1, 0) if prev else n
            return (b, nn, unit + npair * g + hp)
        return pl.BlockSpec((1, win, LANES), imap)

    in_specs = []
    for hp in range(npair):
        in_specs += [in_spec(U_QB, False, hp), in_spec(U_KB, False, hp), in_spec(U_KB, True, hp),
                     in_spec(U_VB, False, hp), in_spec(U_VB, True, hp)]
    out_spec = pl.BlockSpec((1, win, LANES), lambda b, n: (b, n, 0))
    out_shape = jax.ShapeDtypeStruct((bsz, t, LANES), F32)
    outs = pl.pallas_call(
        functools.partial(_dwa_prompt_kernel, dil),
        grid=(bsz, t // win),
        in_specs=in_specs,
        out_specs=[out_spec] * (2 * npair),
        out_shape=[out_shape] * (2 * npair),
        compiler_params=_cparams(("parallel", "parallel")),
        name="dwa_prompt",
    )(*([h3] * len(in_specs)))
    return list(outs)


def _dwa_decode_kernel(t_new, h_ref, c0_ref, c1_ref, c2_ref, *out_refs):
    scale = B_HEAD_DIM ** -0.5
    dn = (((1,), (1,)), ((), ()))
    head_of_lane = lax.broadcasted_iota(jnp.int32, (SUBLANES, B_OUT), 1) // B_HEAD_DIM
    sub = lax.broadcasted_iota(jnp.int32, (SUBLANES, B_OUT), 0)
    own = head_of_lane == sub
    m_idx = lax.broadcasted_iota(jnp.int32, (SUBLANES, B_BLK), 1)
    neg = -jnp.inf
    hrow = h_ref[0]
    for g, c_ref in enumerate((c0_ref, c1_ref, c2_ref)):
        _, dil = B_CONFIGS[g]
        g_refs = out_refs[4 * g:4 * g + 4]
        qs = hrow[:, U_QB * LANES + g * B_OUT:U_QB * LANES + (g + 1) * B_OUT]
        ks = hrow[:, U_KB * LANES + g * B_OUT:U_KB * LANES + (g + 1) * B_OUT]
        vs = hrow[:, U_VB * LANES + g * B_OUT:U_VB * LANES + (g + 1) * B_OUT]
        for t in range(t_new):
            res = t % dil
            kbuf = jnp.concatenate([c_ref[0, 0, :, res, 0, hd, :] for hd in range(B_HEADS_PER_GROUP)], axis=-1)
            vbuf = jnp.concatenate([c_ref[0, 0, :, res, 1, hd, :] for hd in range(B_HEADS_PER_GROUP)], axis=-1)
            qbd = jnp.where(own, jnp.broadcast_to(qs[t:t + 1], (SUBLANES, B_OUT)), 0.0)
            s_buf = lax.dot_general(qbd.astype(BF16), kbuf.astype(BF16), dn, preferred_element_type=F32) * scale
            s_buf = jnp.where(m_idx * dil + res >= t, s_buf, neg)
            new_rows = [n for n in range(t + 1) if (t - n) % dil == 0]
            kq = qbd.astype(BF16).astype(F32)
            s_new = [jnp.sum(kq * ks[n:n + 1].astype(BF16).astype(F32), axis=-1, keepdims=True) * scale
                     for n in new_rows]
            m = jnp.max(s_buf, axis=-1, keepdims=True)
            for sn in s_new:
                m = jnp.maximum(m, sn)
            e_buf = jnp.exp(s_buf - m)
            e_new = [jnp.exp(sn - m) for sn in s_new]
            l = jnp.sum(e_buf, axis=-1, keepdims=True)
            for en in e_new:
                l = l + en
            o = jnp.dot((e_buf / l).astype(BF16), vbuf.astype(BF16), preferred_element_type=F32)
            for n, en in zip(new_rows, e_new):
                o = o + (en / l).astype(BF16).astype(F32) * vs[n:n + 1].astype(BF16).astype(F32)
            lse = jnp.broadcast_to(m + jnp.log(l), (SUBLANES, B_OUT))
            o_row = jnp.sum(jnp.where(own, o, 0.0), axis=0, keepdims=True)
            lse_row = jnp.sum(jnp.where(own, lse, 0.0), axis=0, keepdims=True)
            for hp in range(B_OUT // LANES):
                g_refs[2 * hp][0, t:t + 1, :] = o_row[:, hp * LANES:(hp + 1) * LANES]
                g_refs[2 * hp + 1][0, t:t + 1, :] = lse_row[:, hp * LANES:(hp + 1) * LANES]


def dwa_decode(h3, caches, e):
    bsz, t, _ = h3.shape
    views, specs = [], []
    for g, (win, dil) in enumerate(B_CONFIGS):
        assert caches[g].shape[2] == win and (dil == 1 or t <= dil)
        views.append(caches[g].reshape(caches[g].shape[0], bsz, win // dil, dil, 2, B_HEADS_PER_GROUP, B_HEAD_DIM))
        specs.append(pl.BlockSpec((1, 1, B_BLK, min(dil, t), 2, B_HEADS_PER_GROUP, B_HEAD_DIM),
                                  lambda b: (e, b, 0, 0, 0, 0, 0)))
    out_spec = pl.BlockSpec((1, t, LANES), lambda b: (b, 0, 0))
    out_shape = jax.ShapeDtypeStruct((bsz, t, LANES), F32)
    n_out = 4 * B_GROUPS
    outs = pl.pallas_call(
        functools.partial(_dwa_decode_kernel, t),
        grid=(bsz,),
        in_specs=[pl.BlockSpec((1, t, NU * LANES), lambda b: (b, 0, 0))] + specs,
        out_specs=[out_spec] * n_out,
        out_shape=[out_shape] * n_out,
        compiler_params=_cparams(("parallel",)),
        name="dwa_decode",
    )(h3, *views)
    return [list(outs[4 * g:4 * g + 4]) for g in range(B_GROUPS)]


def _even_post_kernel(tt, yt_ref, bonus_ref, *refs):
    n_ga = A_WIDTH // LANES
    ga_refs = refs[:n_ga]
    gb0_ref, gb1_ref, lnxg_ref, lnxb_ref, ones_ref = refs[n_ga:n_ga + 5]
    dwa_refs = refs[n_ga + 5:n_ga + 5 + 4 * B_GROUPS]
    ua_ref, ub_ref = refs[n_ga + 5 + 4 * B_GROUPS:]
    ys = []
    for p in range(A_HEADS // 2):
        ys.append(yt_ref[0, p].T[0:tt, :])
    y = jnp.concatenate(ys, axis=-1)
    ones_bf = ones_ref[...]
    inv = 1.0 / A_HEAD_DIM
    mu = _seg_sum(y, ones_bf) * inv
    yc = y - mu
    var = _seg_sum(yc * yc, ones_bf) * inv
    ya = yc * lax.rsqrt(var + A_GN_EPS) * lnxg_ref[...] + lnxb_ref[...] + bonus_ref[0]
    gate_a = jnp.concatenate([g_ref[0] for g_ref in ga_refs], axis=-1)
    ua_ref[0] = (ya * _silu(gate_a)).astype(BF16)
    ybs = []
    for hp, gb_ref in enumerate((gb0_ref, gb1_ref)):
        o0, l0, o1, l1, o2, l2 = (dwa_refs[4 * g + 2 * hp + k][0] for g in range(B_GROUPS) for k in range(2))
        m = jnp.maximum(jnp.maximum(l0, l1), l2)
        e0, e1, e2 = jnp.exp(l0 - m), jnp.exp(l1 - m), jnp.exp(l2 - m)
        den = e0 + e1 + e2
        yb = (e0 / den) * o0 + (e1 / den) * o1 + (e2 / den) * o2
        ybs.append(yb * _silu(gb_ref[0]))
    ub_ref[0] = jnp.concatenate(ybs, axis=-1).astype(BF16)


def even_post(yt, bonus, h3, lnx_g, lnx_b, head_ones, dwa):
    bsz, t, _ = h3.shape
    tt = min(t, 128)
    tp = max(tt, LANES)
    nt = t // tt
    dwa_flat = [a for grp in dwa for a in grp]
    bspec = pl.BlockSpec((1, tt, B_OUT), lambda b, i: (b, i, 0))
    pspec = pl.BlockSpec((1, tt, LANES), lambda b, i: (b, i, 0))
    row_spec = pl.BlockSpec((1, A_WIDTH), lambda b, i: (0, 0))
    return pl.pallas_call(
        functools.partial(_even_post_kernel, tt),
        grid=(bsz, nt),
        in_specs=[pl.BlockSpec((1, A_HEADS // 2, LANES, tp), lambda b, i: (b, 0, 0, i)),
                  pl.BlockSpec((1, tt, A_WIDTH), lambda b, i: (b, i, 0)),
                  *[pl.BlockSpec((1, tt, LANES), functools.partial(lambda b, i, u: (b, i, u), u=U_GATE_A + u))
                    for u in range(A_WIDTH // LANES)],
                  pl.BlockSpec((1, tt, LANES), lambda b, i: (b, i, U_GATE_B)),
                  pl.BlockSpec((1, tt, LANES), lambda b, i: (b, i, U_GATE_B + 1)),
                  row_spec, row_spec,
                  pl.BlockSpec((LANES, LANES), lambda b, i: (0, 0))] + [pspec] * len(dwa_flat),
        out_specs=[pl.BlockSpec((1, tt, A_WIDTH), lambda b, i: (b, i, 0)), bspec],
        out_shape=[jax.ShapeDtypeStruct((bsz, t, A_WIDTH), BF16), jax.ShapeDtypeStruct((bsz, t, B_OUT), BF16)],
        compiler_params=_cparams(("parallel", "parallel")),
        name="even_post",
    )(yt, bonus, *([h3] * (A_WIDTH // LANES + 2)), lnx_g.reshape(1, -1), lnx_b.reshape(1, -1), head_ones, *dwa_flat)


def _mem_attn_kernel(*refs):
    q_refs, g_refs = refs[:M_HEADS], refs[M_HEADS:2 * M_HEADS]
    kv_ref, u_ref = refs[2 * M_HEADS:]
    dn = (((1,), (1,)), ((), ()))
    hds = range(M_HEADS)
    ss = [lax.dot_general(q_refs[hd][0].astype(BF16), kv_ref[0, 0, :, 0, hd, :].astype(BF16), dn,
                          preferred_element_type=F32) * (M_HEAD_DIM ** -0.5) for hd in hds]
    ms = [jnp.max(s, axis=-1, keepdims=True) for s in ss]
    es = [jnp.exp(s - m) for s, m in zip(ss, ms)]
    ps = [e / jnp.sum(e, axis=-1, keepdims=True) for e in es]
    os_ = [jnp.dot(ps[hd].astype(BF16), kv_ref[0, 0, :, 1, hd, :].astype(BF16), preferred_element_type=F32)
           for hd in hds]
    u_ref[0] = jnp.concatenate([os_[hd] * _silu(g_refs[hd][0]) for hd in hds], axis=-1).astype(BF16)


def mem_attn(h3, mkv, layer, u_q, u_g):
    bsz, t, _ = h3.shape
    tq = min(t, 256)

    def col_spec(u):
        return pl.BlockSpec((1, tq, LANES), lambda b, i: (b, i, u))

    return pl.pallas_call(
        _mem_attn_kernel,
        grid=(bsz, t // tq),
        in_specs=[col_spec(u_q + hd) for hd in range(M_HEADS)] + [col_spec(u_g + hd) for hd in range(M_HEADS)] + [
            pl.BlockSpec((1, 1, M_TOKENS, 2, M_HEADS, M_HEAD_DIM), lambda b, i: (layer, b, 0, 0, 0, 0))],
        out_specs=pl.BlockSpec((1, tq, M_WIDTH), lambda b, i: (b, i, 0)),
        out_shape=jax.ShapeDtypeStruct((bsz, t, M_WIDTH), BF16),
        compiler_params=_cparams(("parallel", "parallel")),
        name="mem_attn",
    )(*([h3] * (2 * M_HEADS)), mkv)


def _rope_kernel(pos0, ang_ref, cos_ref, sin_ref):
    rows = cos_ref.shape[0]
    base = pl.program_id(0) * rows
    pos = (lax.broadcasted_iota(jnp.int32, cos_ref.shape, 0) + base).astype(F32) + pos0
    ph = pos * ang_ref[...]
    cos_ref[...] = jnp.cos(ph)
    sin_ref[...] = jnp.sin(ph)


def rope_tables(rows, pos0):
    angle = 1.0 / (C_ROT_BASE ** jnp.linspace(0.0, 1.0, C_HEAD_DIM // 2, dtype=F32))
    ang = jnp.repeat(angle, 2).reshape(1, C_HEAD_DIM)
    tr = min(rows, 512)
    spec = pl.BlockSpec((tr, C_HEAD_DIM), lambda i: (i, 0))
    shape = jax.ShapeDtypeStruct((rows, C_HEAD_DIM), F32)
    return pl.pallas_call(
        functools.partial(_rope_kernel, float(pos0)),
        grid=(rows // tr,),
        in_specs=[pl.BlockSpec((1, C_HEAD_DIM), lambda i: (0, 0))],
        out_specs=[spec, spec],
        out_shape=[shape, shape],
        compiler_params=_cparams(("parallel",)),
        name="rope_tables",
    )(ang)


def _rot_pairs(z):
    even = lax.broadcasted_iota(jnp.int32, (z.shape[0], LANES), 1) % 2 == 0
    parts = []
    for blk in range(z.shape[1] // LANES):
        zb = z[:, blk * LANES:(blk + 1) * LANES]
        nxt = pltpu.roll(zb, LANES - 1, axis=1)
        prv = pltpu.roll(zb, 1, axis=1)
        parts.append(jnp.where(even, -nxt, prv))
    return jnp.concatenate(parts, axis=-1)


def _retention_kernel(tb, chunk, lg_ref, q_ref, k_ref, v_ref, g_ref, cos_ref, sin_ref, r0_ref,
                      u_ref, rout_ref, r_scr, *pad):
    c = pl.program_id(2)

    @pl.when(c == 0)
    def _():
        r_scr[...] = r0_ref[0]

    if tb == C_CHUNK:
        q2, k2, v2 = q_ref[0], k_ref[0], v_ref[0]
    else:
        vals = []
        for src, buf in zip((q_ref, k_ref, v_ref), pad):
            buf[...] = jnp.zeros_like(buf)
            buf[0:tb, :] = src[0]
            vals.append(buf[...])
        q2, k2, v2 = vals
    cos, sin = cos_ref[...], sin_ref[...]
    ii = lax.broadcasted_iota(jnp.int32, (C_CHUNK, C_CHUNK), 0)
    jj = lax.broadcasted_iota(jnp.int32, (C_CHUNK, C_CHUNK), 1)
    diff = (ii - jj).astype(F32)
    idx = lax.broadcasted_iota(jnp.int32, (C_CHUNK, 1), 0).astype(F32)
    dn = (((1,), (1,)), ((), ()))
    hh_r = range(C_HEADS_PER_STEP)
    sls = [slice(hh * C_HEAD_DIM, (hh + 1) * C_HEAD_DIM) for hh in hh_r]
    lgs = [lg_ref[pl.program_id(1) * C_HEADS_PER_STEP + hh] for hh in hh_r]
    qrs = [q2[:, sl] * cos + _rot_pairs(q2[:, sl]) * sin for sl in sls]
    krs = [(k2[:, sl] * cos + _rot_pairs(k2[:, sl]) * sin) * (C_HEAD_DIM ** -0.5) for sl in sls]
    qbs = [qr.astype(BF16) for qr in qrs]
    vbs = [v2[:, sl].astype(BF16) for sl in sls]
    scs = [lax.dot_general(qb, kr.astype(BF16), dn, preferred_element_type=F32)
           * jnp.where(diff >= 0, jnp.exp(lg * jnp.maximum(diff, 0.0)), 0.0) for qb, kr, lg in zip(qbs, krs, lgs)]
    r_olds = [r_scr[hh] for hh in hh_r]
    kz_ts = [(kr * jnp.exp(lg * (chunk - 1.0 - idx))).T.astype(BF16) for kr, lg in zip(krs, lgs)]
    os_ = [jnp.dot(sc.astype(BF16), vb, preferred_element_type=F32)
           + jnp.dot(qb, r_old.astype(BF16), preferred_element_type=F32) * jnp.exp(lg * (idx + 1.0))
           for sc, vb, qb, r_old, lg in zip(scs, vbs, qbs, r_olds, lgs)]
    for hh in hh_r:
        r_scr[hh] = (r_olds[hh] * jnp.exp(lgs[hh] * jnp.full((1, 1), chunk, F32))
                     + jnp.dot(kz_ts[hh], vbs[hh], preferred_element_type=F32))
    us = []
    for hh in hh_r:
        o = os_[hh][0:tb]
        y = o * lax.rsqrt(jnp.mean(o * o, axis=-1, keepdims=True) + C_NORM_EPS)
        us.append(y * _silu(g_ref[0][:, sls[hh]]))
    u_ref[0] = jnp.concatenate(us, axis=-1).astype(BF16)

    @pl.when(c == pl.num_programs(2) - 1)
    def _():
        rout_ref[0] = r_scr[...]


def retention(h3, cos, sin, r0):
    bsz, t, _ = h3.shape
    tb = min(t, C_CHUNK)
    nc = t // tb
    chunk = float(tb)
    lg = jnp.log(1.0 - 2.0 ** (-5.0 - jnp.arange(C_HEADS, dtype=F32)))
    hps = C_HEADS_PER_STEP
    width = hps * C_HEAD_DIM
    nq = C_WIDTH // width

    def col_spec(off):
        return pl.BlockSpec((1, tb, width), lambda b, hd, c: (b, c, off + hd))

    tab_spec = pl.BlockSpec((C_CHUNK, C_HEAD_DIM), lambda b, hd, c: (c, 0))
    st_spec = pl.BlockSpec((1, hps, C_HEAD_DIM, C_HEAD_DIM), lambda b, hd, c: (b, hd, 0, 0))
    scratch = [pltpu.VMEM((hps, C_HEAD_DIM, C_HEAD_DIM), F32)]
    if tb != C_CHUNK:
        scratch += [pltpu.VMEM((C_CHUNK, width), F32)] * 3
    return pl.pallas_call(
        functools.partial(_retention_kernel, tb, chunk),
        grid=(bsz, C_HEADS // hps, nc),
        in_specs=[pl.BlockSpec(memory_space=pltpu.SMEM),
                  col_spec(0), col_spec(nq), col_spec(2 * nq), col_spec(3 * nq),
                  tab_spec, tab_spec, st_spec],
        out_specs=[pl.BlockSpec((1, tb, width), lambda b, hd, c: (b, c, hd)), st_spec],
        out_shape=[jax.ShapeDtypeStruct((bsz, t, C_WIDTH), BF16), jax.ShapeDtypeStruct(r0.shape, F32)],
        scratch_shapes=scratch,
        compiler_params=_cparams(("parallel", "parallel", "arbitrary")),
        name="retention",
    )(lg, h3, h3, h3, h3, cos, sin, r0)


def _even_layer(x, x_bf, mkv, dwa_bufs, s0, shift0, w_in_bf, w_out_bf, e, ln_g, ln_b, rw, head_ones):
    bsz, t, d = x.shape
    h = matmul(x_bf.reshape(bsz * t, d), w_in_bf, e)
    h3 = h.reshape(bsz, t, NU * LANES)
    mu, w0, wup, a0, aup, k_k, k_a, r_k, lnx_g, lnx_b = rw
    r, dcy, k2, kk, bb, vt, bonus = rwkv_prep(h3, shift0, (mu, w0, wup, a0, aup, k_k, k_a, r_k), head_ones)
    yt, s_new = rwkv_scan((r, dcy, k2, kk, bb), vt, s0)
    if dwa_bufs is None:
        dwa = [dwa_prompt(h3, g) for g in range(B_GROUPS)]
    else:
        dwa = dwa_decode(h3, dwa_bufs, e)
    u_a, u_b = even_post(yt, bonus, h3, lnx_g, lnx_b, head_ones, dwa)
    u_m = mem_attn(h3, mkv[0], mkv[1], U_QM_EVEN, U_GM_EVEN)
    x2, x2_bf = outproj_ln([u_a.reshape(bsz * t, -1), u_b.reshape(bsz * t, -1), u_m.reshape(bsz * t, -1)],
                           w_out_bf, e, x.reshape(bsz * t, d), ln_g, ln_b)
    rows = []
    for g, (win, _) in enumerate(B_CONFIGS):
        keep = t if dwa_bufs is not None else min(win, t)
        kg = h3[:, t - keep:, U_KB * LANES + g * B_OUT:U_KB * LANES + (g + 1) * B_OUT]
        vg = h3[:, t - keep:, U_VB * LANES + g * B_OUT:U_VB * LANES + (g + 1) * B_OUT]
        rows.append(jnp.stack([kg.reshape(bsz, keep, B_HEADS_PER_GROUP, B_HEAD_DIM),
                               vg.reshape(bsz, keep, B_HEADS_PER_GROUP, B_HEAD_DIM)], axis=2))
    return x2.reshape(bsz, t, d), x2_bf.reshape(bsz, t, d), s_new, h3[:, t - 1, :A_SHIFT_W], rows


def _odd_layer(x, x_bf, mkv, r0, tabs, w_in_bf, w_out_bf, o, ln_g, ln_b):
    bsz, t, d = x.shape
    h = matmul(x_bf.reshape(bsz * t, d), w_in_bf, o)
    h3 = h.reshape(bsz, t, NU * LANES)
    u_c, r_new = retention(h3, tabs[0], tabs[1], r0)
    u_m = mem_attn(h3, mkv[0], mkv[1], U_QM_ODD, U_GM_ODD)
    x2, x2_bf = outproj_ln([u_c.reshape(bsz * t, -1), u_m.reshape(bsz * t, -1)],
                           w_out_bf, o, x.reshape(bsz * t, d), ln_g, ln_b)
    return x2.reshape(bsz, t, d), x2_bf.reshape(bsz, t, d), r_new


def kernel(x_prompt, x_sample, state_rwkv, state_rwkv_shift, cache_dwa_g0, cache_dwa_g1, cache_dwa_g2, state_ret, cache_mem_kv, mem_prompt, w_in_even, w_out_even, w_in_odd, w_out_odd, w_mem_kv, ln_g, ln_b, rwkv_mu, rwkv_w0, rwkv_w_up, rwkv_a0, rwkv_a_up, rwkv_k_k, rwkv_k_a, rwkv_r_k, rwkv_lnx_g, rwkv_lnx_b):
    xp, xs = x_prompt, x_sample
    xp_bf, xs_bf = xp.astype(BF16), xs.astype(BF16)
    bp, tp_len, d = xp.shape
    bs, ts_len, _ = xs.shape
    dwa_cache = (cache_dwa_g0, cache_dwa_g1, cache_dwa_g2)
    head_ones = _block_ones(LANES, A_HEAD_DIM)
    mem_bf = mem_prompt.reshape(bp * M_TOKENS, d).astype(BF16)
    tabs_p = rope_tables(max(tp_len, C_CHUNK), 0)
    tabs_s = rope_tables(max(ts_len, C_CHUNK), PAST_LEN)
    rwkv_p, rwkv_s, shift_p, shift_s, ret_p, ret_s, mem_p = [], [], [], [], [], [], []
    dwa_p = [[] for _ in B_CONFIGS]
    dwa_s = [[] for _ in B_CONFIGS]
    w_in_even_bf = jnp.concatenate(
        [w_in_even.astype(BF16), jnp.zeros(w_in_even.shape[:2] + (EVEN_IN_PAD - EVEN_IN,), BF16)], axis=-1)
    w_out_even_bf = w_out_even.astype(BF16)
    w_in_odd_bf = w_in_odd.astype(BF16)
    w_out_odd_bf = w_out_odd.astype(BF16)
    w_mem_bf = w_mem_kv.astype(BF16)
    for l in range(DEPTH):
        mkv_new = matmul(mem_bf, w_mem_bf, l).reshape(1, bp, M_TOKENS, 2, M_HEADS, M_HEAD_DIM)
        mem_p.append(mkv_new[0])
        mkv_p = (mkv_new, 0)
        mkv_s = (cache_mem_kv, l)
        if l % 2 == 0:
            e = l // 2
            w_in_bf, w_out_bf = w_in_even_bf, w_out_even_bf
            rw = (rwkv_mu[e], rwkv_w0[e], rwkv_w_up[e], rwkv_a0[e], rwkv_a_up[e], rwkv_k_k[e], rwkv_k_a[e],
                  rwkv_r_k[e], rwkv_lnx_g[e], rwkv_lnx_b[e])
            s0 = jnp.zeros((bp, A_HEADS, A_HEAD_DIM, A_HEAD_DIM), F32)
            sh0 = jnp.zeros((bp, A_SHIFT_W), F32)
            xp, xp_bf, st, sh, rows = _even_layer(xp, xp_bf, mkv_p, None, s0, sh0, w_in_bf, w_out_bf, e,
                                                  ln_g[l], ln_b[l], rw, head_ones)
            rwkv_p.append(st)
            shift_p.append(sh)
            for g in range(B_GROUPS):
                dwa_p[g].append(rows[g])
            bufs = dwa_cache
            xs, xs_bf, st, sh, rows = _even_layer(xs, xs_bf, mkv_s, bufs, state_rwkv[e], state_rwkv_shift[e],
                                                  w_in_bf, w_out_bf, e, ln_g[l], ln_b[l], rw, head_ones)
            rwkv_s.append(st)
            shift_s.append(sh)
            for g in range(B_GROUPS):
                dwa_s[g].append(rows[g])
        else:
            o = l // 2
            r0 = jnp.zeros((bp, C_HEADS, C_HEAD_DIM, C_HEAD_DIM), F32)
            xp, xp_bf, st = _odd_layer(xp, xp_bf, mkv_p, r0, tabs_p, w_in_odd_bf, w_out_odd_bf, o, ln_g[l], ln_b[l])
            ret_p.append(st)
            xs, xs_bf, st = _odd_layer(xs, xs_bf, mkv_s, state_ret[o], tabs_s, w_in_odd_bf, w_out_odd_bf, o,
                                       ln_g[l], ln_b[l])
            ret_s.append(st)
    return (xp, xs, jnp.stack(rwkv_p), jnp.stack(rwkv_s), jnp.stack(shift_p), jnp.stack(shift_s),
            jnp.stack(dwa_p[0]), jnp.stack(dwa_s[0]), jnp.stack(dwa_p[1]), jnp.stack(dwa_s[1]),
            jnp.stack(dwa_p[2]), jnp.stack(dwa_s[2]), jnp.stack(ret_p), jnp.stack(ret_s), jnp.stack(mem_p))
```

```python
import functools

import numpy as np
import jax
import jax.numpy as jnp
from jax import lax
from jax.experimental import pallas as pl
from jax.experimental.pallas import tpu as pltpu

F32 = jnp.float32
BF16 = jnp.bfloat16

D_MODEL = 2048
DEPTH = 4
PAST_LEN = 16384
ALPHA = (2 * DEPTH) ** 0.25
LN_EPS = 1e-5
A_HEADS = 12
A_HEAD_DIM = 64
A_WIDTH = A_HEADS * A_HEAD_DIM
A_LORA = 64
A_SHIFT_W = 3 * A_WIDTH + 2 * A_LORA
A_GN_EPS = 64e-5
B_CONFIGS = ((128, 1), (512, 4), (2048, 16))
B_GROUPS = 3
B_HEADS_PER_GROUP = 4
B_HEAD_DIM = 64
B_OUT = B_HEADS_PER_GROUP * B_HEAD_DIM
B_WIDTH = B_GROUPS * B_OUT
B_BLK = 128
B_RES_PER_ITER = 2
C_HEADS = 6
C_HEAD_DIM = 256
C_WIDTH = C_HEADS * C_HEAD_DIM
C_CHUNK = 128
C_HEADS_PER_STEP = 3
C_ROT_BASE = 10000.0
C_NORM_EPS = 1e-6
M_TOKENS = 256
M_HEADS = 4
M_HEAD_DIM = 128
M_WIDTH = M_HEADS * M_HEAD_DIM
EVEN_IN = A_SHIFT_W + A_WIDTH + 3 * B_WIDTH + B_OUT + 2 * M_WIDTH
ODD_IN = 4 * C_WIDTH + 2 * M_WIDTH

LANES = 128
SUBLANES = 8
VMEM_LIMIT = 48 * 1024 * 1024

EVEN_IN_PAD = 7168
U_GATE_A = A_SHIFT_W // LANES
U_QB = U_GATE_A + A_WIDTH // LANES
U_KB = U_QB + B_WIDTH // LANES
U_VB = U_KB + B_WIDTH // LANES
U_GATE_B = U_VB + B_WIDTH // LANES
U_QM_EVEN = U_GATE_B + B_OUT // LANES
U_GM_EVEN = U_QM_EVEN + M_WIDTH // LANES
U_QM_ODD = 4 * C_WIDTH // LANES
U_GM_ODD = U_QM_ODD + M_WIDTH // LANES
NU = EVEN_IN_PAD // LANES


def _cparams(sem):
    return pltpu.CompilerParams(dimension_semantics=sem, vmem_limit_bytes=VMEM_LIMIT)


def _sigmoid(z):
    return 1.0 / (1.0 + jnp.exp(-z))


def _silu(z):
    return z * _sigmoid(z)


def _mm_kernel(x_ref, w_ref, o_ref):
    o_ref[...] = jnp.dot(x_ref[...], w_ref[0], preferred_element_type=F32)


def matmul(x, w, layer):
    m, k = x.shape
    n = w.shape[2]
    tm = min(m, 1024)
    tn = 1024 if n % 1024 == 0 else 512
    return pl.pallas_call(
        _mm_kernel,
        grid=(m // tm, n // tn),
        in_specs=[pl.BlockSpec((tm, k), lambda i, j: (i, 0)),
                  pl.BlockSpec((1, k, tn), lambda i, j: (layer, 0, j))],
        out_specs=pl.BlockSpec((tm, tn), lambda i, j: (i, j)),
        out_shape=jax.ShapeDtypeStruct((m, n), F32),
        compiler_params=_cparams(("parallel", "parallel")),
        name="matmul",
    )(x, w)


def _outproj_kernel(n_u, *refs):
    u_refs = refs[:n_u]
    w_refs = refs[n_u:2 * n_u]
    x_ref, g_ref, b_ref, o_ref, obf_ref = refs[2 * n_u:]
    acc = jnp.dot(u_refs[0][...], w_refs[0][0], preferred_element_type=F32)
    for u_ref, w_ref in zip(u_refs[1:], w_refs[1:]):
        acc = acc + jnp.dot(u_ref[...], w_ref[0], preferred_element_type=F32)
    z = ALPHA * x_ref[...] + acc
    mu = jnp.mean(z, axis=-1, keepdims=True)
    zc = z - mu
    var = jnp.mean(zc * zc, axis=-1, keepdims=True)
    y = zc * lax.rsqrt(var + LN_EPS) * g_ref[...] + b_ref[...]
    o_ref[...] = y
    obf_ref[...] = y.astype(BF16)


def outproj_ln(us, w_out, layer, x, g, b):
    m, d = x.shape
    tm = min(m, 512)
    n_u = len(us)
    in_specs = [pl.BlockSpec((tm, u.shape[1]), lambda i: (i, 0)) for u in us]
    row = 0
    for u in us:
        kw = u.shape[1]
        assert row % kw == 0
        in_specs.append(pl.BlockSpec((1, kw, d), functools.partial(lambda i, r: (layer, r, 0), r=row // kw)))
        row += kw
    assert row == w_out.shape[1]
    in_specs += [pl.BlockSpec((tm, d), lambda i: (i, 0)),
                 pl.BlockSpec((1, d), lambda i: (0, 0)),
                 pl.BlockSpec((1, d), lambda i: (0, 0))]
    return pl.pallas_call(
        functools.partial(_outproj_kernel, n_u),
        grid=(m // tm,),
        in_specs=in_specs,
        out_specs=[pl.BlockSpec((tm, d), lambda i: (i, 0)), pl.BlockSpec((tm, d), lambda i: (i, 0))],
        out_shape=[jax.ShapeDtypeStruct((m, d), F32), jax.ShapeDtypeStruct((m, d), BF16)],
        compiler_params=_cparams(("parallel",)),
        name="outproj_ln",
    )(*us, *([w_out] * n_u), x, g.reshape(1, d), b.reshape(1, d))


def _seg_sum(x, ones_bf):
    hi = x.astype(BF16)
    r1 = x - hi.astype(F32)
    mid = r1.astype(BF16)
    lo = (r1 - mid.astype(F32)).astype(BF16)
    outs = []
    for p in range(x.shape[1] // LANES):
        sl = slice(p * LANES, (p + 1) * LANES)
        terms = jnp.concatenate([hi[:, sl], mid[:, sl], lo[:, sl]], axis=1)
        outs.append(jnp.dot(terms, jnp.concatenate([ones_bf] * 3, axis=0), preferred_element_type=F32))
    return jnp.concatenate(outs, axis=1)


def _block_ones(width, seg):
    i = np.arange(width) // seg
    return jnp.asarray((i[:, None] == i[None, :]).astype(np.float32), dtype=BF16)


def _rwkv_prep_kernel(tt, tp, h_ref, prev_ref, sh0_ref, mu_ref, w0_ref, wup_ref, a0_ref, aup_ref,
                      kk_ref_p, ka_ref_p, rk_ref_p, ones_ref,
                      r_o, d_o, k_o, kk_o, b_o, vt_o, bonus_o, *scratch):
    i = pl.program_id(1)
    x = h_ref[0]
    pr = prev_ref[0]
    prev_last = pr[pr.shape[0] - 1:pr.shape[0], :]
    first = jnp.where(i == 0, sh0_ref[0], prev_last)
    row = lax.broadcasted_iota(jnp.int32, x.shape, 0)
    if tt % SUBLANES == 0:
        rolled = pltpu.roll(x, 1, axis=0)
    else:
        rolled = jnp.concatenate([x[tt - 1:tt], x[:tt - 1]], axis=0)
    prev = jnp.where(row == 0, first, rolled)
    hs = x + (prev - x) * mu_ref[...]
    r = hs[:, 0:A_WIDTH]
    k = hs[:, A_WIDTH:2 * A_WIDTH]
    v = hs[:, 2 * A_WIDTH:3 * A_WIDTH]
    hw = hs[:, 3 * A_WIDTH:3 * A_WIDTH + A_LORA]
    ha = hs[:, 3 * A_WIDTH + A_LORA:A_SHIFT_W]
    zw = w0_ref[...] + jnp.dot(jnp.tanh(hw).astype(BF16), wup_ref[...], preferred_element_type=F32)
    nz = -zw
    softplus = jnp.maximum(nz, 0.0) + jnp.log(1.0 + jnp.exp(-jnp.abs(nz)))
    w_log = -softplus - 0.5
    neg_log_decay = jnp.exp(w_log)
    a = _sigmoid(a0_ref[...] + jnp.dot(ha.astype(BF16), aup_ref[...], preferred_element_type=F32))
    ones_bf = ones_ref[...]
    kk = k * kk_ref_p[...]
    kk = kk * lax.rsqrt(jnp.maximum(_seg_sum(kk * kk, ones_bf), 1e-24))
    k2 = k * (1.0 + (a - 1.0) * ka_ref_p[...])
    bb = kk * a
    bonus_o[0] = _seg_sum(r * k2 * rk_ref_p[...], ones_bf) * v
    width = r_o.shape[3]
    for s in range(A_WIDTH // width):
        sl = slice(s * width, (s + 1) * width)
        r_o[0, s] = r[:, sl]
        d_o[0, s] = neg_log_decay[:, sl]
        k_o[0, s] = k2[:, sl]
        kk_o[0, s] = kk[:, sl]
        b_o[0, s] = bb[:, sl]
    for p in range(A_HEADS // 2):
        vp = v[:, p * LANES:(p + 1) * LANES]
        if tt == tp:
            vt_o[0, p] = vp.T
        else:
            pad = scratch[0]
            pad[...] = jnp.zeros_like(pad)
            pad[0:tt, :] = vp
            vt_o[0, p] = pad[...].T


def rwkv_prep(h3, shift0, prm, head_ones):
    bsz, t, _ = h3.shape
    tt = min(t, 128)
    tp = max(tt, LANES)
    nt = t // tt
    pr_rows = min(t, SUBLANES)
    pb = tt // pr_rows
    mu, w0, wup, a0, aup, k_k, k_a, r_k = prm
    row_spec = pl.BlockSpec((1, A_WIDTH), lambda b, i: (0, 0))
    lora_spec = pl.BlockSpec((A_LORA, A_WIDTH), lambda b, i: (0, 0))
    width = LANES if t % A_CHUNK == 0 else A_HEAD_DIM
    head_spec = pl.BlockSpec((1, A_WIDTH // width, tt, width), lambda b, i: (b, 0, i, 0))
    head_shape = jax.ShapeDtypeStruct((bsz, A_WIDTH // width, t, width), F32)
    scratch = [] if tt == tp else [pltpu.VMEM((tp, LANES), F32)]
    return pl.pallas_call(
        functools.partial(_rwkv_prep_kernel, tt, tp),
        grid=(bsz, nt),
        in_specs=[pl.BlockSpec((1, tt, A_SHIFT_W), lambda b, i: (b, i, 0)),
                  pl.BlockSpec((1, pr_rows, A_SHIFT_W), lambda b, i: (b, jnp.maximum(i * pb - 1, 0), 0)),
                  pl.BlockSpec((1, 1, A_SHIFT_W), lambda b, i: (b, 0, 0)),
                  pl.BlockSpec((1, A_SHIFT_W), lambda b, i: (0, 0)),
                  row_spec, lora_spec, row_spec, lora_spec, row_spec, row_spec, row_spec,
                  pl.BlockSpec((LANES, LANES), lambda b, i: (0, 0))],
        out_specs=[head_spec] * 5 + [
            pl.BlockSpec((1, A_HEADS // 2, LANES, tp), lambda b, i: (b, 0, 0, i)),
            pl.BlockSpec((1, tt, A_WIDTH), lambda b, i: (b, i, 0))],
        out_shape=[head_shape] * 5 + [
            jax.ShapeDtypeStruct((bsz, A_HEADS // 2, LANES, nt * tp), F32),
            jax.ShapeDtypeStruct((bsz, t, A_WIDTH), F32)],
        scratch_shapes=scratch,
        compiler_params=_cparams(("parallel", "parallel")),
        name="rwkv_prep",
    )(h3, h3, shift0.reshape(bsz, 1, A_SHIFT_W), mu.reshape(1, -1), w0.reshape(1, -1), wup.astype(BF16),
      a0.reshape(1, -1), aup.astype(BF16), k_k.reshape(1, -1), k_a.reshape(1, -1), r_k.reshape(1, -1), head_ones)


def _rwkv_scan_kernel(steps, r_ref, d_ref, k_ref, kk_ref, b_ref, vt_ref, s0_ref, yt_ref, sout_ref, s_scr, p_scr):
    c = pl.program_id(1)

    @pl.when(c == 0)
    def _():
        s_scr[...] = s0_ref[0]

    tp = vt_ref.shape[3]
    lane = lax.broadcasted_iota(jnp.int32, (A_HEAD_DIM, tp), 1)
    yt_ref[...] = jnp.zeros_like(yt_ref)
    p_scr[...] = jnp.zeros_like(p_scr)
    ones_bf = jnp.ones((A_HEAD_DIM, tp), BF16)

    def write_y(t):
        msk = lane == t
        for hd in range(A_HEADS):
            y = jnp.dot(p_scr[hd], ones_bf, preferred_element_type=F32)
            yt_ref[0, hd] = jnp.where(msk, y, yt_ref[0, hd])

    def body(t, carry):
        write_y(t - 1)
        msk = lane == t
        sas, vcols = [], []
        for hd in range(A_HEADS):
            sas.append(-jnp.sum(s_scr[hd] * kk_ref[0, hd, pl.ds(t, 1), :], axis=-1, keepdims=True))
            vcols.append(jnp.sum(jnp.where(msk, vt_ref[0, hd], 0.0), axis=-1, keepdims=True))
        for hd in range(A_HEADS):
            s = (s_scr[hd] * jnp.exp(-d_ref[0, hd, pl.ds(t, 1), :]) + sas[hd] * b_ref[0, hd, pl.ds(t, 1), :]
                 + vcols[hd] * k_ref[0, hd, pl.ds(t, 1), :])
            s_scr[hd] = s
            p_scr[hd] = (s * r_ref[0, hd, pl.ds(t, 1), :]).astype(BF16)
        return carry

    lax.fori_loop(0, steps, body, 0, unroll=4)
    write_y(steps - 1)

    @pl.when(c == pl.num_programs(1) - 1)
    def _():
        sout_ref[0] = s_scr[...]


A_CHUNK = 128
A_CHUNK_HEADS = 6
_NT = (((1,), (1,)), ((), ()))


def _split2(x):
    hi = x.astype(BF16)
    return hi, (x - hi.astype(F32)).astype(BF16)


def _mm3(a, b):
    ah, al = _split2(a)
    bh, bl = _split2(b)
    return jnp.dot(jnp.concatenate([ah, ah, al], axis=1), jnp.concatenate([bh, bl, bh], axis=0),
                   preferred_element_type=F32)


def _mm3_nt(a, b):
    ah, al = _split2(a)
    bh, bl = _split2(b)
    return lax.dot_general(jnp.concatenate([ah, ah, al], axis=1), jnp.concatenate([bh, bl, bh], axis=1), _NT,
                           preferred_element_type=F32)


def _inv_unit_lower(lbs, row, col):
    def blk(s):
        return (row // s) == (col // s)
    eye = jnp.where(row == col, 1.0, 0.0)
    ds = [jnp.where(blk(8), lb, 0.0) for lb in lbs]
    d2s = [_mm3(d, d) for d in ds]
    d4s = [_mm3(d2, d2) for d2 in d2s]
    xs = [eye + d for d in ds]
    xs = [x + _mm3(x, d2) for x, d2 in zip(xs, d2s)]
    xs = [x + _mm3(x, d4) for x, d4 in zip(xs, d4s)]
    s = 8
    while s < A_CHUNK:
        msk = blk(2 * s) & jnp.logical_not(blk(s))
        ts = [_mm3(x, jnp.where(msk, lb, 0.0)) for x, lb in zip(xs, lbs)]
        xs = [x + _mm3(t, x) for x, t in zip(xs, ts)]
        s *= 2
    return xs


def _rwkv_chunk_heads(es, rs, kks, bs, k2s, vs, sps, tril, row, col):
    n_tok, n = A_CHUNK, A_HEAD_DIM
    n_pair = len(es)
    idx = range(2 * n_pair)
    cs = []
    for e in es:
        e_hi = e.astype(BF16)
        e_r1 = e - e_hi.astype(F32)
        e_mid = e_r1.astype(BF16)
        e_lo = (e_r1 - e_mid.astype(F32)).astype(BF16)
        cs.append(jnp.dot(jnp.concatenate([tril, tril, tril], axis=1),
                          jnp.concatenate([e_hi, e_mid, e_lo], axis=0), preferred_element_type=F32))
    ms = [c[n_tok // 2 - 1:n_tok // 2] for c in cs]
    ccs = [c - m for c, m in zip(cs, ms)]
    gbs = [jnp.exp(cc) for cc in ccs]
    own = [lax.broadcasted_iota(jnp.int32, (n_tok, LANES), 1) < n]
    own.append(jnp.logical_not(own[0]))
    own_s = [lax.broadcasted_iota(jnp.int32, (n, LANES), 1) < n]
    own_s.append(jnp.logical_not(own_s[0]))

    def halves(xs, masks):
        return [jnp.where(masks[i % 2], xs[i // 2], 0.0) for i in idx]

    ats = halves([-kks[p] * jnp.exp(es[p] - ccs[p]) for p in range(n_pair)], own)
    bts = halves([bs[p] * gbs[p] for p in range(n_pair)], own)
    kts = halves([k2s[p] * gbs[p] for p in range(n_pair)], own)
    rts = halves([rs[p] * jnp.exp(-ccs[p]) for p in range(n_pair)], own)
    s0ps = halves([sps[p] * jnp.exp(-ms[p]) for p in range(n_pair)], own_s)
    bks = [jnp.concatenate([bts[i], kts[i]], axis=0) for i in idx]
    zs = [_mm3_nt(ats[i], bks[i]) for i in idx]
    m2s = [_mm3_nt(bks[i], rts[i]) for i in idx]
    low = row > col
    upp = row <= col
    lbs = [jnp.where(low, z[:, :n_tok], 0.0) for z in zs]
    lks = [jnp.where(low, z[:, n_tok:], 0.0) for z in zs]
    mbks = [jnp.concatenate([jnp.where(upp, m2[:n_tok], 0.0), jnp.where(upp, m2[n_tok:], 0.0)], axis=0)
            for m2 in m2s]
    ws = _inv_unit_lower(lbs, row, col)
    ptqs = [_mm3(ws[i], jnp.concatenate([ats[i], lks[i]], axis=1)) for i in idx]
    srps = [_mm3_nt(s0ps[i], jnp.concatenate([rts[i], ptqs[i][:, :LANES]], axis=0)) for i in idx]
    us = [srps[i][:, n_tok:] + _mm3_nt(vs[i], ptqs[i][:, LANES:]) for i in idx]
    yss = [_mm3(jnp.concatenate([us[i], vs[i]], axis=1), jnp.concatenate([mbks[i], bks[i]], axis=1)) for i in idx]
    ys = [srps[i][:, :n_tok] + yss[i][:, :n_tok] for i in idx]
    s1s = [(s0ps[2 * p] + yss[2 * p][:, n_tok:] + s0ps[2 * p + 1] + yss[2 * p + 1][:, n_tok:])
           * jnp.exp(-ccs[p][n_tok - 1:n_tok]) for p in range(n_pair)]
    return ys, s1s


def _rwkv_chunk_kernel(r_ref, e_ref, k_ref, kk_ref, b_ref, vt_ref, s0_ref, yt_ref, sout_ref, s_scr):
    c = pl.program_id(1)
    n_pair = A_HEADS // 2

    @pl.when(c == 0)
    def _():
        for p in range(n_pair):
            s_scr[p] = jnp.concatenate([s0_ref[0, 2 * p], s0_ref[0, 2 * p + 1]], axis=1)

    row = lax.broadcasted_iota(jnp.int32, (A_CHUNK, A_CHUNK), 0)
    col = lax.broadcasted_iota(jnp.int32, (A_CHUNK, A_CHUNK), 1)
    tril = jnp.where(row >= col, 1.0, 0.0).astype(BF16)
    pairs_per_group = A_CHUNK_HEADS // 2

    def group(gi, carry):
        prs = [gi * pairs_per_group + j for j in range(pairs_per_group)]
        hds = [2 * p + k for p in prs for k in range(2)]
        ys, s1s = _rwkv_chunk_heads([e_ref[0, p] for p in prs], [r_ref[0, p] for p in prs],
                                    [kk_ref[0, p] for p in prs], [b_ref[0, p] for p in prs],
                                    [k_ref[0, p] for p in prs], [vt_ref[0, hd] for hd in hds],
                                    [s_scr[p] for p in prs], tril, row, col)
        for hd, y in zip(hds, ys):
            yt_ref[0, hd] = y
        for p, s1 in zip(prs, s1s):
            s_scr[p] = s1
        return carry

    lax.fori_loop(0, n_pair // pairs_per_group, group, 0)

    @pl.when(c == pl.num_programs(1) - 1)
    def _():
        for p in range(n_pair):
            sp = s_scr[p]
            sout_ref[0, 2 * p] = sp[:, :A_HEAD_DIM]
            sout_ref[0, 2 * p + 1] = sp[:, A_HEAD_DIM:]


def rwkv_scan(ops, vt, s0):
    r, d, k, kk, b = ops
    bsz, n_slab, t, width = r.shape
    tt = min(t, 128)
    tp = max(tt, LANES)
    nt = t // tt
    vt = vt.reshape(bsz, A_HEADS, A_HEAD_DIM, nt * tp)
    chunked = t % A_CHUNK == 0
    assert width == (LANES if chunked else A_HEAD_DIM)
    head_spec = pl.BlockSpec((1, n_slab, tt, width), lambda b_, i: (b_, 0, i, 0))
    vt_spec = pl.BlockSpec((1, A_HEADS, A_HEAD_DIM, tp), lambda b_, i: (b_, 0, 0, i))
    st_spec = pl.BlockSpec((1, A_HEADS, A_HEAD_DIM, A_HEAD_DIM), lambda b_, i: (b_, 0, 0, 0))
    if chunked:
        scratch = [pltpu.VMEM((A_HEADS // 2, A_HEAD_DIM, LANES), F32)]
    else:
        scratch = [pltpu.VMEM((A_HEADS, A_HEAD_DIM, A_HEAD_DIM), F32),
                   pltpu.VMEM((A_HEADS, A_HEAD_DIM, A_HEAD_DIM), BF16)]
    yt, s_new = pl.pallas_call(
        _rwkv_chunk_kernel if chunked else functools.partial(_rwkv_scan_kernel, tt),
        grid=(bsz, nt),
        in_specs=[head_spec] * 5 + [vt_spec, st_spec],
        out_specs=[vt_spec, st_spec],
        out_shape=[jax.ShapeDtypeStruct(vt.shape, F32), jax.ShapeDtypeStruct(s0.shape, F32)],
        scratch_shapes=scratch,
        compiler_params=_cparams(("parallel", "arbitrary")),
        name="rwkv_chunk" if chunked else "rwkv_scan",
    )(r, d, k, kk, b, vt, s0)
    return yt.reshape(bsz, A_HEADS // 2, LANES, nt * tp), s_new


def _dwa_prompt_kernel(dil, *refs):
    npair = B_OUT // LANES
    in_refs = [refs[5 * hp:5 * hp + 5] for hp in range(npair)]
    out_refs = refs[5 * npair:]
    n = pl.program_id(1)
    i = lax.broadcasted_iota(jnp.int32, (B_BLK, B_BLK), 0)
    j = lax.broadcasted_iota(jnp.int32, (B_BLK, B_BLK), 1)
    mask_prev = j >= i + jnp.where(n > 0, 0, B_BLK)
    mask_cur = j <= i
    neg = -jnp.inf
    scale = B_HEAD_DIM ** -0.5
    dn = (((1,), (1,)), ((), ()))

    n_res = min(dil, B_RES_PER_ITER)
    n_hh = LANES // B_HEAD_DIM

    def residues(it, carry):
        rows = [pl.ds(it * n_res + k, B_BLK, stride=dil) if dil > 1 else pl.ds(0, B_BLK) for k in range(n_res)]
        units = [(k, hp, hh) for k in range(n_res) for hp in range(npair) for hh in range(n_hh)]
        loaded = {(k, hp): [ref[0, rows[k], :] for ref in in_refs[hp]] for k in range(n_res) for hp in range(npair)}

        def part(u, which):
            k, hp, hh = u
            return loaded[(k, hp)][which][:, hh * B_HEAD_DIM:(hh + 1) * B_HEAD_DIM].astype(BF16)

        qs = [part(u, 0) for u in units]
        sps = [jnp.where(mask_prev, lax.dot_general(q, part(u, 2), dn, preferred_element_type=F32) * scale, neg)
               for q, u in zip(qs, units)]
        scs = [jnp.where(mask_cur, lax.dot_general(q, part(u, 1), dn, preferred_element_type=F32) * scale, neg)
               for q, u in zip(qs, units)]
        ms = [jnp.maximum(jnp.max(sp, axis=-1, keepdims=True), jnp.max(sc, axis=-1, keepdims=True))
              for sp, sc in zip(sps, scs)]
        eps = [jnp.exp(sp - m) for sp, m in zip(sps, ms)]
        ecs = [jnp.exp(sc - m) for sc, m in zip(scs, ms)]
        ls = [jnp.sum(ep, axis=-1, keepdims=True) + jnp.sum(ec, axis=-1, keepdims=True) for ep, ec in zip(eps, ecs)]
        os_ = [jnp.dot((ep / l).astype(BF16), part(u, 4), preferred_element_type=F32)
               + jnp.dot((ec / l).astype(BF16), part(u, 3), preferred_element_type=F32)
               for ep, ec, l, u in zip(eps, ecs, ls, units)]
        lses = [jnp.broadcast_to(m + jnp.log(l), (B_BLK, B_HEAD_DIM)) for m, l in zip(ms, ls)]
        for k in range(n_res):
            for hp in range(npair):
                sel = [i_u for i_u, u in enumerate(units) if u[0] == k and u[1] == hp]
                out_refs[2 * hp][0, rows[k], :] = jnp.concatenate([os_[i_u] for i_u in sel], axis=-1)
                out_refs[2 * hp + 1][0, rows[k], :] = jnp.concatenate([lses[i_u] for i_u in sel], axis=-1)
        return carry

    lax.fori_loop(0, dil // n_res, residues, 0)


def dwa_prompt(h3, g):
    bsz, t, _ = h3.shape
    win, dil = B_CONFIGS[g]
    assert win == B_BLK * dil and t % win == 0
    npair = B_OUT // LANES

    def in_spec(unit, prev, hp):
        def imap(b, n):
            nn = jnp.maximum(n - 1, 0) if prev else n
            return (b, nn, unit + npair * g + hp)
        return pl.BlockSpec((1, win, LANES), imap)

    in_specs = []
    for hp in range(npair):
        in_specs += [in_spec(U_QB, False, hp), in_spec(U_KB, False, hp), in_spec(U_KB, True, hp),
                     in_spec(U_VB, False, hp), in_spec(U_VB, True, hp)]
    out_spec = pl.BlockSpec((1, win, LANES), lambda b, n: (b, n, 0))
    out_shape = jax.ShapeDtypeStruct((bsz, t, LANES), F32)
    outs = pl.pallas_call(
        functools.partial(_dwa_prompt_kernel, dil),
        grid=(bsz, t // win),
        in_specs=in_specs,
        out_specs=[out_spec] * (2 * npair),
        out_shape=[out_shape] * (2 * npair),
        compiler_params=_cparams(("parallel", "parallel")),
        name="dwa_prompt",
    )(*([h3] * len(in_specs)))
    return list(outs)


def _dwa_decode_kernel(t_new, h_ref, c0_ref, c1_ref, c2_ref, *out_refs):
    scale = B_HEAD_DIM ** -0.5
    dn = (((1,), (1,)), ((), ()))
    head_of_lane = lax.broadcasted_iota(jnp.int32, (SUBLANES, B_OUT), 1) // B_HEAD_DIM
    sub = lax.broadcasted_iota(jnp.int32, (SUBLANES, B_OUT), 0)
    own = head_of_lane == sub
    m_idx = lax.broadcasted_iota(jnp.int32, (SUBLANES, B_BLK), 1)
    neg = -jnp.inf
    hrow = h_ref[0]
    for g, c_ref in enumerate((c0_ref, c1_ref, c2_ref)):
        _, dil = B_CONFIGS[g]
        g_refs = out_refs[4 * g:4 * g + 4]
        qs = hrow[:, U_QB * LANES + g * B_OUT:U_QB * LANES + (g + 1) * B_OUT]
        ks = hrow[:, U_KB * LANES + g * B_OUT:U_KB * LANES + (g + 1) * B_OUT]
        vs = hrow[:, U_VB * LANES + g * B_OUT:U_VB * LANES + (g + 1) * B_OUT]
        for t in range(t_new):
            res = t % dil
            kbuf = jnp.concatenate([c_ref[0, 0, :, res, 0, hd, :] for hd in range(B_HEADS_PER_GROUP)], axis=-1)
            vbuf = jnp.concatenate([c_ref[0, 0, :, res, 1, hd, :] for hd in range(B_HEADS_PER_GROUP)], axis=-1)
            qbd = jnp.where(own, jnp.broadcast_to(qs[t:t + 1], (SUBLANES, B_OUT)), 0.0)
            s_buf = lax.dot_general(qbd.astype(BF16), kbuf.astype(BF16), dn, preferred_element_type=F32) * scale
            s_buf = jnp.where(m_idx * dil + res >= t, s_buf, neg)
            new_rows = [n for n in range(t + 1) if (t - n) % dil == 0]
            kq = qbd.astype(BF16).astype(F32)
            s_new = [jnp.sum(kq * ks[n:n + 1].astype(BF16).astype(F32), axis=-1, keepdims=True) * scale
                     for n in new_rows]
            m = jnp.max(s_buf, axis=-1, keepdims=True)
            for sn in s_new:
                m = jnp.maximum(m, sn)
            e_buf = jnp.exp(s_buf - m)
            e_new = [jnp.exp(sn - m) for sn in s_new]
            l = jnp.sum(e_buf, axis=-1, keepdims=True)
            for en in e_new:
                l = l + en
            o = jnp.dot((e_buf / l).astype(BF16), vbuf.astype(BF16), preferred_element_type=F32)
            for n, en in zip(new_rows, e_new):
                o = o + (en / l).astype(BF16).astype(F32) * vs[n:n + 1].astype(BF16).astype(F32)
            lse = jnp.broadcast_to(m + jnp.log(l), (SUBLANES, B_OUT))
            o_row = jnp.sum(jnp.where(own, o, 0.0), axis=0, keepdims=True)
            lse_row = jnp.sum(jnp.where(own, lse, 0.0), axis=0, keepdims=True)
            for hp in range(B_OUT // LANES):
                g_refs[2 * hp][0, t:t + 1, :] = o_row[:, hp * LANES:(hp + 1) * LANES]
                g_refs[2 * hp + 1][0, t:t + 1, :] = lse_row[:, hp * LANES:(hp + 1) * LANES]


def dwa_decode(h3, caches, e):
    bsz, t, _ = h3.shape
    views, specs = [], []
    for g, (win, dil) in enumerate(B_CONFIGS):
        assert caches[g].shape[2] == win and (dil == 1 or t <= dil)
        used = min(dil, t)
        view = caches[g].reshape(caches[g].shape[0], bsz, win // dil, dil, 2, B_HEADS_PER_GROUP, B_HEAD_DIM)
        views.append(view[:, :, :, :used])
        specs.append(pl.BlockSpec((1, 1, B_BLK, used, 2, B_HEADS_PER_GROUP, B_HEAD_DIM),
                                  lambda b: (e, b, 0, 0, 0, 0, 0)))
    out_spec = pl.BlockSpec((1, t, LANES), lambda b: (b, 0, 0))
    out_shape = jax.ShapeDtypeStruct((bsz, t, LANES), F32)
    n_out = 4 * B_GROUPS
    outs = pl.pallas_call(
        functools.partial(_dwa_decode_kernel, t),
        grid=(bsz,),
        in_specs=[pl.BlockSpec((1, t, NU * LANES), lambda b: (b, 0, 0))] + specs,
        out_specs=[out_spec] * n_out,
        out_shape=[out_shape] * n_out,
        compiler_params=_cparams(("parallel",)),
        name="dwa_decode",
    )(h3, *views)
    return [list(outs[4 * g:4 * g + 4]) for g in range(B_GROUPS)]


def _even_post_kernel(tt, yt_ref, bonus_ref, *refs):
    n_ga = A_WIDTH // LANES
    ga_refs = refs[:n_ga]
    gb0_ref, gb1_ref, lnxg_ref, lnxb_ref, ones_ref = refs[n_ga:n_ga + 5]
    dwa_refs = refs[n_ga + 5:n_ga + 5 + 4 * B_GROUPS]
    ua_ref, ub_ref = refs[n_ga + 5 + 4 * B_GROUPS:]
    ys = []
    for p in range(A_HEADS // 2):
        ys.append(yt_ref[0, p].T[0:tt, :])
    y = jnp.concatenate(ys, axis=-1)
    ones_bf = ones_ref[...]
    inv = 1.0 / A_HEAD_DIM
    mu = _seg_sum(y, ones_bf) * inv
    yc = y - mu
    var = _seg_sum(yc * yc, ones_bf) * inv
    ya = yc * lax.rsqrt(var + A_GN_EPS) * lnxg_ref[...] + lnxb_ref[...] + bonus_ref[0]
    gate_a = jnp.concatenate([g_ref[0] for g_ref in ga_refs], axis=-1)
    ua_ref[0] = (ya * _silu(gate_a)).astype(BF16)
    ybs = []
    for hp, gb_ref in enumerate((gb0_ref, gb1_ref)):
        o0, l0, o1, l1, o2, l2 = (dwa_refs[4 * g + 2 * hp + k][0] for g in range(B_GROUPS) for k in range(2))
        m = jnp.maximum(jnp.maximum(l0, l1), l2)
        e0, e1, e2 = jnp.exp(l0 - m), jnp.exp(l1 - m), jnp.exp(l2 - m)
        den = e0 + e1 + e2
        yb = (e0 / den) * o0 + (e1 / den) * o1 + (e2 / den) * o2
        ybs.append(yb * _silu(gb_ref[0]))
    ub_ref[0] = jnp.concatenate(ybs, axis=-1).astype(BF16)


def even_post(yt, bonus, h3, lnx_g, lnx_b, head_ones, dwa):
    bsz, t, _ = h3.shape
    tt = min(t, 128)
    tp = max(tt, LANES)
    nt = t // tt
    dwa_flat = [a for grp in dwa for a in grp]
    bspec = pl.BlockSpec((1, tt, B_OUT), lambda b, i: (b, i, 0))
    pspec = pl.BlockSpec((1, tt, LANES), lambda b, i: (b, i, 0))
    row_spec = pl.BlockSpec((1, A_WIDTH), lambda b, i: (0, 0))
    return pl.pallas_call(
        functools.partial(_even_post_kernel, tt),
        grid=(bsz, nt),
        in_specs=[pl.BlockSpec((1, A_HEADS // 2, LANES, tp), lambda b, i: (b, 0, 0, i)),
                  pl.BlockSpec((1, tt, A_WIDTH), lambda b, i: (b, i, 0)),
                  *[pl.BlockSpec((1, tt, LANES), functools.partial(lambda b, i, u: (b, i, u), u=U_GATE_A + u))
                    for u in range(A_WIDTH // LANES)],
                  pl.BlockSpec((1, tt, LANES), lambda b, i: (b, i, U_GATE_B)),
                  pl.BlockSpec((1, tt, LANES), lambda b, i: (b, i, U_GATE_B + 1)),
                  row_spec, row_spec,
                  pl.BlockSpec((LANES, LANES), lambda b, i: (0, 0))] + [pspec] * len(dwa_flat),
        out_specs=[pl.BlockSpec((1, tt, A_WIDTH), lambda b, i: (b, i, 0)), bspec],
        out_shape=[jax.ShapeDtypeStruct((bsz, t, A_WIDTH), BF16), jax.ShapeDtypeStruct((bsz, t, B_OUT), BF16)],
        compiler_params=_cparams(("parallel", "parallel")),
        name="even_post",
    )(yt, bonus, *([h3] * (A_WIDTH // LANES + 2)), lnx_g.reshape(1, -1), lnx_b.reshape(1, -1), head_ones, *dwa_flat)


def _mem_attn_kernel(*refs):
    q_refs, g_refs = refs[:M_HEADS], refs[M_HEADS:2 * M_HEADS]
    kv_refs, u_ref = refs[2 * M_HEADS:-1], refs[-1]
    dn = (((1,), (1,)), ((), ()))
    hds = range(M_HEADS)
    if len(kv_refs) == 1:
        ks = [kv_refs[0][0, 0, :, 0, hd, :] for hd in hds]
        vs = [kv_refs[0][0, 0, :, 1, hd, :] for hd in hds]
    else:
        ks = [kv_refs[0][0][:, hd * M_HEAD_DIM:(hd + 1) * M_HEAD_DIM] for hd in hds]
        vs = [kv_refs[1][0][:, hd * M_HEAD_DIM:(hd + 1) * M_HEAD_DIM] for hd in hds]
    ss = [lax.dot_general(q_refs[hd][0].astype(BF16), ks[hd].astype(BF16), dn,
                          preferred_element_type=F32) * (M_HEAD_DIM ** -0.5) for hd in hds]
    ms = [jnp.max(s, axis=-1, keepdims=True) for s in ss]
    es = [jnp.exp(s - m) for s, m in zip(ss, ms)]
    ps = [e / jnp.sum(e, axis=-1, keepdims=True) for e in es]
    os_ = [jnp.dot(ps[hd].astype(BF16), vs[hd].astype(BF16), preferred_element_type=F32) for hd in hds]
    u_ref[0] = jnp.concatenate([os_[hd] * _silu(g_refs[hd][0]) for hd in hds], axis=-1).astype(BF16)


def mem_attn(h3, mkv, layer, u_q, u_g):
    bsz, t, _ = h3.shape
    tq = min(t, 256)

    def col_spec(u):
        return pl.BlockSpec((1, tq, LANES), lambda b, i: (b, i, u))

    if layer is None:
        kv_specs = [pl.BlockSpec((1, M_TOKENS, M_WIDTH), lambda b, i: (b, 0, 0)),
                    pl.BlockSpec((1, M_TOKENS, M_WIDTH), lambda b, i: (b, 0, 1))]
    else:
        kv_specs = [pl.BlockSpec((1, 1, M_TOKENS, 2, M_HEADS, M_HEAD_DIM), lambda b, i: (layer, b, 0, 0, 0, 0))]
    return pl.pallas_call(
        _mem_attn_kernel,
        grid=(bsz, t // tq),
        in_specs=[col_spec(u_q + hd) for hd in range(M_HEADS)] + [col_spec(u_g + hd) for hd in range(M_HEADS)]
        + kv_specs,
        out_specs=pl.BlockSpec((1, tq, M_WIDTH), lambda b, i: (b, i, 0)),
        out_shape=jax.ShapeDtypeStruct((bsz, t, M_WIDTH), BF16),
        compiler_params=_cparams(("parallel", "parallel")),
        name="mem_attn",
    )(*([h3] * (2 * M_HEADS)), *([mkv] * len(kv_specs)))


def _rope_kernel(pos0, ang_ref, cos_ref, sin_ref):
    rows = cos_ref.shape[0]
    base = pl.program_id(0) * rows
    pos = (lax.broadcasted_iota(jnp.int32, cos_ref.shape, 0) + base).astype(F32) + pos0
    ph = pos * ang_ref[...]
    cos_ref[...] = jnp.cos(ph)
    sin_ref[...] = jnp.sin(ph)


def rope_tables(rows, pos0):
    angle = 1.0 / (C_ROT_BASE ** jnp.linspace(0.0, 1.0, C_HEAD_DIM // 2, dtype=F32))
    ang = jnp.repeat(angle, 2).reshape(1, C_HEAD_DIM)
    tr = min(rows, 512)
    spec = pl.BlockSpec((tr, C_HEAD_DIM), lambda i: (i, 0))
    shape = jax.ShapeDtypeStruct((rows, C_HEAD_DIM), F32)
    return pl.pallas_call(
        functools.partial(_rope_kernel, float(pos0)),
        grid=(rows // tr,),
        in_specs=[pl.BlockSpec((1, C_HEAD_DIM), lambda i: (0, 0))],
        out_specs=[spec, spec],
        out_shape=[shape, shape],
        compiler_params=_cparams(("parallel",)),
        name="rope_tables",
    )(ang)


def _rot_pairs(z):
    even = lax.broadcasted_iota(jnp.int32, (z.shape[0], LANES), 1) % 2 == 0
    parts = []
    for blk in range(z.shape[1] // LANES):
        zb = z[:, blk * LANES:(blk + 1) * LANES]
        nxt = pltpu.roll(zb, LANES - 1, axis=1)
        prv = pltpu.roll(zb, 1, axis=1)
        parts.append(jnp.where(even, -nxt, prv))
    return jnp.concatenate(parts, axis=-1)


def _retention_kernel(tb, chunk, lg_ref, q_ref, k_ref, v_ref, g_ref, cos_ref, sin_ref, r0_ref,
                      u_ref, rout_ref, r_scr, *pad):
    c = pl.program_id(2)

    @pl.when(c == 0)
    def _():
        r_scr[...] = r0_ref[0]

    if tb == C_CHUNK:
        q2, k2, v2 = q_ref[0], k_ref[0], v_ref[0]
    else:
        vals = []
        for src, buf in zip((q_ref, k_ref, v_ref), pad):
            buf[...] = jnp.zeros_like(buf)
            buf[0:tb, :] = src[0]
            vals.append(buf[...])
        q2, k2, v2 = vals
    cos, sin = cos_ref[...], sin_ref[...]
    ii = lax.broadcasted_iota(jnp.int32, (C_CHUNK, C_CHUNK), 0)
    jj = lax.broadcasted_iota(jnp.int32, (C_CHUNK, C_CHUNK), 1)
    diff = (ii - jj).astype(F32)
    idx = lax.broadcasted_iota(jnp.int32, (C_CHUNK, 1), 0).astype(F32)
    dn = (((1,), (1,)), ((), ()))
    hh_r = range(C_HEADS_PER_STEP)
    sls = [slice(hh * C_HEAD_DIM, (hh + 1) * C_HEAD_DIM) for hh in hh_r]
    lgs = [lg_ref[pl.program_id(1) * C_HEADS_PER_STEP + hh] for hh in hh_r]
    qrs = [q2[:, sl] * cos + _rot_pairs(q2[:, sl]) * sin for sl in sls]
    krs = [(k2[:, sl] * cos + _rot_pairs(k2[:, sl]) * sin) * (C_HEAD_DIM ** -0.5) for sl in sls]
    qbs = [qr.astype(BF16) for qr in qrs]
    vbs = [v2[:, sl].astype(BF16) for sl in sls]
    scs = [lax.dot_general(qb, kr.astype(BF16), dn, preferred_element_type=F32)
           * jnp.where(diff >= 0, jnp.exp(lg * jnp.maximum(diff, 0.0)), 0.0) for qb, kr, lg in zip(qbs, krs, lgs)]
    r_olds = [r_scr[hh] for hh in hh_r]
    kz_ts = [(kr * jnp.exp(lg * (chunk - 1.0 - idx))).T.astype(BF16) for kr, lg in zip(krs, lgs)]
    os_ = [jnp.dot(sc.astype(BF16), vb, preferred_element_type=F32)
           + jnp.dot(qb, r_old.astype(BF16), preferred_element_type=F32) * jnp.exp(lg * (idx + 1.0))
           for sc, vb, qb, r_old, lg in zip(scs, vbs, qbs, r_olds, lgs)]
    for hh in hh_r:
        r_scr[hh] = (r_olds[hh] * jnp.exp(lgs[hh] * jnp.full((1, 1), chunk, F32))
                     + jnp.dot(kz_ts[hh], vbs[hh], preferred_element_type=F32))
    us = []
    for hh in hh_r:
        o = os_[hh][0:tb]
        y = o * lax.rsqrt(jnp.mean(o * o, axis=-1, keepdims=True) + C_NORM_EPS)
        us.append(y * _silu(g_ref[0][:, sls[hh]]))
    u_ref[0] = jnp.concatenate(us, axis=-1).astype(BF16)

    @pl.when(c == pl.num_programs(2) - 1)
    def _():
        rout_ref[0] = r_scr[...]


def retention(h3, cos, sin, r0):
    bsz, t, _ = h3.shape
    tb = min(t, C_CHUNK)
    nc = t // tb
    chunk = float(tb)
    lg = jnp.log(1.0 - 2.0 ** (-5.0 - jnp.arange(C_HEADS, dtype=F32)))
    hps = C_HEADS_PER_STEP
    width = hps * C_HEAD_DIM
    nq = C_WIDTH // width

    def col_spec(off):
        return pl.BlockSpec((1, tb, width), lambda b, hd, c: (b, c, off + hd))

    tab_spec = pl.BlockSpec((C_CHUNK, C_HEAD_DIM), lambda b, hd, c: (c, 0))
    st_spec = pl.BlockSpec((1, hps, C_HEAD_DIM, C_HEAD_DIM), lambda b, hd, c: (b, hd, 0, 0))
    scratch = [pltpu.VMEM((hps, C_HEAD_DIM, C_HEAD_DIM), F32)]
    if tb != C_CHUNK:
        scratch += [pltpu.VMEM((C_CHUNK, width), F32)] * 3
    return pl.pallas_call(
        functools.partial(_retention_kernel, tb, chunk),
        grid=(bsz, C_HEADS // hps, nc),
        in_specs=[pl.BlockSpec(memory_space=pltpu.SMEM),
                  col_spec(0), col_spec(nq), col_spec(2 * nq), col_spec(3 * nq),
                  tab_spec, tab_spec, st_spec],
        out_specs=[pl.BlockSpec((1, tb, width), lambda b, hd, c: (b, c, hd)), st_spec],
        out_shape=[jax.ShapeDtypeStruct((bsz, t, C_WIDTH), BF16), jax.ShapeDtypeStruct(r0.shape, F32)],
        scratch_shapes=scratch,
        compiler_params=_cparams(("parallel", "parallel", "arbitrary")),
        name="retention",
    )(lg, h3, h3, h3, h3, cos, sin, r0)


def _even_layer(x, x_bf, mkv, dwa_bufs, s0, shift0, w_in_bf, w_out_bf, e, ln_g, ln_b, rw, head_ones):
    bsz, t, d = x.shape
    h = matmul(x_bf.reshape(bsz * t, d), w_in_bf, e)
    h3 = h.reshape(bsz, t, NU * LANES)
    mu, w0, wup, a0, aup, k_k, k_a, r_k, lnx_g, lnx_b = rw
    r, dcy, k2, kk, bb, vt, bonus = rwkv_prep(h3, shift0, (mu, w0, wup, a0, aup, k_k, k_a, r_k), head_ones)
    yt, s_new = rwkv_scan((r, dcy, k2, kk, bb), vt, s0)
    if dwa_bufs is None:
        dwa = [dwa_prompt(h3, g) for g in range(B_GROUPS)]
    else:
        dwa = dwa_decode(h3, dwa_bufs, e)
    u_a, u_b = even_post(yt, bonus, h3, lnx_g, lnx_b, head_ones, dwa)
    u_m = mem_attn(h3, mkv[0], mkv[1], U_QM_EVEN, U_GM_EVEN)
    x2, x2_bf = outproj_ln([u_a.reshape(bsz * t, -1), u_b.reshape(bsz * t, -1), u_m.reshape(bsz * t, -1)],
                           w_out_bf, e, x.reshape(bsz * t, d), ln_g, ln_b)
    rows = []
    for g, (win, _) in enumerate(B_CONFIGS):
        keep = t if dwa_bufs is not None else min(win, t)
        kg = h3[:, t - keep:, U_KB * LANES + g * B_OUT:U_KB * LANES + (g + 1) * B_OUT]
        vg = h3[:, t - keep:, U_VB * LANES + g * B_OUT:U_VB * LANES + (g + 1) * B_OUT]
        rows.append(jnp.stack([kg.reshape(bsz, keep, B_HEADS_PER_GROUP, B_HEAD_DIM),
                               vg.reshape(bsz, keep, B_HEADS_PER_GROUP, B_HEAD_DIM)], axis=2))
    return x2.reshape(bsz, t, d), x2_bf.reshape(bsz, t, d), s_new, h3[:, t - 1, :A_SHIFT_W], rows


def _odd_layer(x, x_bf, mkv, r0, tabs, w_in_bf, w_out_bf, o, ln_g, ln_b):
    bsz, t, d = x.shape
    h = matmul(x_bf.reshape(bsz * t, d), w_in_bf, o)
    h3 = h.reshape(bsz, t, NU * LANES)
    u_c, r_new = retention(h3, tabs[0], tabs[1], r0)
    u_m = mem_attn(h3, mkv[0], mkv[1], U_QM_ODD, U_GM_ODD)
    x2, x2_bf = outproj_ln([u_c.reshape(bsz * t, -1), u_m.reshape(bsz * t, -1)],
                           w_out_bf, o, x.reshape(bsz * t, d), ln_g, ln_b)
    return x2.reshape(bsz, t, d), x2_bf.reshape(bsz, t, d), r_new


def kernel(x_prompt, x_sample, state_rwkv, state_rwkv_shift, cache_dwa_g0, cache_dwa_g1, cache_dwa_g2, state_ret, cache_mem_kv, mem_prompt, w_in_even, w_out_even, w_in_odd, w_out_odd, w_mem_kv, ln_g, ln_b, rwkv_mu, rwkv_w0, rwkv_w_up, rwkv_a0, rwkv_a_up, rwkv_k_k, rwkv_k_a, rwkv_r_k, rwkv_lnx_g, rwkv_lnx_b):
    xp, xs = x_prompt, x_sample
    xp_bf, xs_bf = xp.astype(BF16), xs.astype(BF16)
    bp, tp_len, d = xp.shape
    bs, ts_len, _ = xs.shape
    dwa_cache = (cache_dwa_g0, cache_dwa_g1, cache_dwa_g2)
    head_ones = _block_ones(LANES, A_HEAD_DIM)
    mem_bf = mem_prompt.reshape(bp * M_TOKENS, d).astype(BF16)
    tabs_p = rope_tables(max(tp_len, C_CHUNK), 0)
    tabs_s = rope_tables(max(ts_len, C_CHUNK), PAST_LEN)
    rwkv_p, rwkv_s, shift_p, shift_s, ret_p, ret_s, mem_p = [], [], [], [], [], [], []
    dwa_p = [[] for _ in B_CONFIGS]
    dwa_s = [[] for _ in B_CONFIGS]
    w_in_even_bf = jnp.concatenate(
        [w_in_even.astype(BF16), jnp.zeros(w_in_even.shape[:2] + (EVEN_IN_PAD - EVEN_IN,), BF16)], axis=-1)
    w_out_even_bf = w_out_even.astype(BF16)
    w_in_odd_bf = w_in_odd.astype(BF16)
    w_out_odd_bf = w_out_odd.astype(BF16)
    w_mem_bf = w_mem_kv.astype(BF16)
    for l in range(DEPTH):
        mkv_new = matmul(mem_bf, w_mem_bf, l).reshape(bp, M_TOKENS, 2 * M_WIDTH)
        mem_p.append(mkv_new.reshape(bp, M_TOKENS, 2, M_HEADS, M_HEAD_DIM))
        mkv_p = (mkv_new, None)
        mkv_s = (cache_mem_kv, l)
        if l % 2 == 0:
            e = l // 2
            w_in_bf, w_out_bf = w_in_even_bf, w_out_even_bf
            rw = (rwkv_mu[e], rwkv_w0[e], rwkv_w_up[e], rwkv_a0[e], rwkv_a_up[e], rwkv_k_k[e], rwkv_k_a[e],
                  rwkv_r_k[e], rwkv_lnx_g[e], rwkv_lnx_b[e])
            s0 = jnp.zeros((bp, A_HEADS, A_HEAD_DIM, A_HEAD_DIM), F32)
            sh0 = jnp.zeros((bp, A_SHIFT_W), F32)
            xp, xp_bf, st, sh, rows = _even_layer(xp, xp_bf, mkv_p, None, s0, sh0, w_in_bf, w_out_bf, e,
                                                  ln_g[l], ln_b[l], rw, head_ones)
            rwkv_p.append(st)
            shift_p.append(sh)
            for g in range(B_GROUPS):
                dwa_p[g].append(rows[g])
            bufs = dwa_cache
            xs, xs_bf, st, sh, rows = _even_layer(xs, xs_bf, mkv_s, bufs, state_rwkv[e], state_rwkv_shift[e],
                                                  w_in_bf, w_out_bf, e, ln_g[l], ln_b[l], rw, head_ones)
            rwkv_s.append(st)
            shift_s.append(sh)
            for g in range(B_GROUPS):
                dwa_s[g].append(rows[g])
        else:
            o = l // 2
            r0 = jnp.zeros((bp, C_HEADS, C_HEAD_DIM, C_HEAD_DIM), F32)
            xp, xp_bf, st = _odd_layer(xp, xp_bf, mkv_p, r0, tabs_p, w_in_odd_bf, w_out_odd_bf, o, ln_g[l], ln_b[l])
            ret_p.append(st)
            xs, xs_bf, st = _odd_layer(xs, xs_bf, mkv_s, state_ret[o], tabs_s, w_in_odd_bf, w_out_odd_bf, o,
                                       ln_g[l], ln_b[l])
            ret_s.append(st)
    return (xp, xs, jnp.stack(rwkv_p), jnp.stack(rwkv_s), jnp.stack(shift_p), jnp.stack(shift_s),
            jnp.stack(dwa_p[0]), jnp.stack(dwa_s[0]), jnp.stack(dwa_p[1]), jnp.stack(dwa_s[1]),
            jnp.stack(dwa_p[2]), jnp.stack(dwa_s[2]), jnp.stack(ret_p), jnp.stack(ret_s), jnp.stack(mem_p))
```

```python
import functools

import numpy as np
import jax
import jax.numpy as jnp
from jax import lax
from jax.experimental import pallas as pl
from jax.experimental.pallas import tpu as pltpu

F32 = jnp.float32
BF16 = jnp.bfloat16

D_MODEL = 2048
DEPTH = 4
PAST_LEN = 16384
ALPHA = (2 * DEPTH) ** 0.25
LN_EPS = 1e-5
A_HEADS = 12
A_HEAD_DIM = 64
A_WIDTH = A_HEADS * A_HEAD_DIM
A_LORA = 64
A_SHIFT_W = 3 * A_WIDTH + 2 * A_LORA
A_GN_EPS = 64e-5
B_CONFIGS = ((128, 1), (512, 4), (2048, 16))
B_GROUPS = 3
B_HEADS_PER_GROUP = 4
B_HEAD_DIM = 64
B_OUT = B_HEADS_PER_GROUP * B_HEAD_DIM
B_WIDTH = B_GROUPS * B_OUT
B_BLK = 128
B_RES_PER_ITER = 2
C_HEADS = 6
C_HEAD_DIM = 256
C_WIDTH = C_HEADS * C_HEAD_DIM
C_CHUNK = 128
C_HEADS_PER_STEP = 3
C_ROT_BASE = 10000.0
C_NORM_EPS = 1e-6
M_TOKENS = 256
M_HEADS = 4
M_HEAD_DIM = 128
M_WIDTH = M_HEADS * M_HEAD_DIM
EVEN_IN = A_SHIFT_W + A_WIDTH + 3 * B_WIDTH + B_OUT + 2 * M_WIDTH
ODD_IN = 4 * C_WIDTH + 2 * M_WIDTH

LANES = 128
SUBLANES = 8
VMEM_LIMIT = 48 * 1024 * 1024

EVEN_IN_PAD = 7168
U_GATE_A = A_SHIFT_W // LANES
U_QB = U_GATE_A + A_WIDTH // LANES
U_KB = U_QB + B_WIDTH // LANES
U_VB = U_KB + B_WIDTH // LANES
U_GATE_B = U_VB + B_WIDTH // LANES
U_QM_EVEN = U_GATE_B + B_OUT // LANES
U_GM_EVEN = U_QM_EVEN + M_WIDTH // LANES
U_QM_ODD = 4 * C_WIDTH // LANES
U_GM_ODD = U_QM_ODD + M_WIDTH // LANES
NU = EVEN_IN_PAD // LANES


def _cparams(sem):
    return pltpu.CompilerParams(dimension_semantics=sem, vmem_limit_bytes=VMEM_LIMIT)


def _sigmoid(z):
    return 1.0 / (1.0 + jnp.exp(-z))


def _silu(z):
    return z * _sigmoid(z)


def _mm_kernel(x_ref, w_ref, o_ref, w_scr):
    @pl.when(pl.program_id(1) == 0)
    def _():
        w_scr[...] = w_ref[0].astype(BF16)

    o_ref[...] = jnp.dot(x_ref[...], w_scr[...], preferred_element_type=F32)


def matmul(x, w, layer):
    m, k = x.shape
    n = w.shape[2]
    tm = min(m, 1024)
    tn = 1024 if n % 1024 == 0 else 512
    return pl.pallas_call(
        _mm_kernel,
        grid=(n // tn, m // tm),
        in_specs=[pl.BlockSpec((tm, k), lambda j, i: (i, 0)),
                  pl.BlockSpec((1, k, tn), lambda j, i: (layer, 0, j))],
        out_specs=pl.BlockSpec((tm, tn), lambda j, i: (i, j)),
        out_shape=jax.ShapeDtypeStruct((m, n), F32),
        scratch_shapes=[pltpu.VMEM((k, tn), BF16)],
        compiler_params=_cparams(("parallel", "arbitrary")),
        name="matmul",
    )(x, w)


def _outproj_kernel(n_u, *refs):
    u_refs = refs[:n_u]
    w_refs = refs[n_u:2 * n_u]
    x_ref, g_ref, b_ref, o_ref, obf_ref = refs[2 * n_u:]
    acc = jnp.dot(u_refs[0][...], w_refs[0][0], preferred_element_type=F32)
    for u_ref, w_ref in zip(u_refs[1:], w_refs[1:]):
        acc = acc + jnp.dot(u_ref[...], w_ref[0], preferred_element_type=F32)
    z = ALPHA * x_ref[...] + acc
    mu = jnp.mean(z, axis=-1, keepdims=True)
    zc = z - mu
    var = jnp.mean(zc * zc, axis=-1, keepdims=True)
    y = zc * lax.rsqrt(var + LN_EPS) * g_ref[...] + b_ref[...]
    o_ref[...] = y
    obf_ref[...] = y.astype(BF16)


def outproj_ln(us, w_out, layer, x, g, b):
    m, d = x.shape
    tm = min(m, 512)
    n_u = len(us)
    in_specs = [pl.BlockSpec((tm, u.shape[1]), lambda i: (i, 0)) for u in us]
    row = 0
    for u in us:
        kw = u.shape[1]
        assert row % kw == 0
        in_specs.append(pl.BlockSpec((1, kw, d), functools.partial(lambda i, r: (layer, r, 0), r=row // kw)))
        row += kw
    assert row == w_out.shape[1]
    in_specs += [pl.BlockSpec((tm, d), lambda i: (i, 0)),
                 pl.BlockSpec((1, d), lambda i: (0, 0)),
                 pl.BlockSpec((1, d), lambda i: (0, 0))]
    return pl.pallas_call(
        functools.partial(_outproj_kernel, n_u),
        grid=(m // tm,),
        in_specs=in_specs,
        out_specs=[pl.BlockSpec((tm, d), lambda i: (i, 0)), pl.BlockSpec((tm, d), lambda i: (i, 0))],
        out_shape=[jax.ShapeDtypeStruct((m, d), F32), jax.ShapeDtypeStruct((m, d), BF16)],
        compiler_params=_cparams(("parallel",)),
        name="outproj_ln",
    )(*us, *([w_out] * n_u), x, g.reshape(1, d), b.reshape(1, d))


def _seg_sum(x, ones_bf):
    hi = x.astype(BF16)
    r1 = x - hi.astype(F32)
    mid = r1.astype(BF16)
    lo = (r1 - mid.astype(F32)).astype(BF16)
    outs = []
    for p in range(x.shape[1] // LANES):
        sl = slice(p * LANES, (p + 1) * LANES)
        terms = jnp.concatenate([hi[:, sl], mid[:, sl], lo[:, sl]], axis=1)
        outs.append(jnp.dot(terms, jnp.concatenate([ones_bf] * 3, axis=0), preferred_element_type=F32))
    return jnp.concatenate(outs, axis=1)


def _block_ones(width, seg):
    i = np.arange(width) // seg
    return jnp.asarray((i[:, None] == i[None, :]).astype(np.float32), dtype=BF16)


def _rwkv_prep_kernel(tt, tp, h_ref, prev_ref, sh0_ref, mu_ref, w0_ref, wup_ref, a0_ref, aup_ref,
                      kk_ref_p, ka_ref_p, rk_ref_p, ones_ref,
                      r_o, d_o, k_o, kk_o, b_o, vt_o, bonus_o, *scratch):
    i = pl.program_id(1)
    x = h_ref[0]
    pr = prev_ref[0]
    prev_last = pr[pr.shape[0] - 1:pr.shape[0], :]
    first = jnp.where(i == 0, sh0_ref[0], prev_last)
    row = lax.broadcasted_iota(jnp.int32, x.shape, 0)
    if tt % SUBLANES == 0:
        rolled = pltpu.roll(x, 1, axis=0)
    else:
        rolled = jnp.concatenate([x[tt - 1:tt], x[:tt - 1]], axis=0)
    prev = jnp.where(row == 0, first, rolled)
    hs = x + (prev - x) * mu_ref[...]
    r = hs[:, 0:A_WIDTH]
    k = hs[:, A_WIDTH:2 * A_WIDTH]
    v = hs[:, 2 * A_WIDTH:3 * A_WIDTH]
    hw = hs[:, 3 * A_WIDTH:3 * A_WIDTH + A_LORA]
    ha = hs[:, 3 * A_WIDTH + A_LORA:A_SHIFT_W]
    zw = w0_ref[...] + jnp.dot(jnp.tanh(hw).astype(BF16), wup_ref[...], preferred_element_type=F32)
    nz = -zw
    softplus = jnp.maximum(nz, 0.0) + jnp.log(1.0 + jnp.exp(-jnp.abs(nz)))
    w_log = -softplus - 0.5
    neg_log_decay = jnp.exp(w_log)
    a = _sigmoid(a0_ref[...] + jnp.dot(ha.astype(BF16), aup_ref[...], preferred_element_type=F32))
    ones_bf = ones_ref[...]
    kk = k * kk_ref_p[...]
    kk = kk * lax.rsqrt(jnp.maximum(_seg_sum(kk * kk, ones_bf), 1e-24))
    k2 = k * (1.0 + (a - 1.0) * ka_ref_p[...])
    bb = kk * a
    bonus_o[0] = _seg_sum(r * k2 * rk_ref_p[...], ones_bf) * v
    width = r_o.shape[3]
    for s in range(A_WIDTH // width):
        sl = slice(s * width, (s + 1) * width)
        r_o[0, s] = r[:, sl]
        d_o[0, s] = neg_log_decay[:, sl]
        k_o[0, s] = k2[:, sl]
        kk_o[0, s] = kk[:, sl]
        b_o[0, s] = bb[:, sl]
    for p in range(A_HEADS // 2):
        vp = v[:, p * LANES:(p + 1) * LANES]
        if tt == tp:
            vt_o[0, p] = vp.T
        else:
            pad = scratch[0]
            pad[...] = jnp.zeros_like(pad)
            pad[0:tt, :] = vp
            vt_o[0, p] = pad[...].T


def rwkv_prep(h3, shift0, prm, head_ones):
    bsz, t, _ = h3.shape
    tt = min(t, 128)
    tp = max(tt, LANES)
    nt = t // tt
    pr_rows = min(t, SUBLANES)
    pb = tt // pr_rows
    mu, w0, wup, a0, aup, k_k, k_a, r_k = prm
    row_spec = pl.BlockSpec((1, A_WIDTH), lambda b, i: (0, 0))
    lora_spec = pl.BlockSpec((A_LORA, A_WIDTH), lambda b, i: (0, 0))
    width = LANES if t % A_CHUNK == 0 else A_HEAD_DIM
    head_spec = pl.BlockSpec((1, A_WIDTH // width, tt, width), lambda b, i: (b, 0, i, 0))
    head_shape = jax.ShapeDtypeStruct((bsz, A_WIDTH // width, t, width), F32)
    scratch = [] if tt == tp else [pltpu.VMEM((tp, LANES), F32)]
    return pl.pallas_call(
        functools.partial(_rwkv_prep_kernel, tt, tp),
        grid=(bsz, nt),
        in_specs=[pl.BlockSpec((1, tt, A_SHIFT_W), lambda b, i: (b, i, 0)),
                  pl.BlockSpec((1, pr_rows, A_SHIFT_W), lambda b, i: (b, jnp.maximum(i * pb - 1, 0), 0)),
                  pl.BlockSpec((1, 1, A_SHIFT_W), lambda b, i: (b, 0, 0)),
                  pl.BlockSpec((1, A_SHIFT_W), lambda b, i: (0, 0)),
                  row_spec, lora_spec, row_spec, lora_spec, row_spec, row_spec, row_spec,
                  pl.BlockSpec((LANES, LANES), lambda b, i: (0, 0))],
        out_specs=[head_spec] * 5 + [
            pl.BlockSpec((1, A_HEADS // 2, LANES, tp), lambda b, i: (b, 0, 0, i)),
            pl.BlockSpec((1, tt, A_WIDTH), lambda b, i: (b, i, 0))],
        out_shape=[head_shape] * 5 + [
            jax.ShapeDtypeStruct((bsz, A_HEADS // 2, LANES, nt * tp), F32),
            jax.ShapeDtypeStruct((bsz, t, A_WIDTH), F32)],
        scratch_shapes=scratch,
        compiler_params=_cparams(("parallel", "parallel")),
        name="rwkv_prep",
    )(h3, h3, shift0.reshape(bsz, 1, A_SHIFT_W), mu.reshape(1, -1), w0.reshape(1, -1), wup.astype(BF16),
      a0.reshape(1, -1), aup.astype(BF16), k_k.reshape(1, -1), k_a.reshape(1, -1), r_k.reshape(1, -1), head_ones)


def _rwkv_scan_kernel(steps, r_ref, d_ref, k_ref, kk_ref, b_ref, vt_ref, s0_ref, yt_ref, sout_ref, s_scr, p_scr):
    c = pl.program_id(1)

    @pl.when(c == 0)
    def _():
        s_scr[...] = s0_ref[0]

    tp = vt_ref.shape[3]
    lane = lax.broadcasted_iota(jnp.int32, (A_HEAD_DIM, tp), 1)
    yt_ref[...] = jnp.zeros_like(yt_ref)
    p_scr[...] = jnp.zeros_like(p_scr)
    ones_bf = jnp.ones((A_HEAD_DIM, tp), BF16)

    def write_y(t):
        msk = lane == t
        for hd in range(A_HEADS):
            y = jnp.dot(p_scr[hd], ones_bf, preferred_element_type=F32)
            yt_ref[0, hd] = jnp.where(msk, y, yt_ref[0, hd])

    def body(t, carry):
        write_y(t - 1)
        msk = lane == t
        sas, vcols = [], []
        for hd in range(A_HEADS):
            sas.append(-jnp.sum(s_scr[hd] * kk_ref[0, hd, pl.ds(t, 1), :], axis=-1, keepdims=True))
            vcols.append(jnp.sum(jnp.where(msk, vt_ref[0, hd], 0.0), axis=-1, keepdims=True))
        for hd in range(A_HEADS):
            s = (s_scr[hd] * jnp.exp(-d_ref[0, hd, pl.ds(t, 1), :]) + sas[hd] * b_ref[0, hd, pl.ds(t, 1), :]
                 + vcols[hd] * k_ref[0, hd, pl.ds(t, 1), :])
            s_scr[hd] = s
            p_scr[hd] = (s * r_ref[0, hd, pl.ds(t, 1), :]).astype(BF16)
        return carry

    lax.fori_loop(0, steps, body, 0, unroll=4)
    write_y(steps - 1)

    @pl.when(c == pl.num_programs(1) - 1)
    def _():
        sout_ref[0] = s_scr[...]


A_CHUNK = 128
A_CHUNK_HEADS = 6
_NT = (((1,), (1,)), ((), ()))


def _split2(x):
    hi = x.astype(BF16)
    return hi, (x - hi.astype(F32)).astype(BF16)


def _mm3(a, b):
    ah, al = _split2(a)
    bh, bl = _split2(b)
    return jnp.dot(jnp.concatenate([ah, ah, al], axis=1), jnp.concatenate([bh, bl, bh], axis=0),
                   preferred_element_type=F32)


def _mm3_nt(a, b):
    ah, al = _split2(a)
    bh, bl = _split2(b)
    return lax.dot_general(jnp.concatenate([ah, ah, al], axis=1), jnp.concatenate([bh, bl, bh], axis=1), _NT,
                           preferred_element_type=F32)


def _inv_unit_lower(lbs, row, col):
    def blk(s):
        return (row // s) == (col // s)
    eye = jnp.where(row == col, 1.0, 0.0)
    ds = [jnp.where(blk(8), lb, 0.0) for lb in lbs]
    d2s = [_mm3(d, d) for d in ds]
    d4s = [_mm3(d2, d2) for d2 in d2s]
    xs = [eye + d for d in ds]
    xs = [x + _mm3(x, d2) for x, d2 in zip(xs, d2s)]
    xs = [x + _mm3(x, d4) for x, d4 in zip(xs, d4s)]
    s = 8
    while s < A_CHUNK:
        msk = blk(2 * s) & jnp.logical_not(blk(s))
        ts = [_mm3(x, jnp.where(msk, lb, 0.0)) for x, lb in zip(xs, lbs)]
        xs = [x + _mm3(t, x) for x, t in zip(xs, ts)]
        s *= 2
    return xs


def _rwkv_chunk_heads(es, rs, kks, bs, k2s, vs, sps, tril, row, col):
    n_tok, n = A_CHUNK, A_HEAD_DIM
    n_pair = len(es)
    idx = range(2 * n_pair)
    cs = []
    for e in es:
        e_hi = e.astype(BF16)
        e_r1 = e - e_hi.astype(F32)
        e_mid = e_r1.astype(BF16)
        e_lo = (e_r1 - e_mid.astype(F32)).astype(BF16)
        cs.append(jnp.dot(jnp.concatenate([tril, tril, tril], axis=1),
                          jnp.concatenate([e_hi, e_mid, e_lo], axis=0), preferred_element_type=F32))
    ms = [c[n_tok // 2 - 1:n_tok // 2] for c in cs]
    ccs = [c - m for c, m in zip(cs, ms)]
    gbs = [jnp.exp(cc) for cc in ccs]
    own = [lax.broadcasted_iota(jnp.int32, (n_tok, LANES), 1) < n]
    own.append(jnp.logical_not(own[0]))
    own_s = [lax.broadcasted_iota(jnp.int32, (n, LANES), 1) < n]
    own_s.append(jnp.logical_not(own_s[0]))

    def halves(xs, masks):
        return [jnp.where(masks[i % 2], xs[i // 2], 0.0) for i in idx]

    ats = halves([-kks[p] * jnp.exp(es[p] - ccs[p]) for p in range(n_pair)], own)
    bts = halves([bs[p] * gbs[p] for p in range(n_pair)], own)
    kts = halves([k2s[p] * gbs[p] for p in range(n_pair)], own)
    rts = halves([rs[p] * jnp.exp(-ccs[p]) for p in range(n_pair)], own)
    s0ps = halves([sps[p] * jnp.exp(-ms[p]) for p in range(n_pair)], own_s)
    bks = [jnp.concatenate([bts[i], kts[i]], axis=0) for i in idx]
    zs = [_mm3_nt(ats[i], bks[i]) for i in idx]
    m2s = [_mm3_nt(bks[i], rts[i]) for i in idx]
    low = row > col
    upp = row <= col
    lbs = [jnp.where(low, z[:, :n_tok], 0.0) for z in zs]
    lks = [jnp.where(low, z[:, n_tok:], 0.0) for z in zs]
    mbks = [jnp.concatenate([jnp.where(upp, m2[:n_tok], 0.0), jnp.where(upp, m2[n_tok:], 0.0)], axis=0)
            for m2 in m2s]
    ws = _inv_unit_lower(lbs, row, col)
    ptqs = [_mm3(ws[i], jnp.concatenate([ats[i], lks[i]], axis=1)) for i in idx]
    srps = [_mm3_nt(s0ps[i], jnp.concatenate([rts[i], ptqs[i][:, :LANES]], axis=0)) for i in idx]
    us = [srps[i][:, n_tok:] + _mm3_nt(vs[i], ptqs[i][:, LANES:]) for i in idx]
    yss = [_mm3(jnp.concatenate([us[i], vs[i]], axis=1), jnp.concatenate([mbks[i], bks[i]], axis=1)) for i in idx]
    ys = [srps[i][:, :n_tok] + yss[i][:, :n_tok] for i in idx]
    s1s = [(s0ps[2 * p] + yss[2 * p][:, n_tok:] + s0ps[2 * p + 1] + yss[2 * p + 1][:, n_tok:])
           * jnp.exp(-ccs[p][n_tok - 1:n_tok]) for p in range(n_pair)]
    return ys, s1s


def _rwkv_chunk_kernel(r_ref, e_ref, k_ref, kk_ref, b_ref, vt_ref, s0_ref, yt_ref, sout_ref, s_scr):
    c = pl.program_id(1)
    n_pair = A_HEADS // 2

    @pl.when(c == 0)
    def _():
        for p in range(n_pair):
            s_scr[p] = jnp.concatenate([s0_ref[0, 2 * p], s0_ref[0, 2 * p + 1]], axis=1)

    row = lax.broadcasted_iota(jnp.int32, (A_CHUNK, A_CHUNK), 0)
    col = lax.broadcasted_iota(jnp.int32, (A_CHUNK, A_CHUNK), 1)
    tril = jnp.where(row >= col, 1.0, 0.0).astype(BF16)
    pairs_per_group = A_CHUNK_HEADS // 2

    def group(gi, carry):
        prs = [gi * pairs_per_group + j for j in range(pairs_per_group)]
        hds = [2 * p + k for p in prs for k in range(2)]
        ys, s1s = _rwkv_chunk_heads([e_ref[0, p] for p in prs], [r_ref[0, p] for p in prs],
                                    [kk_ref[0, p] for p in prs], [b_ref[0, p] for p in prs],
                                    [k_ref[0, p] for p in prs], [vt_ref[0, hd] for hd in hds],
                                    [s_scr[p] for p in prs], tril, row, col)
        for hd, y in zip(hds, ys):
            yt_ref[0, hd] = y
        for p, s1 in zip(prs, s1s):
            s_scr[p] = s1
        return carry

    lax.fori_loop(0, n_pair // pairs_per_group, group, 0)

    @pl.when(c == pl.num_programs(1) - 1)
    def _():
        for p in range(n_pair):
            sp = s_scr[p]
            sout_ref[0, 2 * p] = sp[:, :A_HEAD_DIM]
            sout_ref[0, 2 * p + 1] = sp[:, A_HEAD_DIM:]


def rwkv_scan(ops, vt, s0):
    r, d, k, kk, b = ops
    bsz, n_slab, t, width = r.shape
    tt = min(t, 128)
    tp = max(tt, LANES)
    nt = t // tt
    vt = vt.reshape(bsz, A_HEADS, A_HEAD_DIM, nt * tp)
    chunked = t % A_CHUNK == 0
    assert width == (LANES if chunked else A_HEAD_DIM)
    head_spec = pl.BlockSpec((1, n_slab, tt, width), lambda b_, i: (b_, 0, i, 0))
    vt_spec = pl.BlockSpec((1, A_HEADS, A_HEAD_DIM, tp), lambda b_, i: (b_, 0, 0, i))
    st_spec = pl.BlockSpec((1, A_HEADS, A_HEAD_DIM, A_HEAD_DIM), lambda b_, i: (b_, 0, 0, 0))
    if chunked:
        scratch = [pltpu.VMEM((A_HEADS // 2, A_HEAD_DIM, LANES), F32)]
    else:
        scratch = [pltpu.VMEM((A_HEADS, A_HEAD_DIM, A_HEAD_DIM), F32),
                   pltpu.VMEM((A_HEADS, A_HEAD_DIM, A_HEAD_DIM), BF16)]
    yt, s_new = pl.pallas_call(
        _rwkv_chunk_kernel if chunked else functools.partial(_rwkv_scan_kernel, tt),
        grid=(bsz, nt),
        in_specs=[head_spec] * 5 + [vt_spec, st_spec],
        out_specs=[vt_spec, st_spec],
        out_shape=[jax.ShapeDtypeStruct(vt.shape, F32), jax.ShapeDtypeStruct(s0.shape, F32)],
        scratch_shapes=scratch,
        compiler_params=_cparams(("parallel", "arbitrary")),
        name="rwkv_chunk" if chunked else "rwkv_scan",
    )(r, d, k, kk, b, vt, s0)
    return yt.reshape(bsz, A_HEADS // 2, LANES, nt * tp), s_new


def _dwa_prompt_kernel(dil, *refs):
    npair = B_OUT // LANES
    in_refs = [refs[5 * hp:5 * hp + 5] for hp in range(npair)]
    out_refs = refs[5 * npair:]
    n = pl.program_id(1)
    i = lax.broadcasted_iota(jnp.int32, (B_BLK, B_BLK), 0)
    j = lax.broadcasted_iota(jnp.int32, (B_BLK, B_BLK), 1)
    mask_prev = j >= i + jnp.where(n > 0, 0, B_BLK)
    mask_cur = j <= i
    neg = -jnp.inf
    scale = B_HEAD_DIM ** -0.5
    dn = (((1,), (1,)), ((), ()))

    n_res = min(dil, B_RES_PER_ITER)
    n_hh = LANES // B_HEAD_DIM

    def residues(it, carry):
        rows = [pl.ds(it * n_res + k, B_BLK, stride=dil) if dil > 1 else pl.ds(0, B_BLK) for k in range(n_res)]
        units = [(k, hp, hh) for k in range(n_res) for hp in range(npair) for hh in range(n_hh)]
        loaded = {(k, hp): [ref[0, rows[k], :] for ref in in_refs[hp]] for k in range(n_res) for hp in range(npair)}

        def part(u, which):
            k, hp, hh = u
            return loaded[(k, hp)][which][:, hh * B_HEAD_DIM:(hh + 1) * B_HEAD_DIM].astype(BF16)

        qs = [part(u, 0) for u in units]
        sps = [jnp.where(mask_prev, lax.dot_general(q, part(u, 2), dn, preferred_element_type=F32) * scale, neg)
               for q, u in zip(qs, units)]
        scs = [jnp.where(mask_cur, lax.dot_general(q, part(u, 1), dn, preferred_element_type=F32) * scale, neg)
               for q, u in zip(qs, units)]
        ms = [jnp.maximum(jnp.max(sp, axis=-1, keepdims=True), jnp.max(sc, axis=-1, keepdims=True))
              for sp, sc in zip(sps, scs)]
        eps = [jnp.exp(sp - m) for sp, m in zip(sps, ms)]
        ecs = [jnp.exp(sc - m) for sc, m in zip(scs, ms)]
        ls = [jnp.sum(ep, axis=-1, keepdims=True) + jnp.sum(ec, axis=-1, keepdims=True) for ep, ec in zip(eps, ecs)]
        os_ = [jnp.dot((ep / l).astype(BF16), part(u, 4), preferred_element_type=F32)
               + jnp.dot((ec / l).astype(BF16), part(u, 3), preferred_element_type=F32)
               for ep, ec, l, u in zip(eps, ecs, ls, units)]
        lses = [jnp.broadcast_to(m + jnp.log(l), (B_BLK, B_HEAD_DIM)) for m, l in zip(ms, ls)]
        for k in range(n_res):
            for hp in range(npair):
                sel = [i_u for i_u, u in enumerate(units) if u[0] == k and u[1] == hp]
                out_refs[2 * hp][0, rows[k], :] = jnp.concatenate([os_[i_u] for i_u in sel], axis=-1)
                out_refs[2 * hp + 1][0, rows[k], :] = jnp.concatenate([lses[i_u] for i_u in sel], axis=-1)
        return carry

    lax.fori_loop(0, dil // n_res, residues, 0)


def dwa_prompt(h3, g):
    bsz, t, _ = h3.shape
    win, dil = B_CONFIGS[g]
    assert win == B_BLK * dil and t % win == 0
    npair = B_OUT // LANES

    def in_spec(unit, prev, hp):
        def imap(b, n):
            nn = jnp.maximum(n - 1, 0) if prev else n
            return (b, nn, unit + npair * g + hp)
        return pl.BlockSpec((1, win, LANES), imap)

    in_specs = []
    for hp in range(npair):
        in_specs += [in_spec(U_QB, False, hp), in_spec(U_KB, False, hp), in_spec(U_KB, True, hp),
                     in_spec(U_VB, False, hp), in_spec(U_VB, True, hp)]
    out_spec = pl.BlockSpec((1, win, LANES), lambda b, n: (b, n, 0))
    out_shape = jax.ShapeDtypeStruct((bsz, t, LANES), F32)
    outs = pl.pallas_call(
        functools.partial(_dwa_prompt_kernel, dil),
        grid=(bsz, t // win),
        in_specs=in_specs,
        out_specs=[out_spec] * (2 * npair),
        out_shape=[out_shape] * (2 * npair),
        compiler_params=_cparams(("parallel", "parallel")),
        name="dwa_prompt",
    )(*([h3] * len(in_specs)))
    return list(outs)


def _dwa_decode_kernel(t_new, h_ref, c0_ref, c1_ref, c2_ref, *out_refs):
    scale = B_HEAD_DIM ** -0.5
    dn = (((1,), (1,)), ((), ()))
    head_of_lane = lax.broadcasted_iota(jnp.int32, (SUBLANES, B_OUT), 1) // B_HEAD_DIM
    sub = lax.broadcasted_iota(jnp.int32, (SUBLANES, B_OUT), 0)
    own = head_of_lane == sub
    m_idx = lax.broadcasted_iota(jnp.int32, (SUBLANES, B_BLK), 1)
    neg = -jnp.inf
    hrow = h_ref[0]
    for g, c_ref in enumerate((c0_ref, c1_ref, c2_ref)):
        _, dil = B_CONFIGS[g]
        g_refs = out_refs[4 * g:4 * g + 4]
        qs = hrow[:, U_QB * LANES + g * B_OUT:U_QB * LANES + (g + 1) * B_OUT]
        ks = hrow[:, U_KB * LANES + g * B_OUT:U_KB * LANES + (g + 1) * B_OUT]
        vs = hrow[:, U_VB * LANES + g * B_OUT:U_VB * LANES + (g + 1) * B_OUT]
        for t in range(t_new):
            res = t % dil
            kbuf = jnp.concatenate([c_ref[0, 0, :, res, 0, hd, :] for hd in range(B_HEADS_PER_GROUP)], axis=-1)
            vbuf = jnp.concatenate([c_ref[0, 0, :, res, 1, hd, :] for hd in range(B_HEADS_PER_GROUP)], axis=-1)
            qbd = jnp.where(own, jnp.broadcast_to(qs[t:t + 1], (SUBLANES, B_OUT)), 0.0)
            s_buf = lax.dot_general(qbd.astype(BF16), kbuf.astype(BF16), dn, preferred_element_type=F32) * scale
            s_buf = jnp.where(m_idx * dil + res >= t, s_buf, neg)
            new_rows = [n for n in range(t + 1) if (t - n) % dil == 0]
            kq = qbd.astype(BF16).astype(F32)
            s_new = [jnp.sum(kq * ks[n:n + 1].astype(BF16).astype(F32), axis=-1, keepdims=True) * scale
                     for n in new_rows]
            m = jnp.max(s_buf, axis=-1, keepdims=True)
            for sn in s_new:
                m = jnp.maximum(m, sn)
            e_buf = jnp.exp(s_buf - m)
            e_new = [jnp.exp(sn - m) for sn in s_new]
            l = jnp.sum(e_buf, axis=-1, keepdims=True)
            for en in e_new:
                l = l + en
            o = jnp.dot((e_buf / l).astype(BF16), vbuf.astype(BF16), preferred_element_type=F32)
            for n, en in zip(new_rows, e_new):
                o = o + (en / l).astype(BF16).astype(F32) * vs[n:n + 1].astype(BF16).astype(F32)
            lse = jnp.broadcast_to(m + jnp.log(l), (SUBLANES, B_OUT))
            o_row = jnp.sum(jnp.where(own, o, 0.0), axis=0, keepdims=True)
            lse_row = jnp.sum(jnp.where(own, lse, 0.0), axis=0, keepdims=True)
            for hp in range(B_OUT // LANES):
                g_refs[2 * hp][0, t:t + 1, :] = o_row[:, hp * LANES:(hp + 1) * LANES]
                g_refs[2 * hp + 1][0, t:t + 1, :] = lse_row[:, hp * LANES:(hp + 1) * LANES]


def dwa_decode(h3, caches, e):
    bsz, t, _ = h3.shape
    views, specs = [], []
    for g, (win, dil) in enumerate(B_CONFIGS):
        assert caches[g].shape[2] == win and (dil == 1 or t <= dil)
        views.append(caches[g].reshape(caches[g].shape[0], bsz, win // dil, dil, 2, B_HEADS_PER_GROUP, B_HEAD_DIM))
        specs.append(pl.BlockSpec((1, 1, B_BLK, min(dil, t), 2, B_HEADS_PER_GROUP, B_HEAD_DIM),
                                  lambda b: (e, b, 0, 0, 0, 0, 0)))
    out_spec = pl.BlockSpec((1, t, LANES), lambda b: (b, 0, 0))
    out_shape = jax.ShapeDtypeStruct((bsz, t, LANES), F32)
    n_out = 4 * B_GROUPS
    outs = pl.pallas_call(
        functools.partial(_dwa_decode_kernel, t),
        grid=(bsz,),
        in_specs=[pl.BlockSpec((1, t, NU * LANES), lambda b: (b, 0, 0))] + specs,
        out_specs=[out_spec] * n_out,
        out_shape=[out_shape] * n_out,
        compiler_params=_cparams(("parallel",)),
        name="dwa_decode",
    )(h3, *views)
    return [list(outs[4 * g:4 * g + 4]) for g in range(B_GROUPS)]


def _even_post_kernel(tt, yt_ref, bonus_ref, *refs):
    n_ga = A_WIDTH // LANES
    ga_refs = refs[:n_ga]
    gb0_ref, gb1_ref, lnxg_ref, lnxb_ref, ones_ref = refs[n_ga:n_ga + 5]
    dwa_refs = refs[n_ga + 5:n_ga + 5 + 4 * B_GROUPS]
    ua_ref, ub_ref = refs[n_ga + 5 + 4 * B_GROUPS:]
    ys = []
    for p in range(A_HEADS // 2):
        ys.append(yt_ref[0, p].T[0:tt, :])
    y = jnp.concatenate(ys, axis=-1)
    ones_bf = ones_ref[...]
    inv = 1.0 / A_HEAD_DIM
    mu = _seg_sum(y, ones_bf) * inv
    yc = y - mu
    var = _seg_sum(yc * yc, ones_bf) * inv
    ya = yc * lax.rsqrt(var + A_GN_EPS) * lnxg_ref[...] + lnxb_ref[...] + bonus_ref[0]
    gate_a = jnp.concatenate([g_ref[0] for g_ref in ga_refs], axis=-1)
    ua_ref[0] = (ya * _silu(gate_a)).astype(BF16)
    ybs = []
    for hp, gb_ref in enumerate((gb0_ref, gb1_ref)):
        o0, l0, o1, l1, o2, l2 = (dwa_refs[4 * g + 2 * hp + k][0] for g in range(B_GROUPS) for k in range(2))
        m = jnp.maximum(jnp.maximum(l0, l1), l2)
        e0, e1, e2 = jnp.exp(l0 - m), jnp.exp(l1 - m), jnp.exp(l2 - m)
        den = e0 + e1 + e2
        yb = (e0 / den) * o0 + (e1 / den) * o1 + (e2 / den) * o2
        ybs.append(yb * _silu(gb_ref[0]))
    ub_ref[0] = jnp.concatenate(ybs, axis=-1).astype(BF16)


def even_post(yt, bonus, h3, lnx_g, lnx_b, head_ones, dwa):
    bsz, t, _ = h3.shape
    tt = min(t, 128)
    tp = max(tt, LANES)
    nt = t // tt
    dwa_flat = [a for grp in dwa for a in grp]
    bspec = pl.BlockSpec((1, tt, B_OUT), lambda b, i: (b, i, 0))
    pspec = pl.BlockSpec((1, tt, LANES), lambda b, i: (b, i, 0))
    row_spec = pl.BlockSpec((1, A_WIDTH), lambda b, i: (0, 0))
    return pl.pallas_call(
        functools.partial(_even_post_kernel, tt),
        grid=(bsz, nt),
        in_specs=[pl.BlockSpec((1, A_HEADS // 2, LANES, tp), lambda b, i: (b, 0, 0, i)),
                  pl.BlockSpec((1, tt, A_WIDTH), lambda b, i: (b, i, 0)),
                  *[pl.BlockSpec((1, tt, LANES), functools.partial(lambda b, i, u: (b, i, u), u=U_GATE_A + u))
                    for u in range(A_WIDTH // LANES)],
                  pl.BlockSpec((1, tt, LANES), lambda b, i: (b, i, U_GATE_B)),
                  pl.BlockSpec((1, tt, LANES), lambda b, i: (b, i, U_GATE_B + 1)),
                  row_spec, row_spec,
                  pl.BlockSpec((LANES, LANES), lambda b, i: (0, 0))] + [pspec] * len(dwa_flat),
        out_specs=[pl.BlockSpec((1, tt, A_WIDTH), lambda b, i: (b, i, 0)), bspec],
        out_shape=[jax.ShapeDtypeStruct((bsz, t, A_WIDTH), BF16), jax.ShapeDtypeStruct((bsz, t, B_OUT), BF16)],
        compiler_params=_cparams(("parallel", "parallel")),
        name="even_post",
    )(yt, bonus, *([h3] * (A_WIDTH // LANES + 2)), lnx_g.reshape(1, -1), lnx_b.reshape(1, -1), head_ones, *dwa_flat)


def _mem_attn_kernel(*refs):
    q_refs, g_refs = refs[:M_HEADS], refs[M_HEADS:2 * M_HEADS]
    kv_refs, u_ref = refs[2 * M_HEADS:-1], refs[-1]
    dn = (((1,), (1,)), ((), ()))
    hds = range(M_HEADS)
    if len(kv_refs) == 1:
        ks = [kv_refs[0][0, 0, :, 0, hd, :] for hd in hds]
        vs = [kv_refs[0][0, 0, :, 1, hd, :] for hd in hds]
    else:
        ks = [kv_refs[0][0][:, hd * M_HEAD_DIM:(hd + 1) * M_HEAD_DIM] for hd in hds]
        vs = [kv_refs[1][0][:, hd * M_HEAD_DIM:(hd + 1) * M_HEAD_DIM] for hd in hds]
    ss = [lax.dot_general(q_refs[hd][0].astype(BF16), ks[hd].astype(BF16), dn,
                          preferred_element_type=F32) * (M_HEAD_DIM ** -0.5) for hd in hds]
    ms = [jnp.max(s, axis=-1, keepdims=True) for s in ss]
    es = [jnp.exp(s - m) for s, m in zip(ss, ms)]
    ps = [e / jnp.sum(e, axis=-1, keepdims=True) for e in es]
    os_ = [jnp.dot(ps[hd].astype(BF16), vs[hd].astype(BF16), preferred_element_type=F32) for hd in hds]
    u_ref[0] = jnp.concatenate([os_[hd] * _silu(g_refs[hd][0]) for hd in hds], axis=-1).astype(BF16)


def mem_attn(h3, mkv, layer, u_q, u_g):
    bsz, t, _ = h3.shape
    tq = min(t, 256)

    def col_spec(u):
        return pl.BlockSpec((1, tq, LANES), lambda b, i: (b, i, u))

    if layer is None:
        kv_specs = [pl.BlockSpec((1, M_TOKENS, M_WIDTH), lambda b, i: (b, 0, 0)),
                    pl.BlockSpec((1, M_TOKENS, M_WIDTH), lambda b, i: (b, 0, 1))]
    else:
        kv_specs = [pl.BlockSpec((1, 1, M_TOKENS, 2, M_HEADS, M_HEAD_DIM), lambda b, i: (layer, b, 0, 0, 0, 0))]
    return pl.pallas_call(
        _mem_attn_kernel,
        grid=(bsz, t // tq),
        in_specs=[col_spec(u_q + hd) for hd in range(M_HEADS)] + [col_spec(u_g + hd) for hd in range(M_HEADS)]
        + kv_specs,
        out_specs=pl.BlockSpec((1, tq, M_WIDTH), lambda b, i: (b, i, 0)),
        out_shape=jax.ShapeDtypeStruct((bsz, t, M_WIDTH), BF16),
        compiler_params=_cparams(("parallel", "parallel")),
        name="mem_attn",
    )(*([h3] * (2 * M_HEADS)), *([mkv] * len(kv_specs)))


def _rope_kernel(pos0, ang_ref, cos_ref, sin_ref):
    rows = cos_ref.shape[0]
    base = pl.program_id(0) * rows
    pos = (lax.broadcasted_iota(jnp.int32, cos_ref.shape, 0) + base).astype(F32) + pos0
    ph = pos * ang_ref[...]
    cos_ref[...] = jnp.cos(ph)
    sin_ref[...] = jnp.sin(ph)


def rope_tables(rows, pos0):
    angle = 1.0 / (C_ROT_BASE ** jnp.linspace(0.0, 1.0, C_HEAD_DIM // 2, dtype=F32))
    ang = jnp.repeat(angle, 2).reshape(1, C_HEAD_DIM)
    tr = min(rows, 512)
    spec = pl.BlockSpec((tr, C_HEAD_DIM), lambda i: (i, 0))
    shape = jax.ShapeDtypeStruct((rows, C_HEAD_DIM), F32)
    return pl.pallas_call(
        functools.partial(_rope_kernel, float(pos0)),
        grid=(rows // tr,),
        in_specs=[pl.BlockSpec((1, C_HEAD_DIM), lambda i: (0, 0))],
        out_specs=[spec, spec],
        out_shape=[shape, shape],
        compiler_params=_cparams(("parallel",)),
        name="rope_tables",
    )(ang)


def _rot_pairs(z):
    even = lax.broadcasted_iota(jnp.int32, (z.shape[0], LANES), 1) % 2 == 0
    parts = []
    for blk in range(z.shape[1] // LANES):
        zb = z[:, blk * LANES:(blk + 1) * LANES]
        nxt = pltpu.roll(zb, LANES - 1, axis=1)
        prv = pltpu.roll(zb, 1, axis=1)
        parts.append(jnp.where(even, -nxt, prv))
    return jnp.concatenate(parts, axis=-1)


def _retention_kernel(tb, chunk, lg_ref, q_ref, k_ref, v_ref, g_ref, cos_ref, sin_ref, r0_ref,
                      u_ref, rout_ref, r_scr, *pad):
    c = pl.program_id(2)

    @pl.when(c == 0)
    def _():
        r_scr[...] = r0_ref[0]

    if tb == C_CHUNK:
        q2, k2, v2 = q_ref[0], k_ref[0], v_ref[0]
    else:
        vals = []
        for src, buf in zip((q_ref, k_ref, v_ref), pad):
            buf[...] = jnp.zeros_like(buf)
            buf[0:tb, :] = src[0]
            vals.append(buf[...])
        q2, k2, v2 = vals
    cos, sin = cos_ref[...], sin_ref[...]
    ii = lax.broadcasted_iota(jnp.int32, (C_CHUNK, C_CHUNK), 0)
    jj = lax.broadcasted_iota(jnp.int32, (C_CHUNK, C_CHUNK), 1)
    diff = (ii - jj).astype(F32)
    idx = lax.broadcasted_iota(jnp.int32, (C_CHUNK, 1), 0).astype(F32)
    dn = (((1,), (1,)), ((), ()))
    hh_r = range(C_HEADS_PER_STEP)
    sls = [slice(hh * C_HEAD_DIM, (hh + 1) * C_HEAD_DIM) for hh in hh_r]
    lgs = [lg_ref[pl.program_id(1) * C_HEADS_PER_STEP + hh] for hh in hh_r]
    qrs = [q2[:, sl] * cos + _rot_pairs(q2[:, sl]) * sin for sl in sls]
    krs = [(k2[:, sl] * cos + _rot_pairs(k2[:, sl]) * sin) * (C_HEAD_DIM ** -0.5) for sl in sls]
    qbs = [qr.astype(BF16) for qr in qrs]
    vbs = [v2[:, sl].astype(BF16) for sl in sls]
    scs = [lax.dot_general(qb, kr.astype(BF16), dn, preferred_element_type=F32)
           * jnp.where(diff >= 0, jnp.exp(lg * jnp.maximum(diff, 0.0)), 0.0) for qb, kr, lg in zip(qbs, krs, lgs)]
    r_olds = [r_scr[hh] for hh in hh_r]
    kz_ts = [(kr * jnp.exp(lg * (chunk - 1.0 - idx))).T.astype(BF16) for kr, lg in zip(krs, lgs)]
    os_ = [jnp.dot(sc.astype(BF16), vb, preferred_element_type=F32)
           + jnp.dot(qb, r_old.astype(BF16), preferred_element_type=F32) * jnp.exp(lg * (idx + 1.0))
           for sc, vb, qb, r_old, lg in zip(scs, vbs, qbs, r_olds, lgs)]
    for hh in hh_r:
        r_scr[hh] = (r_olds[hh] * jnp.exp(lgs[hh] * jnp.full((1, 1), chunk, F32))
                     + jnp.dot(kz_ts[hh], vbs[hh], preferred_element_type=F32))
    us = []
    for hh in hh_r:
        o = os_[hh][0:tb]
        y = o * lax.rsqrt(jnp.mean(o * o, axis=-1, keepdims=True) + C_NORM_EPS)
        us.append(y * _silu(g_ref[0][:, sls[hh]]))
    u_ref[0] = jnp.concatenate(us, axis=-1).astype(BF16)

    @pl.when(c == pl.num_programs(2) - 1)
    def _():
        rout_ref[0] = r_scr[...]


def retention(h3, cos, sin, r0):
    bsz, t, _ = h3.shape
    tb = min(t, C_CHUNK)
    nc = t // tb
    chunk = float(tb)
    lg = jnp.log(1.0 - 2.0 ** (-5.0 - jnp.arange(C_HEADS, dtype=F32)))
    hps = C_HEADS_PER_STEP
    width = hps * C_HEAD_DIM
    nq = C_WIDTH // width

    def col_spec(off):
        return pl.BlockSpec((1, tb, width), lambda b, hd, c: (b, c, off + hd))

    tab_spec = pl.BlockSpec((C_CHUNK, C_HEAD_DIM), lambda b, hd, c: (c, 0))
    st_spec = pl.BlockSpec((1, hps, C_HEAD_DIM, C_HEAD_DIM), lambda b, hd, c: (b, hd, 0, 0))
    scratch = [pltpu.VMEM((hps, C_HEAD_DIM, C_HEAD_DIM), F32)]
    if tb != C_CHUNK:
        scratch += [pltpu.VMEM((C_CHUNK, width), F32)] * 3
    return pl.pallas_call(
        functools.partial(_retention_kernel, tb, chunk),
        grid=(bsz, C_HEADS // hps, nc),
        in_specs=[pl.BlockSpec(memory_space=pltpu.SMEM),
                  col_spec(0), col_spec(nq), col_spec(2 * nq), col_spec(3 * nq),
                  tab_spec, tab_spec, st_spec],
        out_specs=[pl.BlockSpec((1, tb, width), lambda b, hd, c: (b, c, hd)), st_spec],
        out_shape=[jax.ShapeDtypeStruct((bsz, t, C_WIDTH), BF16), jax.ShapeDtypeStruct(r0.shape, F32)],
        scratch_shapes=scratch,
        compiler_params=_cparams(("parallel", "parallel", "arbitrary")),
        name="retention",
    )(lg, h3, h3, h3, h3, cos, sin, r0)


def _even_layer(x, x_bf, mkv, dwa_bufs, s0, shift0, w_in_bf, w_out_bf, e, ln_g, ln_b, rw, head_ones):
    bsz, t, d = x.shape
    h = matmul(x_bf.reshape(bsz * t, d), w_in_bf, e)
    h3 = h.reshape(bsz, t, NU * LANES)
    mu, w0, wup, a0, aup, k_k, k_a, r_k, lnx_g, lnx_b = rw
    r, dcy, k2, kk, bb, vt, bonus = rwkv_prep(h3, shift0, (mu, w0, wup, a0, aup, k_k, k_a, r_k), head_ones)
    yt, s_new = rwkv_scan((r, dcy, k2, kk, bb), vt, s0)
    if dwa_bufs is None:
        dwa = [dwa_prompt(h3, g) for g in range(B_GROUPS)]
    else:
        dwa = dwa_decode(h3, dwa_bufs, e)
    u_a, u_b = even_post(yt, bonus, h3, lnx_g, lnx_b, head_ones, dwa)
    u_m = mem_attn(h3, mkv[0], mkv[1], U_QM_EVEN, U_GM_EVEN)
    x2, x2_bf = outproj_ln([u_a.reshape(bsz * t, -1), u_b.reshape(bsz * t, -1), u_m.reshape(bsz * t, -1)],
                           w_out_bf, e, x.reshape(bsz * t, d), ln_g, ln_b)
    rows = []
    for g, (win, _) in enumerate(B_CONFIGS):
        keep = t if dwa_bufs is not None else min(win, t)
        kg = h3[:, t - keep:, U_KB * LANES + g * B_OUT:U_KB * LANES + (g + 1) * B_OUT]
        vg = h3[:, t - keep:, U_VB * LANES + g * B_OUT:U_VB * LANES + (g + 1) * B_OUT]
        rows.append(jnp.stack([kg.reshape(bsz, keep, B_HEADS_PER_GROUP, B_HEAD_DIM),
                               vg.reshape(bsz, keep, B_HEADS_PER_GROUP, B_HEAD_DIM)], axis=2))
    return x2.reshape(bsz, t, d), x2_bf.reshape(bsz, t, d), s_new, h3[:, t - 1, :A_SHIFT_W], rows


def _odd_layer(x, x_bf, mkv, r0, tabs, w_in_bf, w_out_bf, o, ln_g, ln_b):
    bsz, t, d = x.shape
    h = matmul(x_bf.reshape(bsz * t, d), w_in_bf, o)
    h3 = h.reshape(bsz, t, NU * LANES)
    u_c, r_new = retention(h3, tabs[0], tabs[1], r0)
    u_m = mem_attn(h3, mkv[0], mkv[1], U_QM_ODD, U_GM_ODD)
    x2, x2_bf = outproj_ln([u_c.reshape(bsz * t, -1), u_m.reshape(bsz * t, -1)],
                           w_out_bf, o, x.reshape(bsz * t, d), ln_g, ln_b)
    return x2.reshape(bsz, t, d), x2_bf.reshape(bsz, t, d), r_new


def kernel(x_prompt, x_sample, state_rwkv, state_rwkv_shift, cache_dwa_g0, cache_dwa_g1, cache_dwa_g2, state_ret, cache_mem_kv, mem_prompt, w_in_even, w_out_even, w_in_odd, w_out_odd, w_mem_kv, ln_g, ln_b, rwkv_mu, rwkv_w0, rwkv_w_up, rwkv_a0, rwkv_a_up, rwkv_k_k, rwkv_k_a, rwkv_r_k, rwkv_lnx_g, rwkv_lnx_b):
    xp, xs = x_prompt, x_sample
    xp_bf, xs_bf = xp.astype(BF16), xs.astype(BF16)
    bp, tp_len, d = xp.shape
    bs, ts_len, _ = xs.shape
    dwa_cache = (cache_dwa_g0, cache_dwa_g1, cache_dwa_g2)
    head_ones = _block_ones(LANES, A_HEAD_DIM)
    mem_bf = mem_prompt.reshape(bp * M_TOKENS, d).astype(BF16)
    tabs_p = rope_tables(max(tp_len, C_CHUNK), 0)
    tabs_s = rope_tables(max(ts_len, C_CHUNK), PAST_LEN)
    rwkv_p, rwkv_s, shift_p, shift_s, ret_p, ret_s, mem_p = [], [], [], [], [], [], []
    dwa_p = [[] for _ in B_CONFIGS]
    dwa_s = [[] for _ in B_CONFIGS]
    w_in_even_bf = jnp.concatenate(
        [w_in_even.astype(BF16), jnp.zeros(w_in_even.shape[:2] + (EVEN_IN_PAD - EVEN_IN,), BF16)], axis=-1)
    w_out_even_bf = w_out_even.astype(BF16)
    w_in_odd_bf = w_in_odd
    w_out_odd_bf = w_out_odd.astype(BF16)
    w_mem_bf = w_mem_kv
    for l in range(DEPTH):
        mkv_new = matmul(mem_bf, w_mem_bf, l).reshape(bp, M_TOKENS, 2 * M_WIDTH)
        mem_p.append(mkv_new.reshape(bp, M_TOKENS, 2, M_HEADS, M_HEAD_DIM))
        mkv_p = (mkv_new, None)
        mkv_s = (cache_mem_kv, l)
        if l % 2 == 0:
            e = l // 2
            w_in_bf, w_out_bf = w_in_even_bf, w_out_even_bf
            rw = (rwkv_mu[e], rwkv_w0[e], rwkv_w_up[e], rwkv_a0[e], rwkv_a_up[e], rwkv_k_k[e], rwkv_k_a[e],
                  rwkv_r_k[e], rwkv_lnx_g[e], rwkv_lnx_b[e])
            s0 = jnp.zeros((bp, A_HEADS, A_HEAD_DIM, A_HEAD_DIM), F32)
            sh0 = jnp.zeros((bp, A_SHIFT_W), F32)
            xp, xp_bf, st, sh, rows = _even_layer(xp, xp_bf, mkv_p, None, s0, sh0, w_in_bf, w_out_bf, e,
                                                  ln_g[l], ln_b[l], rw, head_ones)
            rwkv_p.append(st)
            shift_p.append(sh)
            for g in range(B_GROUPS):
                dwa_p[g].append(rows[g])
            bufs = dwa_cache
            xs, xs_bf, st, sh, rows = _even_layer(xs, xs_bf, mkv_s, bufs, state_rwkv[e], state_rwkv_shift[e],
                                                  w_in_bf, w_out_bf, e, ln_g[l], ln_b[l], rw, head_ones)
            rwkv_s.append(st)
            shift_s.append(sh)
            for g in range(B_GROUPS):
                dwa_s[g].append(rows[g])
        else:
            o = l // 2
            r0 = jnp.zeros((bp, C_HEADS, C_HEAD_DIM, C_HEAD_DIM), F32)
            xp, xp_bf, st = _odd_layer(xp, xp_bf, mkv_p, r0, tabs_p, w_in_odd_bf, w_out_odd_bf, o, ln_g[l], ln_b[l])
            ret_p.append(st)
            xs, xs_bf, st = _odd_layer(xs, xs_bf, mkv_s, state_ret[o], tabs_s, w_in_odd_bf, w_out_odd_bf, o,
                                       ln_g[l], ln_b[l])
            ret_s.append(st)
    return (xp, xs, jnp.stack(rwkv_p), jnp.stack(rwkv_s), jnp.stack(shift_p), jnp.stack(shift_s),
            jnp.stack(dwa_p[0]), jnp.stack(dwa_s[0]), jnp.stack(dwa_p[1]), jnp.stack(dwa_s[1]),
            jnp.stack(dwa_p[2]), jnp.stack(dwa_s[2]), jnp.stack(ret_p), jnp.stack(ret_s), jnp.stack(mem_p))
```

```python
import functools

import numpy as np
import jax
import jax.numpy as jnp
from jax import lax
from jax.experimental import pallas as pl
from jax.experimental.pallas import tpu as pltpu

F32 = jnp.float32
BF16 = jnp.bfloat16

D_MODEL = 2048
DEPTH = 4
PAST_LEN = 16384
ALPHA = (2 * DEPTH) ** 0.25
LN_EPS = 1e-5
A_HEADS = 12
A_HEAD_DIM = 64
A_WIDTH = A_HEADS * A_HEAD_DIM
A_LORA = 64
A_SHIFT_W = 3 * A_WIDTH + 2 * A_LORA
A_GN_EPS = 64e-5
B_CONFIGS = ((128, 1), (512, 4), (2048, 16))
B_GROUPS = 3
B_HEADS_PER_GROUP = 4
B_HEAD_DIM = 64
B_OUT = B_HEADS_PER_GROUP * B_HEAD_DIM
B_WIDTH = B_GROUPS * B_OUT
B_BLK = 128
B_RES_PER_ITER = 4
C_HEADS = 6
C_HEAD_DIM = 256
C_WIDTH = C_HEADS * C_HEAD_DIM
C_CHUNK = 128
C_HEADS_PER_STEP = 6
C_ROT_BASE = 10000.0
C_NORM_EPS = 1e-6
M_TOKENS = 256
M_HEADS = 4
M_HEAD_DIM = 128
M_WIDTH = M_HEADS * M_HEAD_DIM
EVEN_IN = A_SHIFT_W + A_WIDTH + 3 * B_WIDTH + B_OUT + 2 * M_WIDTH
ODD_IN = 4 * C_WIDTH + 2 * M_WIDTH

LANES = 128
SUBLANES = 8
VMEM_LIMIT = 48 * 1024 * 1024

EVEN_IN_PAD = 7168
U_GATE_A = A_SHIFT_W // LANES
U_QB = U_GATE_A + A_WIDTH // LANES
U_KB = U_QB + B_WIDTH // LANES
U_VB = U_KB + B_WIDTH // LANES
U_GATE_B = U_VB + B_WIDTH // LANES
U_QM_EVEN = U_GATE_B + B_OUT // LANES
U_GM_EVEN = U_QM_EVEN + M_WIDTH // LANES
U_QM_ODD = 4 * C_WIDTH // LANES
U_GM_ODD = U_QM_ODD + M_WIDTH // LANES
NU = EVEN_IN_PAD // LANES


def _cparams(sem):
    return pltpu.CompilerParams(dimension_semantics=sem, vmem_limit_bytes=VMEM_LIMIT)


def _sigmoid(z):
    return 1.0 / (1.0 + jnp.exp(-z))


def _silu(z):
    return z * _sigmoid(z)


def _mm_kernel(x_ref, w_ref, o_ref, w_scr):
    @pl.when(pl.program_id(1) == 0)
    def _():
        w_scr[...] = w_ref[0].astype(BF16)

    o_ref[...] = jnp.dot(x_ref[...], w_scr[...], preferred_element_type=F32)


def matmul(x, w, layer):
    m, k = x.shape
    n = w.shape[2]
    tm = min(m, 1024)
    tn = 1024 if n % 1024 == 0 else 512
    return pl.pallas_call(
        _mm_kernel,
        grid=(n // tn, m // tm),
        in_specs=[pl.BlockSpec((tm, k), lambda j, i: (i, 0)),
                  pl.BlockSpec((1, k, tn), lambda j, i: (layer, 0, j))],
        out_specs=pl.BlockSpec((tm, tn), lambda j, i: (i, j)),
        out_shape=jax.ShapeDtypeStruct((m, n), F32),
        scratch_shapes=[pltpu.VMEM((k, tn), BF16)],
        compiler_params=_cparams(("parallel", "arbitrary")),
        name="matmul",
    )(x, w)


def _outproj_kernel(n_u, *refs):
    u_refs = refs[:n_u]
    w_refs = refs[n_u:2 * n_u]
    x_ref, g_ref, b_ref, o_ref, obf_ref = refs[2 * n_u:]
    acc = jnp.dot(u_refs[0][...], w_refs[0][0], preferred_element_type=F32)
    for u_ref, w_ref in zip(u_refs[1:], w_refs[1:]):
        acc = acc + jnp.dot(u_ref[...], w_ref[0], preferred_element_type=F32)
    z = ALPHA * x_ref[...] + acc
    mu = jnp.mean(z, axis=-1, keepdims=True)
    zc = z - mu
    var = jnp.mean(zc * zc, axis=-1, keepdims=True)
    y = zc * lax.rsqrt(var + LN_EPS) * g_ref[...] + b_ref[...]
    o_ref[...] = y
    obf_ref[...] = y.astype(BF16)


def outproj_ln(us, w_out, layer, x, g, b):
    m, d = x.shape
    tm = min(m, 512)
    n_u = len(us)
    in_specs = [pl.BlockSpec((tm, u.shape[1]), lambda i: (i, 0)) for u in us]
    row = 0
    for u in us:
        kw = u.shape[1]
        assert row % kw == 0
        in_specs.append(pl.BlockSpec((1, kw, d), functools.partial(lambda i, r: (layer, r, 0), r=row // kw)))
        row += kw
    assert row == w_out.shape[1]
    in_specs += [pl.BlockSpec((tm, d), lambda i: (i, 0)),
                 pl.BlockSpec((1, d), lambda i: (0, 0)),
                 pl.BlockSpec((1, d), lambda i: (0, 0))]
    return pl.pallas_call(
        functools.partial(_outproj_kernel, n_u),
        grid=(m // tm,),
        in_specs=in_specs,
        out_specs=[pl.BlockSpec((tm, d), lambda i: (i, 0)), pl.BlockSpec((tm, d), lambda i: (i, 0))],
        out_shape=[jax.ShapeDtypeStruct((m, d), F32), jax.ShapeDtypeStruct((m, d), BF16)],
        compiler_params=_cparams(("parallel",)),
        name="outproj_ln",
    )(*us, *([w_out] * n_u), x, g.reshape(1, d), b.reshape(1, d))


def _seg_sum(x, ones_bf):
    hi = x.astype(BF16)
    r1 = x - hi.astype(F32)
    mid = r1.astype(BF16)
    lo = (r1 - mid.astype(F32)).astype(BF16)
    outs = []
    for p in range(x.shape[1] // LANES):
        sl = slice(p * LANES, (p + 1) * LANES)
        terms = jnp.concatenate([hi[:, sl], mid[:, sl], lo[:, sl]], axis=1)
        outs.append(jnp.dot(terms, jnp.concatenate([ones_bf] * 3, axis=0), preferred_element_type=F32))
    return jnp.concatenate(outs, axis=1)


def _block_ones(width, seg):
    i = np.arange(width) // seg
    return jnp.asarray((i[:, None] == i[None, :]).astype(np.float32), dtype=BF16)


def _rwkv_prep_kernel(tt, tp, h_ref, prev_ref, sh0_ref, mu_ref, w0_ref, wup_ref, a0_ref, aup_ref,
                      kk_ref_p, ka_ref_p, rk_ref_p, ones_ref,
                      r_o, d_o, k_o, kk_o, b_o, vt_o, bonus_o, *scratch):
    i = pl.program_id(1)
    x = h_ref[0]
    pr = prev_ref[0]
    prev_last = pr[pr.shape[0] - 1:pr.shape[0], :]
    first = jnp.where(i == 0, sh0_ref[0], prev_last)
    row = lax.broadcasted_iota(jnp.int32, x.shape, 0)
    if tt % SUBLANES == 0:
        rolled = pltpu.roll(x, 1, axis=0)
    else:
        rolled = jnp.concatenate([x[tt - 1:tt], x[:tt - 1]], axis=0)
    prev = jnp.where(row == 0, first, rolled)
    hs = x + (prev - x) * mu_ref[...]
    r = hs[:, 0:A_WIDTH]
    k = hs[:, A_WIDTH:2 * A_WIDTH]
    v = hs[:, 2 * A_WIDTH:3 * A_WIDTH]
    hw = hs[:, 3 * A_WIDTH:3 * A_WIDTH + A_LORA]
    ha = hs[:, 3 * A_WIDTH + A_LORA:A_SHIFT_W]
    zw = w0_ref[...] + jnp.dot(jnp.tanh(hw).astype(BF16), wup_ref[...], preferred_element_type=F32)
    nz = -zw
    softplus = jnp.maximum(nz, 0.0) + jnp.log(1.0 + jnp.exp(-jnp.abs(nz)))
    w_log = -softplus - 0.5
    neg_log_decay = jnp.exp(w_log)
    a = _sigmoid(a0_ref[...] + jnp.dot(ha.astype(BF16), aup_ref[...], preferred_element_type=F32))
    ones_bf = ones_ref[...]
    kk = k * kk_ref_p[...]
    kk = kk * lax.rsqrt(jnp.maximum(_seg_sum(kk * kk, ones_bf), 1e-24))
    k2 = k * (1.0 + (a - 1.0) * ka_ref_p[...])
    bb = kk * a
    bonus_o[0] = _seg_sum(r * k2 * rk_ref_p[...], ones_bf) * v
    width = r_o.shape[3]
    for s in range(A_WIDTH // width):
        sl = slice(s * width, (s + 1) * width)
        r_o[0, s] = r[:, sl]
        d_o[0, s] = neg_log_decay[:, sl]
        k_o[0, s] = k2[:, sl]
        kk_o[0, s] = kk[:, sl]
        b_o[0, s] = bb[:, sl]
    for p in range(A_HEADS // 2):
        vp = v[:, p * LANES:(p + 1) * LANES]
        if tt == tp:
            vt_o[0, p] = vp.T
        else:
            pad = scratch[0]
            pad[...] = jnp.zeros_like(pad)
            pad[0:tt, :] = vp
            vt_o[0, p] = pad[...].T


def rwkv_prep(h3, shift0, prm, head_ones):
    bsz, t, _ = h3.shape
    tt = min(t, 128)
    tp = max(tt, LANES)
    nt = t // tt
    pr_rows = min(t, SUBLANES)
    pb = tt // pr_rows
    mu, w0, wup, a0, aup, k_k, k_a, r_k = prm
    row_spec = pl.BlockSpec((1, A_WIDTH), lambda b, i: (0, 0))
    lora_spec = pl.BlockSpec((A_LORA, A_WIDTH), lambda b, i: (0, 0))
    width = LANES if t % A_CHUNK == 0 else A_HEAD_DIM
    head_spec = pl.BlockSpec((1, A_WIDTH // width, tt, width), lambda b, i: (b, 0, i, 0))
    head_shape = jax.ShapeDtypeStruct((bsz, A_WIDTH // width, t, width), F32)
    scratch = [] if tt == tp else [pltpu.VMEM((tp, LANES), F32)]
    return pl.pallas_call(
        functools.partial(_rwkv_prep_kernel, tt, tp),
        grid=(bsz, nt),
        in_specs=[pl.BlockSpec((1, tt, A_SHIFT_W), lambda b, i: (b, i, 0)),
                  pl.BlockSpec((1, pr_rows, A_SHIFT_W), lambda b, i: (b, jnp.maximum(i * pb - 1, 0), 0)),
                  pl.BlockSpec((1, 1, A_SHIFT_W), lambda b, i: (b, 0, 0)),
                  pl.BlockSpec((1, A_SHIFT_W), lambda b, i: (0, 0)),
                  row_spec, lora_spec, row_spec, lora_spec, row_spec, row_spec, row_spec,
                  pl.BlockSpec((LANES, LANES), lambda b, i: (0, 0))],
        out_specs=[head_spec] * 5 + [
            pl.BlockSpec((1, A_HEADS // 2, LANES, tp), lambda b, i: (b, 0, 0, i)),
            pl.BlockSpec((1, tt, A_WIDTH), lambda b, i: (b, i, 0))],
        out_shape=[head_shape] * 5 + [
            jax.ShapeDtypeStruct((bsz, A_HEADS // 2, LANES, nt * tp), F32),
            jax.ShapeDtypeStruct((bsz, t, A_WIDTH), F32)],
        scratch_shapes=scratch,
        compiler_params=_cparams(("parallel", "parallel")),
        name="rwkv_prep",
    )(h3, h3, shift0.reshape(bsz, 1, A_SHIFT_W), mu.reshape(1, -1), w0.reshape(1, -1), wup.astype(BF16),
      a0.reshape(1, -1), aup.astype(BF16), k_k.reshape(1, -1), k_a.reshape(1, -1), r_k.reshape(1, -1), head_ones)


def _rwkv_scan_kernel(steps, r_ref, d_ref, k_ref, kk_ref, b_ref, vt_ref, s0_ref, yt_ref, sout_ref, s_scr, p_scr):
    c = pl.program_id(1)

    @pl.when(c == 0)
    def _():
        s_scr[...] = s0_ref[0]

    tp = vt_ref.shape[3]
    lane = lax.broadcasted_iota(jnp.int32, (A_HEAD_DIM, tp), 1)
    yt_ref[...] = jnp.zeros_like(yt_ref)
    p_scr[...] = jnp.zeros_like(p_scr)
    ones_bf = jnp.ones((A_HEAD_DIM, tp), BF16)

    def write_y(t):
        msk = lane == t
        for hd in range(A_HEADS):
            y = jnp.dot(p_scr[hd], ones_bf, preferred_element_type=F32)
            yt_ref[0, hd] = jnp.where(msk, y, yt_ref[0, hd])

    def body(t, carry):
        write_y(t - 1)
        msk = lane == t
        sas, vcols = [], []
        for hd in range(A_HEADS):
            sas.append(-jnp.sum(s_scr[hd] * kk_ref[0, hd, pl.ds(t, 1), :], axis=-1, keepdims=True))
            vcols.append(jnp.sum(jnp.where(msk, vt_ref[0, hd], 0.0), axis=-1, keepdims=True))
        for hd in range(A_HEADS):
            s = (s_scr[hd] * jnp.exp(-d_ref[0, hd, pl.ds(t, 1), :]) + sas[hd] * b_ref[0, hd, pl.ds(t, 1), :]
                 + vcols[hd] * k_ref[0, hd, pl.ds(t, 1), :])
            s_scr[hd] = s
            p_scr[hd] = (s * r_ref[0, hd, pl.ds(t, 1), :]).astype(BF16)
        return carry

    lax.fori_loop(0, steps, body, 0, unroll=4)
    write_y(steps - 1)

    @pl.when(c == pl.num_programs(1) - 1)
    def _():
        sout_ref[0] = s_scr[...]


A_CHUNK = 128
A_CHUNK_HEADS = 6
_NT = (((1,), (1,)), ((), ()))


def _split2(x):
    hi = x.astype(BF16)
    return hi, (x - hi.astype(F32)).astype(BF16)


def _mm3(a, b):
    ah, al = _split2(a)
    bh, bl = _split2(b)
    return jnp.dot(jnp.concatenate([ah, ah, al], axis=1), jnp.concatenate([bh, bl, bh], axis=0),
                   preferred_element_type=F32)


def _mm3_nt(a, b):
    ah, al = _split2(a)
    bh, bl = _split2(b)
    return lax.dot_general(jnp.concatenate([ah, ah, al], axis=1), jnp.concatenate([bh, bl, bh], axis=1), _NT,
                           preferred_element_type=F32)


def _inv_unit_lower(lbs, row, col):
    def blk(s):
        return (row // s) == (col // s)
    eye = jnp.where(row == col, 1.0, 0.0)
    ds = [jnp.where(blk(8), lb, 0.0) for lb in lbs]
    d2s = [_mm3(d, d) for d in ds]
    d4s = [_mm3(d2, d2) for d2 in d2s]
    xs = [eye + d for d in ds]
    xs = [x + _mm3(x, d2) for x, d2 in zip(xs, d2s)]
    xs = [x + _mm3(x, d4) for x, d4 in zip(xs, d4s)]
    s = 8
    while s < A_CHUNK:
        msk = blk(2 * s) & jnp.logical_not(blk(s))
        ts = [_mm3(x, jnp.where(msk, lb, 0.0)) for x, lb in zip(xs, lbs)]
        xs = [x + _mm3(t, x) for x, t in zip(xs, ts)]
        s *= 2
    return xs


def _rwkv_chunk_heads(es, rs, kks, bs, k2s, vs, sps, tril, row, col):
    n_tok, n = A_CHUNK, A_HEAD_DIM
    n_pair = len(es)
    idx = range(2 * n_pair)
    cs = []
    for e in es:
        e_hi = e.astype(BF16)
        e_r1 = e - e_hi.astype(F32)
        e_mid = e_r1.astype(BF16)
        e_lo = (e_r1 - e_mid.astype(F32)).astype(BF16)
        cs.append(jnp.dot(jnp.concatenate([tril, tril, tril], axis=1),
                          jnp.concatenate([e_hi, e_mid, e_lo], axis=0), preferred_element_type=F32))
    ms = [c[n_tok // 2 - 1:n_tok // 2] for c in cs]
    ccs = [c - m for c, m in zip(cs, ms)]
    gbs = [jnp.exp(cc) for cc in ccs]
    own = [lax.broadcasted_iota(jnp.int32, (n_tok, LANES), 1) < n]
    own.append(jnp.logical_not(own[0]))
    own_s = [lax.broadcasted_iota(jnp.int32, (n, LANES), 1) < n]
    own_s.append(jnp.logical_not(own_s[0]))

    def halves(xs, masks):
        return [jnp.where(masks[i % 2], xs[i // 2], 0.0) for i in idx]

    ats = halves([-kks[p] * jnp.exp(es[p] - ccs[p]) for p in range(n_pair)], own)
    bts = halves([bs[p] * gbs[p] for p in range(n_pair)], own)
    kts = halves([k2s[p] * gbs[p] for p in range(n_pair)], own)
    rts = halves([rs[p] * jnp.exp(-ccs[p]) for p in range(n_pair)], own)
    s0ps = halves([sps[p] * jnp.exp(-ms[p]) for p in range(n_pair)], own_s)
    bks = [jnp.concatenate([bts[i], kts[i]], axis=0) for i in idx]
    zs = [_mm3_nt(ats[i], bks[i]) for i in idx]
    m2s = [_mm3_nt(bks[i], rts[i]) for i in idx]
    low = row > col
    upp = row <= col
    lbs = [jnp.where(low, z[:, :n_tok], 0.0) for z in zs]
    lks = [jnp.where(low, z[:, n_tok:], 0.0) for z in zs]
    mbks = [jnp.concatenate([jnp.where(upp, m2[:n_tok], 0.0), jnp.where(upp, m2[n_tok:], 0.0)], axis=0)
            for m2 in m2s]
    ws = _inv_unit_lower(lbs, row, col)
    ptqs = [_mm3(ws[i], jnp.concatenate([ats[i], lks[i]], axis=1)) for i in idx]
    srps = [_mm3_nt(s0ps[i], jnp.concatenate([rts[i], ptqs[i][:, :LANES]], axis=0)) for i in idx]
    us = [srps[i][:, n_tok:] + _mm3_nt(vs[i], ptqs[i][:, LANES:]) for i in idx]
    yss = [_mm3(jnp.concatenate([us[i], vs[i]], axis=1), jnp.concatenate([mbks[i], bks[i]], axis=1)) for i in idx]
    ys = [srps[i][:, :n_tok] + yss[i][:, :n_tok] for i in idx]
    s1s = [(s0ps[2 * p] + yss[2 * p][:, n_tok:] + s0ps[2 * p + 1] + yss[2 * p + 1][:, n_tok:])
           * jnp.exp(-ccs[p][n_tok - 1:n_tok]) for p in range(n_pair)]
    return ys, s1s


def _rwkv_chunk_kernel(r_ref, e_ref, k_ref, kk_ref, b_ref, vt_ref, s0_ref, yt_ref, sout_ref, s_scr):
    c = pl.program_id(1)
    n_pair = A_HEADS // 2

    @pl.when(c == 0)
    def _():
        for p in range(n_pair):
            s_scr[p] = jnp.concatenate([s0_ref[0, 2 * p], s0_ref[0, 2 * p + 1]], axis=1)

    row = lax.broadcasted_iota(jnp.int32, (A_CHUNK, A_CHUNK), 0)
    col = lax.broadcasted_iota(jnp.int32, (A_CHUNK, A_CHUNK), 1)
    tril = jnp.where(row >= col, 1.0, 0.0).astype(BF16)
    pairs_per_group = A_CHUNK_HEADS // 2

    def group(gi, carry):
        prs = [gi * pairs_per_group + j for j in range(pairs_per_group)]
        hds = [2 * p + k for p in prs for k in range(2)]
        ys, s1s = _rwkv_chunk_heads([e_ref[0, p] for p in prs], [r_ref[0, p] for p in prs],
                                    [kk_ref[0, p] for p in prs], [b_ref[0, p] for p in prs],
                                    [k_ref[0, p] for p in prs], [vt_ref[0, hd] for hd in hds],
                                    [s_scr[p] for p in prs], tril, row, col)
        for hd, y in zip(hds, ys):
            yt_ref[0, hd] = y
        for p, s1 in zip(prs, s1s):
            s_scr[p] = s1
        return carry

    lax.fori_loop(0, n_pair // pairs_per_group, group, 0)

    @pl.when(c == pl.num_programs(1) - 1)
    def _():
        for p in range(n_pair):
            sp = s_scr[p]
            sout_ref[0, 2 * p] = sp[:, :A_HEAD_DIM]
            sout_ref[0, 2 * p + 1] = sp[:, A_HEAD_DIM:]


def rwkv_scan(ops, vt, s0):
    r, d, k, kk, b = ops
    bsz, n_slab, t, width = r.shape
    tt = min(t, 128)
    tp = max(tt, LANES)
    nt = t // tt
    vt = vt.reshape(bsz, A_HEADS, A_HEAD_DIM, nt * tp)
    chunked = t % A_CHUNK == 0
    assert width == (LANES if chunked else A_HEAD_DIM)
    head_spec = pl.BlockSpec((1, n_slab, tt, width), lambda b_, i: (b_, 0, i, 0))
    vt_spec = pl.BlockSpec((1, A_HEADS, A_HEAD_DIM, tp), lambda b_, i: (b_, 0, 0, i))
    st_spec = pl.BlockSpec((1, A_HEADS, A_HEAD_DIM, A_HEAD_DIM), lambda b_, i: (b_, 0, 0, 0))
    if chunked:
        scratch = [pltpu.VMEM((A_HEADS // 2, A_HEAD_DIM, LANES), F32)]
    else:
        scratch = [pltpu.VMEM((A_HEADS, A_HEAD_DIM, A_HEAD_DIM), F32),
                   pltpu.VMEM((A_HEADS, A_HEAD_DIM, A_HEAD_DIM), BF16)]
    yt, s_new = pl.pallas_call(
        _rwkv_chunk_kernel if chunked else functools.partial(_rwkv_scan_kernel, tt),
        grid=(bsz, nt),
        in_specs=[head_spec] * 5 + [vt_spec, st_spec],
        out_specs=[vt_spec, st_spec],
        out_shape=[jax.ShapeDtypeStruct(vt.shape, F32), jax.ShapeDtypeStruct(s0.shape, F32)],
        scratch_shapes=scratch,
        compiler_params=_cparams(("parallel", "arbitrary")),
        name="rwkv_chunk" if chunked else "rwkv_scan",
    )(r, d, k, kk, b, vt, s0)
    return yt.reshape(bsz, A_HEADS // 2, LANES, nt * tp), s_new


def _dwa_prompt_kernel(dil, *refs):
    npair = B_OUT // LANES
    in_refs = [refs[5 * hp:5 * hp + 5] for hp in range(npair)]
    out_refs = refs[5 * npair:]
    n = pl.program_id(1)
    i = lax.broadcasted_iota(jnp.int32, (B_BLK, B_BLK), 0)
    j = lax.broadcasted_iota(jnp.int32, (B_BLK, B_BLK), 1)
    mask_prev = j >= i + jnp.where(n > 0, 0, B_BLK)
    mask_cur = j <= i
    neg = -jnp.inf
    scale = B_HEAD_DIM ** -0.5
    dn = (((1,), (1,)), ((), ()))

    n_res = B_RES_PER_ITER
    n_hh = LANES // B_HEAD_DIM

    def residues(it, carry):
        if dil > 1:
            rows = [pl.ds(it * n_res + k, B_BLK, stride=dil) for k in range(n_res)]
            loaded = {(k, hp): [ref[0, rows[k], :] for ref in in_refs[hp]]
                      for k in range(n_res) for hp in range(npair)}
            prev_masks = [mask_prev] * n_res
        else:
            rows = [pl.ds(k * B_BLK, B_BLK) for k in range(n_res)]
            loaded = {}
            for hp in range(npair):
                q_ref, kc_ref, kp_ref, vc_ref, vp_ref = in_refs[hp]
                for k in range(n_res):
                    loaded[(k, hp)] = [q_ref[0, rows[k], :], kc_ref[0, rows[k], :],
                                       kp_ref[0] if k == 0 else kc_ref[0, rows[k - 1], :],
                                       vc_ref[0, rows[k], :],
                                       vp_ref[0] if k == 0 else vc_ref[0, rows[k - 1], :]]
            prev_masks = [mask_prev] + [j >= i] * (n_res - 1)
        units = [(k, hp, hh) for k in range(n_res) for hp in range(npair) for hh in range(n_hh)]

        def part(u, which):
            k, hp, hh = u
            return loaded[(k, hp)][which][:, hh * B_HEAD_DIM:(hh + 1) * B_HEAD_DIM].astype(BF16)

        qs = [part(u, 0) for u in units]
        sps = [jnp.where(prev_masks[u[0]], lax.dot_general(q, part(u, 2), dn, preferred_element_type=F32) * scale, neg)
               for q, u in zip(qs, units)]
        scs = [jnp.where(mask_cur, lax.dot_general(q, part(u, 1), dn, preferred_element_type=F32) * scale, neg)
               for q, u in zip(qs, units)]
        ms = [jnp.maximum(jnp.max(sp, axis=-1, keepdims=True), jnp.max(sc, axis=-1, keepdims=True))
              for sp, sc in zip(sps, scs)]
        eps = [jnp.exp(sp - m) for sp, m in zip(sps, ms)]
        ecs = [jnp.exp(sc - m) for sc, m in zip(scs, ms)]
        ls = [jnp.sum(ep, axis=-1, keepdims=True) + jnp.sum(ec, axis=-1, keepdims=True) for ep, ec in zip(eps, ecs)]
        os_ = [jnp.dot((ep / l).astype(BF16), part(u, 4), preferred_element_type=F32)
               + jnp.dot((ec / l).astype(BF16), part(u, 3), preferred_element_type=F32)
               for ep, ec, l, u in zip(eps, ecs, ls, units)]
        lses = [jnp.broadcast_to(m + jnp.log(l), (B_BLK, B_HEAD_DIM)) for m, l in zip(ms, ls)]
        for k in range(n_res):
            for hp in range(npair):
                sel = [i_u for i_u, u in enumerate(units) if u[0] == k and u[1] == hp]
                out_refs[2 * hp][0, rows[k], :] = jnp.concatenate([os_[i_u] for i_u in sel], axis=-1)
                out_refs[2 * hp + 1][0, rows[k], :] = jnp.concatenate([lses[i_u] for i_u in sel], axis=-1)
        return carry

    lax.fori_loop(0, max(dil // n_res, 1), residues, 0)


def dwa_prompt(h3, g):
    bsz, t, _ = h3.shape
    win, dil = B_CONFIGS[g]
    nw = B_RES_PER_ITER if dil == 1 else 1
    assert win == B_BLK * dil and t % (nw * win) == 0 and (dil == 1 or dil % B_RES_PER_ITER == 0)
    npair = B_OUT // LANES

    def in_spec(unit, prev, hp):
        def imap(b, n):
            nn = jnp.maximum(n * nw - 1, 0) if prev else n
            return (b, nn, unit + npair * g + hp)
        return pl.BlockSpec((1, win if prev else nw * win, LANES), imap)

    in_specs = []
    for hp in range(npair):
        in_specs += [in_spec(U_QB, False, hp), in_spec(U_KB, False, hp), in_spec(U_KB, True, hp),
                     in_spec(U_VB, False, hp), in_spec(U_VB, True, hp)]
    out_spec = pl.BlockSpec((1, nw * win, LANES), lambda b, n: (b, n, 0))
    out_shape = jax.ShapeDtypeStruct((bsz, t, LANES), F32)
    outs = pl.pallas_call(
        functools.partial(_dwa_prompt_kernel, dil),
        grid=(bsz, t // (nw * win)),
        in_specs=in_specs,
        out_specs=[out_spec] * (2 * npair),
        out_shape=[out_shape] * (2 * npair),
        compiler_params=_cparams(("parallel", "parallel")),
        name="dwa_prompt",
    )(*([h3] * len(in_specs)))
    return list(outs)


def _dwa_decode_kernel(t_new, h_ref, c0_ref, c1_ref, c2_ref, *out_refs):
    scale = B_HEAD_DIM ** -0.5
    dn = (((1,), (1,)), ((), ()))
    head_of_lane = lax.broadcasted_iota(jnp.int32, (SUBLANES, B_OUT), 1) // B_HEAD_DIM
    sub = lax.broadcasted_iota(jnp.int32, (SUBLANES, B_OUT), 0)
    own = head_of_lane == sub
    m_idx = lax.broadcasted_iota(jnp.int32, (SUBLANES, B_BLK), 1)
    neg = -jnp.inf
    hrow = h_ref[0]
    for g, c_ref in enumerate((c0_ref, c1_ref, c2_ref)):
        _, dil = B_CONFIGS[g]
        g_refs = out_refs[4 * g:4 * g + 4]
        qs = hrow[:, U_QB * LANES + g * B_OUT:U_QB * LANES + (g + 1) * B_OUT]
        ks = hrow[:, U_KB * LANES + g * B_OUT:U_KB * LANES + (g + 1) * B_OUT]
        vs = hrow[:, U_VB * LANES + g * B_OUT:U_VB * LANES + (g + 1) * B_OUT]
        for t in range(t_new):
            res = t % dil
            kbuf = jnp.concatenate([c_ref[0, 0, :, res, 0, hd, :] for hd in range(B_HEADS_PER_GROUP)], axis=-1)
            vbuf = jnp.concatenate([c_ref[0, 0, :, res, 1, hd, :] for hd in range(B_HEADS_PER_GROUP)], axis=-1)
            qbd = jnp.where(own, jnp.broadcast_to(qs[t:t + 1], (SUBLANES, B_OUT)), 0.0)
            s_buf = lax.dot_general(qbd.astype(BF16), kbuf.astype(BF16), dn, preferred_element_type=F32) * scale
            s_buf = jnp.where(m_idx * dil + res >= t, s_buf, neg)
            new_rows = [n for n in range(t + 1) if (t - n) % dil == 0]
            kq = qbd.astype(BF16).astype(F32)
            s_new = [jnp.sum(kq * ks[n:n + 1].astype(BF16).astype(F32), axis=-1, keepdims=True) * scale
                     for n in new_rows]
            m = jnp.max(s_buf, axis=-1, keepdims=True)
            for sn in s_new:
                m = jnp.maximum(m, sn)
            e_buf = jnp.exp(s_buf - m)
            e_new = [jnp.exp(sn - m) for sn in s_new]
            l = jnp.sum(e_buf, axis=-1, keepdims=True)
            for en in e_new:
                l = l + en
            o = jnp.dot((e_buf / l).astype(BF16), vbuf.astype(BF16), preferred_element_type=F32)
            for n, en in zip(new_rows, e_new):
                o = o + (en / l).astype(BF16).astype(F32) * vs[n:n + 1].astype(BF16).astype(F32)
            lse = jnp.broadcast_to(m + jnp.log(l), (SUBLANES, B_OUT))
            o_row = jnp.sum(jnp.where(own, o, 0.0), axis=0, keepdims=True)
            lse_row = jnp.sum(jnp.where(own, lse, 0.0), axis=0, keepdims=True)
            for hp in range(B_OUT // LANES):
                g_refs[2 * hp][0, t:t + 1, :] = o_row[:, hp * LANES:(hp + 1) * LANES]
                g_refs[2 * hp + 1][0, t:t + 1, :] = lse_row[:, hp * LANES:(hp + 1) * LANES]


def dwa_decode(h3, caches, e):
    bsz, t, _ = h3.shape
    views, specs = [], []
    for g, (win, dil) in enumerate(B_CONFIGS):
        assert caches[g].shape[2] == win and (dil == 1 or t <= dil)
        views.append(caches[g].reshape(caches[g].shape[0], bsz, win // dil, dil, 2, B_HEADS_PER_GROUP, B_HEAD_DIM))
        specs.append(pl.BlockSpec((1, 1, B_BLK, min(dil, t), 2, B_HEADS_PER_GROUP, B_HEAD_DIM),
                                  lambda b: (e, b, 0, 0, 0, 0, 0)))
    out_spec = pl.BlockSpec((1, t, LANES), lambda b: (b, 0, 0))
    out_shape = jax.ShapeDtypeStruct((bsz, t, LANES), F32)
    n_out = 4 * B_GROUPS
    outs = pl.pallas_call(
        functools.partial(_dwa_decode_kernel, t),
        grid=(bsz,),
        in_specs=[pl.BlockSpec((1, t, NU * LANES), lambda b: (b, 0, 0))] + specs,
        out_specs=[out_spec] * n_out,
        out_shape=[out_shape] * n_out,
        compiler_params=_cparams(("parallel",)),
        name="dwa_decode",
    )(h3, *views)
    return [list(outs[4 * g:4 * g + 4]) for g in range(B_GROUPS)]


def _even_post_kernel(tt, yt_ref, bonus_ref, *refs):
    n_ga = A_WIDTH // LANES
    ga_refs = refs[:n_ga]
    gb0_ref, gb1_ref, lnxg_ref, lnxb_ref, ones_ref = refs[n_ga:n_ga + 5]
    dwa_refs = refs[n_ga + 5:n_ga + 5 + 4 * B_GROUPS]
    ua_ref, ub_ref = refs[n_ga + 5 + 4 * B_GROUPS:]
    ys = []
    for p in range(A_HEADS // 2):
        ys.append(yt_ref[0, p].T[0:tt, :])
    y = jnp.concatenate(ys, axis=-1)
    ones_bf = ones_ref[...]
    inv = 1.0 / A_HEAD_DIM
    mu = _seg_sum(y, ones_bf) * inv
    yc = y - mu
    var = _seg_sum(yc * yc, ones_bf) * inv
    ya = yc * lax.rsqrt(var + A_GN_EPS) * lnxg_ref[...] + lnxb_ref[...] + bonus_ref[0]
    gate_a = jnp.concatenate([g_ref[0] for g_ref in ga_refs], axis=-1)
    ua_ref[0] = (ya * _silu(gate_a)).astype(BF16)
    ybs = []
    for hp, gb_ref in enumerate((gb0_ref, gb1_ref)):
        o0, l0, o1, l1, o2, l2 = (dwa_refs[4 * g + 2 * hp + k][0] for g in range(B_GROUPS) for k in range(2))
        m = jnp.maximum(jnp.maximum(l0, l1), l2)
        e0, e1, e2 = jnp.exp(l0 - m), jnp.exp(l1 - m), jnp.exp(l2 - m)
        den = e0 + e1 + e2
        yb = (e0 / den) * o0 + (e1 / den) * o1 + (e2 / den) * o2
        ybs.append(yb * _silu(gb_ref[0]))
    ub_ref[0] = jnp.concatenate(ybs, axis=-1).astype(BF16)


def even_post(yt, bonus, h3, lnx_g, lnx_b, head_ones, dwa):
    bsz, t, _ = h3.shape
    tt = min(t, 128)
    tp = max(tt, LANES)
    nt = t // tt
    dwa_flat = [a for grp in dwa for a in grp]
    bspec = pl.BlockSpec((1, tt, B_OUT), lambda b, i: (b, i, 0))
    pspec = pl.BlockSpec((1, tt, LANES), lambda b, i: (b, i, 0))
    row_spec = pl.BlockSpec((1, A_WIDTH), lambda b, i: (0, 0))
    return pl.pallas_call(
        functools.partial(_even_post_kernel, tt),
        grid=(bsz, nt),
        in_specs=[pl.BlockSpec((1, A_HEADS // 2, LANES, tp), lambda b, i: (b, 0, 0, i)),
                  pl.BlockSpec((1, tt, A_WIDTH), lambda b, i: (b, i, 0)),
                  *[pl.BlockSpec((1, tt, LANES), functools.partial(lambda b, i, u: (b, i, u), u=U_GATE_A + u))
                    for u in range(A_WIDTH // LANES)],
                  pl.BlockSpec((1, tt, LANES), lambda b, i: (b, i, U_GATE_B)),
                  pl.BlockSpec((1, tt, LANES), lambda b, i: (b, i, U_GATE_B + 1)),
                  row_spec, row_spec,
                  pl.BlockSpec((LANES, LANES), lambda b, i: (0, 0))] + [pspec] * len(dwa_flat),
        out_specs=[pl.BlockSpec((1, tt, A_WIDTH), lambda b, i: (b, i, 0)), bspec],
        out_shape=[jax.ShapeDtypeStruct((bsz, t, A_WIDTH), BF16), jax.ShapeDtypeStruct((bsz, t, B_OUT), BF16)],
        compiler_params=_cparams(("parallel", "parallel")),
        name="even_post",
    )(yt, bonus, *([h3] * (A_WIDTH // LANES + 2)), lnx_g.reshape(1, -1), lnx_b.reshape(1, -1), head_ones, *dwa_flat)


def _mem_attn_kernel(*refs):
    q_refs, g_refs = refs[:M_HEADS], refs[M_HEADS:2 * M_HEADS]
    kv_refs, u_ref = refs[2 * M_HEADS:-1], refs[-1]
    dn = (((1,), (1,)), ((), ()))
    hds = range(M_HEADS)
    if len(kv_refs) == 1:
        ks = [kv_refs[0][0, 0, :, 0, hd, :] for hd in hds]
        vs = [kv_refs[0][0, 0, :, 1, hd, :] for hd in hds]
    else:
        ks = [kv_refs[0][0][:, hd * M_HEAD_DIM:(hd + 1) * M_HEAD_DIM] for hd in hds]
        vs = [kv_refs[1][0][:, hd * M_HEAD_DIM:(hd + 1) * M_HEAD_DIM] for hd in hds]
    ss = [lax.dot_general(q_refs[hd][0].astype(BF16), ks[hd].astype(BF16), dn,
                          preferred_element_type=F32) * (M_HEAD_DIM ** -0.5) for hd in hds]
    ms = [jnp.max(s, axis=-1, keepdims=True) for s in ss]
    es = [jnp.exp(s - m) for s, m in zip(ss, ms)]
    ps = [e / jnp.sum(e, axis=-1, keepdims=True) for e in es]
    os_ = [jnp.dot(ps[hd].astype(BF16), vs[hd].astype(BF16), preferred_element_type=F32) for hd in hds]
    u_ref[0] = jnp.concatenate([os_[hd] * _silu(g_refs[hd][0]) for hd in hds], axis=-1).astype(BF16)


def mem_attn(h3, mkv, layer, u_q, u_g):
    bsz, t, _ = h3.shape
    tq = min(t, 256)

    def col_spec(u):
        return pl.BlockSpec((1, tq, LANES), lambda b, i: (b, i, u))

    if layer is None:
        kv_specs = [pl.BlockSpec((1, M_TOKENS, M_WIDTH), lambda b, i: (b, 0, 0)),
                    pl.BlockSpec((1, M_TOKENS, M_WIDTH), lambda b, i: (b, 0, 1))]
    else:
        kv_specs = [pl.BlockSpec((1, 1, M_TOKENS, 2, M_HEADS, M_HEAD_DIM), lambda b, i: (layer, b, 0, 0, 0, 0))]
    return pl.pallas_call(
        _mem_attn_kernel,
        grid=(bsz, t // tq),
        in_specs=[col_spec(u_q + hd) for hd in range(M_HEADS)] + [col_spec(u_g + hd) for hd in range(M_HEADS)]
        + kv_specs,
        out_specs=pl.BlockSpec((1, tq, M_WIDTH), lambda b, i: (b, i, 0)),
        out_shape=jax.ShapeDtypeStruct((bsz, t, M_WIDTH), BF16),
        compiler_params=_cparams(("parallel", "parallel")),
        name="mem_attn",
    )(*([h3] * (2 * M_HEADS)), *([mkv] * len(kv_specs)))


def _rope_kernel(pos0, ang_ref, cos_ref, sin_ref):
    rows = cos_ref.shape[0]
    base = pl.program_id(0) * rows
    pos = (lax.broadcasted_iota(jnp.int32, cos_ref.shape, 0) + base).astype(F32) + pos0
    ph = pos * ang_ref[...]
    cos_ref[...] = jnp.cos(ph)
    sin_ref[...] = jnp.sin(ph)


def rope_tables(rows, pos0):
    angle = 1.0 / (C_ROT_BASE ** jnp.linspace(0.0, 1.0, C_HEAD_DIM // 2, dtype=F32))
    ang = jnp.repeat(angle, 2).reshape(1, C_HEAD_DIM)
    tr = min(rows, 512)
    spec = pl.BlockSpec((tr, C_HEAD_DIM), lambda i: (i, 0))
    shape = jax.ShapeDtypeStruct((rows, C_HEAD_DIM), F32)
    return pl.pallas_call(
        functools.partial(_rope_kernel, float(pos0)),
        grid=(rows // tr,),
        in_specs=[pl.BlockSpec((1, C_HEAD_DIM), lambda i: (0, 0))],
        out_specs=[spec, spec],
        out_shape=[shape, shape],
        compiler_params=_cparams(("parallel",)),
        name="rope_tables",
    )(ang)


def _rot_pairs(z):
    even = lax.broadcasted_iota(jnp.int32, (z.shape[0], LANES), 1) % 2 == 0
    parts = []
    for blk in range(z.shape[1] // LANES):
        zb = z[:, blk * LANES:(blk + 1) * LANES]
        nxt = pltpu.roll(zb, LANES - 1, axis=1)
        prv = pltpu.roll(zb, 1, axis=1)
        parts.append(jnp.where(even, -nxt, prv))
    return jnp.concatenate(parts, axis=-1)


def _retention_kernel(tb, chunk, lg_ref, q_ref, k_ref, v_ref, g_ref, cos_ref, sin_ref, r0_ref,
                      u_ref, rout_ref, r_scr, *pad):
    c = pl.program_id(2)

    @pl.when(c == 0)
    def _():
        r_scr[...] = r0_ref[0]

    if tb == C_CHUNK:
        q2, k2, v2 = q_ref[0], k_ref[0], v_ref[0]
    else:
        vals = []
        for src, buf in zip((q_ref, k_ref, v_ref), pad):
            buf[...] = jnp.zeros_like(buf)
            buf[0:tb, :] = src[0]
            vals.append(buf[...])
        q2, k2, v2 = vals
    cos, sin = cos_ref[...], sin_ref[...]
    ii = lax.broadcasted_iota(jnp.int32, (C_CHUNK, C_CHUNK), 0)
    jj = lax.broadcasted_iota(jnp.int32, (C_CHUNK, C_CHUNK), 1)
    diff = (ii - jj).astype(F32)
    idx = lax.broadcasted_iota(jnp.int32, (C_CHUNK, 1), 0).astype(F32)
    dn = (((1,), (1,)), ((), ()))
    hh_r = range(C_HEADS_PER_STEP)
    sls = [slice(hh * C_HEAD_DIM, (hh + 1) * C_HEAD_DIM) for hh in hh_r]
    lgs = [lg_ref[pl.program_id(1) * C_HEADS_PER_STEP + hh] for hh in hh_r]
    qrs = [q2[:, sl] * cos + _rot_pairs(q2[:, sl]) * sin for sl in sls]
    krs = [(k2[:, sl] * cos + _rot_pairs(k2[:, sl]) * sin) * (C_HEAD_DIM ** -0.5) for sl in sls]
    qbs = [qr.astype(BF16) for qr in qrs]
    vbs = [v2[:, sl].astype(BF16) for sl in sls]
    scs = [lax.dot_general(qb, kr.astype(BF16), dn, preferred_element_type=F32)
           * jnp.where(diff >= 0, jnp.exp(lg * jnp.maximum(diff, 0.0)), 0.0) for qb, kr, lg in zip(qbs, krs, lgs)]
    r_olds = [r_scr[hh] for hh in hh_r]
    kz_ts = [(kr * jnp.exp(lg * (chunk - 1.0 - idx))).T.astype(BF16) for kr, lg in zip(krs, lgs)]
    os_ = [jnp.dot(sc.astype(BF16), vb, preferred_element_type=F32)
           + jnp.dot(qb, r_old.astype(BF16), preferred_element_type=F32) * jnp.exp(lg * (idx + 1.0))
           for sc, vb, qb, r_old, lg in zip(scs, vbs, qbs, r_olds, lgs)]
    for hh in hh_r:
        r_scr[hh] = (r_olds[hh] * jnp.exp(lgs[hh] * jnp.full((1, 1), chunk, F32))
                     + jnp.dot(kz_ts[hh], vbs[hh], preferred_element_type=F32))
    us = []
    for hh in hh_r:
        o = os_[hh][0:tb]
        y = o * lax.rsqrt(jnp.mean(o * o, axis=-1, keepdims=True) + C_NORM_EPS)
        us.append(y * _silu(g_ref[0][:, sls[hh]]))
    u_ref[0] = jnp.concatenate(us, axis=-1).astype(BF16)

    @pl.when(c == pl.num_programs(2) - 1)
    def _():
        rout_ref[0] = r_scr[...]


def retention(h3, cos, sin, r0):
    bsz, t, _ = h3.shape
    tb = min(t, C_CHUNK)
    nc = t // tb
    chunk = float(tb)
    lg = jnp.log(1.0 - 2.0 ** (-5.0 - jnp.arange(C_HEADS, dtype=F32)))
    hps = C_HEADS_PER_STEP
    width = hps * C_HEAD_DIM
    nq = C_WIDTH // width

    def col_spec(off):
        return pl.BlockSpec((1, tb, width), lambda b, hd, c: (b, c, off + hd))

    tab_spec = pl.BlockSpec((C_CHUNK, C_HEAD_DIM), lambda b, hd, c: (c, 0))
    st_spec = pl.BlockSpec((1, hps, C_HEAD_DIM, C_HEAD_DIM), lambda b, hd, c: (b, hd, 0, 0))
    scratch = [pltpu.VMEM((hps, C_HEAD_DIM, C_HEAD_DIM), F32)]
    if tb != C_CHUNK:
        scratch += [pltpu.VMEM((C_CHUNK, width), F32)] * 3
    return pl.pallas_call(
        functools.partial(_retention_kernel, tb, chunk),
        grid=(bsz, C_HEADS // hps, nc),
        in_specs=[pl.BlockSpec(memory_space=pltpu.SMEM),
                  col_spec(0), col_spec(nq), col_spec(2 * nq), col_spec(3 * nq),
                  tab_spec, tab_spec, st_spec],
        out_specs=[pl.BlockSpec((1, tb, width), lambda b, hd, c: (b, c, hd)), st_spec],
        out_shape=[jax.ShapeDtypeStruct((bsz, t, C_WIDTH), BF16), jax.ShapeDtypeStruct(r0.shape, F32)],
        scratch_shapes=scratch,
        compiler_params=_cparams(("parallel", "parallel", "arbitrary")),
        name="retention",
    )(lg, h3, h3, h3, h3, cos, sin, r0)


def _even_layer(x, x_bf, mkv, dwa_bufs, s0, shift0, w_in_bf, w_out_bf, e, ln_g, ln_b, rw, head_ones):
    bsz, t, d = x.shape
    h = matmul(x_bf.reshape(bsz * t, d), w_in_bf, e)
    h3 = h.reshape(bsz, t, NU * LANES)
    mu, w0, wup, a0, aup, k_k, k_a, r_k, lnx_g, lnx_b = rw
    r, dcy, k2, kk, bb, vt, bonus = rwkv_prep(h3, shift0, (mu, w0, wup, a0, aup, k_k, k_a, r_k), head_ones)
    yt, s_new = rwkv_scan((r, dcy, k2, kk, bb), vt, s0)
    if dwa_bufs is None:
        dwa = [dwa_prompt(h3, g) for g in range(B_GROUPS)]
    else:
        dwa = dwa_decode(h3, dwa_bufs, e)
    u_a, u_b = even_post(yt, bonus, h3, lnx_g, lnx_b, head_ones, dwa)
    u_m = mem_attn(h3, mkv[0], mkv[1], U_QM_EVEN, U_GM_EVEN)
    x2, x2_bf = outproj_ln([u_a.reshape(bsz * t, -1), u_b.reshape(bsz * t, -1), u_m.reshape(bsz * t, -1)],
                           w_out_bf, e, x.reshape(bsz * t, d), ln_g, ln_b)
    rows = []
    for g, (win, _) in enumerate(B_CONFIGS):
        keep = t if dwa_bufs is not None else min(win, t)
        kg = h3[:, t - keep:, U_KB * LANES + g * B_OUT:U_KB * LANES + (g + 1) * B_OUT]
        vg = h3[:, t - keep:, U_VB * LANES + g * B_OUT:U_VB * LANES + (g + 1) * B_OUT]
        rows.append(jnp.stack([kg.reshape(bsz, keep, B_HEADS_PER_GROUP, B_HEAD_DIM),
                               vg.reshape(bsz, keep, B_HEADS_PER_GROUP, B_HEAD_DIM)], axis=2))
    return x2.reshape(bsz, t, d), x2_bf.reshape(bsz, t, d), s_new, h3[:, t - 1, :A_SHIFT_W], rows


def _odd_layer(x, x_bf, mkv, r0, tabs, w_in_bf, w_out_bf, o, ln_g, ln_b):
    bsz, t, d = x.shape
    h = matmul(x_bf.reshape(bsz * t, d), w_in_bf, o)
    h3 = h.reshape(bsz, t, NU * LANES)
    u_c, r_new = retention(h3, tabs[0], tabs[1], r0)
    u_m = mem_attn(h3, mkv[0], mkv[1], U_QM_ODD, U_GM_ODD)
    x2, x2_bf = outproj_ln([u_c.reshape(bsz * t, -1), u_m.reshape(bsz * t, -1)],
                           w_out_bf, o, x.reshape(bsz * t, d), ln_g, ln_b)
    return x2.reshape(bsz, t, d), x2_bf.reshape(bsz, t, d), r_new


def kernel(x_prompt, x_sample, state_rwkv, state_rwkv_shift, cache_dwa_g0, cache_dwa_g1, cache_dwa_g2, state_ret, cache_mem_kv, mem_prompt, w_in_even, w_out_even, w_in_odd, w_out_odd, w_mem_kv, ln_g, ln_b, rwkv_mu, rwkv_w0, rwkv_w_up, rwkv_a0, rwkv_a_up, rwkv_k_k, rwkv_k_a, rwkv_r_k, rwkv_lnx_g, rwkv_lnx_b):
    xp, xs = x_prompt, x_sample
    xp_bf, xs_bf = xp.astype(BF16), xs.astype(BF16)
    bp, tp_len, d = xp.shape
    bs, ts_len, _ = xs.shape
    dwa_cache = (cache_dwa_g0, cache_dwa_g1, cache_dwa_g2)
    head_ones = _block_ones(LANES, A_HEAD_DIM)
    mem_bf = mem_prompt.reshape(bp * M_TOKENS, d).astype(BF16)
    tabs_p = rope_tables(max(tp_len, C_CHUNK), 0)
    tabs_s = rope_tables(max(ts_len, C_CHUNK), PAST_LEN)
    rwkv_p, rwkv_s, shift_p, shift_s, ret_p, ret_s, mem_p = [], [], [], [], [], [], []
    dwa_p = [[] for _ in B_CONFIGS]
    dwa_s = [[] for _ in B_CONFIGS]
    w_in_even_bf = jnp.concatenate(
        [w_in_even.astype(BF16), jnp.zeros(w_in_even.shape[:2] + (EVEN_IN_PAD - EVEN_IN,), BF16)], axis=-1)
    w_out_even_bf = w_out_even.astype(BF16)
    w_in_odd_bf = w_in_odd
    w_out_odd_bf = w_out_odd.astype(BF16)
    w_mem_bf = w_mem_kv
    for l in range(DEPTH):
        mkv_new = matmul(mem_bf, w_mem_bf, l).reshape(bp, M_TOKENS, 2 * M_WIDTH)
        mem_p.append(mkv_new.reshape(bp, M_TOKENS, 2, M_HEADS, M_HEAD_DIM))
        mkv_p = (mkv_new, None)
        mkv_s = (cache_mem_kv, l)
        if l % 2 == 0:
            e = l // 2
            w_in_bf, w_out_bf = w_in_even_bf, w_out_even_bf
            rw = (rwkv_mu[e], rwkv_w0[e], rwkv_w_up[e], rwkv_a0[e], rwkv_a_up[e], rwkv_k_k[e], rwkv_k_a[e],
                  rwkv_r_k[e], rwkv_lnx_g[e], rwkv_lnx_b[e])
            s0 = jnp.zeros((bp, A_HEADS, A_HEAD_DIM, A_HEAD_DIM), F32)
            sh0 = jnp.zeros((bp, A_SHIFT_W), F32)
            xp, xp_bf, st, sh, rows = _even_layer(xp, xp_bf, mkv_p, None, s0, sh0, w_in_bf, w_out_bf, e,
                                                  ln_g[l], ln_b[l], rw, head_ones)
            rwkv_p.append(st)
            shift_p.append(sh)
            for g in range(B_GROUPS):
                dwa_p[g].append(rows[g])
            bufs = dwa_cache
            xs, xs_bf, st, sh, rows = _even_layer(xs, xs_bf, mkv_s, bufs, state_rwkv[e], state_rwkv_shift[e],
                                                  w_in_bf, w_out_bf, e, ln_g[l], ln_b[l], rw, head_ones)
            rwkv_s.append(st)
            shift_s.append(sh)
            for g in range(B_GROUPS):
                dwa_s[g].append(rows[g])
        else:
            o = l // 2
            r0 = jnp.zeros((bp, C_HEADS, C_HEAD_DIM, C_HEAD_DIM), F32)
            xp, xp_bf, st = _odd_layer(xp, xp_bf, mkv_p, r0, tabs_p, w_in_odd_bf, w_out_odd_bf, o, ln_g[l], ln_b[l])
            ret_p.append(st)
            xs, xs_bf, st = _odd_layer(xs, xs_bf, mkv_s, state_ret[o], tabs_s, w_in_odd_bf, w_out_odd_bf, o,
                                       ln_g[l], ln_b[l])
            ret_s.append(st)
    return (xp, xs, jnp.stack(rwkv_p), jnp.stack(rwkv_s), jnp.stack(shift_p), jnp.stack(shift_s),
            jnp.stack(dwa_p[0]), jnp.stack(dwa_s[0]), jnp.stack(dwa_p[1]), jnp.stack(dwa_s[1]),
            jnp.stack(dwa_p[2]), jnp.stack(dwa_s[2]), jnp.stack(ret_p), jnp.stack(ret_s), jnp.stack(mem_p))
```

```python
import functools

import numpy as np
import jax
import jax.numpy as jnp
from jax import lax
from jax.experimental import pallas as pl
from jax.experimental.pallas import tpu as pltpu

F32 = jnp.float32
BF16 = jnp.bfloat16

D_MODEL = 2048
DEPTH = 4
PAST_LEN = 16384
ALPHA = (2 * DEPTH) ** 0.25
LN_EPS = 1e-5
A_HEADS = 12
A_HEAD_DIM = 64
A_WIDTH = A_HEADS * A_HEAD_DIM
A_LORA = 64
A_SHIFT_W = 3 * A_WIDTH + 2 * A_LORA
A_GN_EPS = 64e-5
B_CONFIGS = ((128, 1), (512, 4), (2048, 16))
B_GROUPS = 3
B_HEADS_PER_GROUP = 4
B_HEAD_DIM = 64
B_OUT = B_HEADS_PER_GROUP * B_HEAD_DIM
B_WIDTH = B_GROUPS * B_OUT
B_BLK = 128
B_RES_PER_ITER = 4
C_HEADS = 6
C_HEAD_DIM = 256
C_WIDTH = C_HEADS * C_HEAD_DIM
C_CHUNK = 128
C_HEADS_PER_STEP = 6
C_ROT_BASE = 10000.0
C_NORM_EPS = 1e-6
M_TOKENS = 256
M_HEADS = 4
M_HEAD_DIM = 128
M_WIDTH = M_HEADS * M_HEAD_DIM
EVEN_IN = A_SHIFT_W + A_WIDTH + 3 * B_WIDTH + B_OUT + 2 * M_WIDTH
ODD_IN = 4 * C_WIDTH + 2 * M_WIDTH

LANES = 128
SUBLANES = 8
VMEM_LIMIT = 48 * 1024 * 1024

EVEN_IN_PAD = 7168
U_GATE_A = A_SHIFT_W // LANES
U_QB = U_GATE_A + A_WIDTH // LANES
U_KB = U_QB + B_WIDTH // LANES
U_VB = U_KB + B_WIDTH // LANES
U_GATE_B = U_VB + B_WIDTH // LANES
U_QM_EVEN = U_GATE_B + B_OUT // LANES
U_GM_EVEN = U_QM_EVEN + M_WIDTH // LANES
U_QM_ODD = 4 * C_WIDTH // LANES
U_GM_ODD = U_QM_ODD + M_WIDTH // LANES
NU = EVEN_IN_PAD // LANES


def _cparams(sem):
    return pltpu.CompilerParams(dimension_semantics=sem, vmem_limit_bytes=VMEM_LIMIT)


def _sigmoid(z):
    return 1.0 / (1.0 + jnp.exp(-z))


def _silu(z):
    return z * _sigmoid(z)


def _mm_kernel(x_ref, w_ref, o_ref, w_scr):
    @pl.when(pl.program_id(1) == 0)
    def _():
        w_scr[...] = w_ref[0].astype(BF16)

    o_ref[...] = jnp.dot(x_ref[...], w_scr[...], preferred_element_type=F32)


def matmul(x, w, layer):
    m, k = x.shape
    n = w.shape[2]
    tm = min(m, 1024)
    tn = 1024 if n % 1024 == 0 else 512
    return pl.pallas_call(
        _mm_kernel,
        grid=(n // tn, m // tm),
        in_specs=[pl.BlockSpec((tm, k), lambda j, i: (i, 0)),
                  pl.BlockSpec((1, k, tn), lambda j, i: (layer, 0, j))],
        out_specs=pl.BlockSpec((tm, tn), lambda j, i: (i, j)),
        out_shape=jax.ShapeDtypeStruct((m, n), F32),
        scratch_shapes=[pltpu.VMEM((k, tn), BF16)],
        compiler_params=_cparams(("parallel", "arbitrary")),
        name="matmul",
    )(x, w)


def _outproj_kernel(n_u, *refs):
    u_refs = refs[:n_u]
    w_refs = refs[n_u:2 * n_u]
    x_ref, g_ref, b_ref, o_ref, obf_ref = refs[2 * n_u:]
    acc = jnp.dot(u_refs[0][...], w_refs[0][0], preferred_element_type=F32)
    for u_ref, w_ref in zip(u_refs[1:], w_refs[1:]):
        acc = acc + jnp.dot(u_ref[...], w_ref[0], preferred_element_type=F32)
    z = ALPHA * x_ref[...] + acc
    mu = jnp.mean(z, axis=-1, keepdims=True)
    zc = z - mu
    var = jnp.mean(zc * zc, axis=-1, keepdims=True)
    y = zc * lax.rsqrt(var + LN_EPS) * g_ref[...] + b_ref[...]
    o_ref[...] = y
    obf_ref[...] = y.astype(BF16)


def outproj_ln(us, w_out, layer, x, g, b):
    m, d = x.shape
    tm = min(m, 512)
    n_u = len(us)
    in_specs = [pl.BlockSpec((tm, u.shape[1]), lambda i: (i, 0)) for u in us]
    row = 0
    for u in us:
        kw = u.shape[1]
        assert row % kw == 0
        in_specs.append(pl.BlockSpec((1, kw, d), functools.partial(lambda i, r: (layer, r, 0), r=row // kw)))
        row += kw
    assert row == w_out.shape[1]
    in_specs += [pl.BlockSpec((tm, d), lambda i: (i, 0)),
                 pl.BlockSpec((1, d), lambda i: (0, 0)),
                 pl.BlockSpec((1, d), lambda i: (0, 0))]
    return pl.pallas_call(
        functools.partial(_outproj_kernel, n_u),
        grid=(m // tm,),
        in_specs=in_specs,
        out_specs=[pl.BlockSpec((tm, d), lambda i: (i, 0)), pl.BlockSpec((tm, d), lambda i: (i, 0))],
        out_shape=[jax.ShapeDtypeStruct((m, d), F32), jax.ShapeDtypeStruct((m, d), BF16)],
        compiler_params=_cparams(("parallel",)),
        name="outproj_ln",
    )(*us, *([w_out] * n_u), x, g.reshape(1, d), b.reshape(1, d))


def _seg_sum(x, ones_bf):
    hi = x.astype(BF16)
    r1 = x - hi.astype(F32)
    mid = r1.astype(BF16)
    lo = (r1 - mid.astype(F32)).astype(BF16)
    outs = []
    for p in range(x.shape[1] // LANES):
        sl = slice(p * LANES, (p + 1) * LANES)
        terms = jnp.concatenate([hi[:, sl], mid[:, sl], lo[:, sl]], axis=1)
        outs.append(jnp.dot(terms, jnp.concatenate([ones_bf] * 3, axis=0), preferred_element_type=F32))
    return jnp.concatenate(outs, axis=1)


def _block_ones(width, seg):
    i = np.arange(width) // seg
    return jnp.asarray((i[:, None] == i[None, :]).astype(np.float32), dtype=BF16)


def _rwkv_prep_kernel(tt, tp, h_ref, prev_ref, sh0_ref, mu_ref, w0_ref, wup_ref, a0_ref, aup_ref,
                      kk_ref_p, ka_ref_p, rk_ref_p, ones_ref,
                      r_o, d_o, k_o, kk_o, b_o, vt_o, bonus_o, *scratch):
    i = pl.program_id(1)
    x = h_ref[0]
    pr = prev_ref[0]
    prev_last = pr[pr.shape[0] - 1:pr.shape[0], :]
    first = jnp.where(i == 0, sh0_ref[0], prev_last)
    row = lax.broadcasted_iota(jnp.int32, x.shape, 0)
    if tt % SUBLANES == 0:
        rolled = pltpu.roll(x, 1, axis=0)
    else:
        rolled = jnp.concatenate([x[tt - 1:tt], x[:tt - 1]], axis=0)
    prev = jnp.where(row == 0, first, rolled)
    hs = x + (prev - x) * mu_ref[...]
    r = hs[:, 0:A_WIDTH]
    k = hs[:, A_WIDTH:2 * A_WIDTH]
    v = hs[:, 2 * A_WIDTH:3 * A_WIDTH]
    hw = hs[:, 3 * A_WIDTH:3 * A_WIDTH + A_LORA]
    ha = hs[:, 3 * A_WIDTH + A_LORA:A_SHIFT_W]
    zw = w0_ref[...] + jnp.dot(jnp.tanh(hw).astype(BF16), wup_ref[...], preferred_element_type=F32)
    nz = -zw
    softplus = jnp.maximum(nz, 0.0) + jnp.log(1.0 + jnp.exp(-jnp.abs(nz)))
    w_log = -softplus - 0.5
    neg_log_decay = jnp.exp(w_log)
    a = _sigmoid(a0_ref[...] + jnp.dot(ha.astype(BF16), aup_ref[...], preferred_element_type=F32))
    ones_bf = ones_ref[...]
    kk = k * kk_ref_p[...]
    kk = kk * lax.rsqrt(jnp.maximum(_seg_sum(kk * kk, ones_bf), 1e-24))
    k2 = k * (1.0 + (a - 1.0) * ka_ref_p[...])
    bb = kk * a
    bonus_o[0] = _seg_sum(r * k2 * rk_ref_p[...], ones_bf) * v
    width = r_o.shape[3]
    for s in range(A_WIDTH // width):
        sl = slice(s * width, (s + 1) * width)
        r_o[0, s] = r[:, sl]
        d_o[0, s] = neg_log_decay[:, sl]
        k_o[0, s] = k2[:, sl]
        kk_o[0, s] = kk[:, sl]
        b_o[0, s] = bb[:, sl]
    for p in range(A_HEADS // 2):
        vp = v[:, p * LANES:(p + 1) * LANES]
        if tt == tp:
            vt_o[0, p] = vp.T
        else:
            pad = scratch[0]
            pad[...] = jnp.zeros_like(pad)
            pad[0:tt, :] = vp
            vt_o[0, p] = pad[...].T


def rwkv_prep(h3, shift0, prm, head_ones):
    bsz, t, _ = h3.shape
    tt = min(t, 128)
    tp = max(tt, LANES)
    nt = t // tt
    pr_rows = min(t, SUBLANES)
    pb = tt // pr_rows
    mu, w0, wup, a0, aup, k_k, k_a, r_k = prm
    row_spec = pl.BlockSpec((1, A_WIDTH), lambda b, i: (0, 0))
    lora_spec = pl.BlockSpec((A_LORA, A_WIDTH), lambda b, i: (0, 0))
    width = LANES if t % A_CHUNK == 0 else A_HEAD_DIM
    head_spec = pl.BlockSpec((1, A_WIDTH // width, tt, width), lambda b, i: (b, 0, i, 0))
    head_shape = jax.ShapeDtypeStruct((bsz, A_WIDTH // width, t, width), F32)
    scratch = [] if tt == tp else [pltpu.VMEM((tp, LANES), F32)]
    return pl.pallas_call(
        functools.partial(_rwkv_prep_kernel, tt, tp),
        grid=(bsz, nt),
        in_specs=[pl.BlockSpec((1, tt, A_SHIFT_W), lambda b, i: (b, i, 0)),
                  pl.BlockSpec((1, pr_rows, A_SHIFT_W), lambda b, i: (b, jnp.maximum(i * pb - 1, 0), 0)),
                  pl.BlockSpec((1, 1, A_SHIFT_W), lambda b, i: (b, 0, 0)),
                  pl.BlockSpec((1, A_SHIFT_W), lambda b, i: (0, 0)),
                  row_spec, lora_spec, row_spec, lora_spec, row_spec, row_spec, row_spec,
                  pl.BlockSpec((LANES, LANES), lambda b, i: (0, 0))],
        out_specs=[head_spec] * 5 + [
            pl.BlockSpec((1, A_HEADS // 2, LANES, tp), lambda b, i: (b, 0, 0, i)),
            pl.BlockSpec((1, tt, A_WIDTH), lambda b, i: (b, i, 0))],
        out_shape=[head_shape] * 5 + [
            jax.ShapeDtypeStruct((bsz, A_HEADS // 2, LANES, nt * tp), F32),
            jax.ShapeDtypeStruct((bsz, t, A_WIDTH), F32)],
        scratch_shapes=scratch,
        compiler_params=_cparams(("parallel", "parallel")),
        name="rwkv_prep",
    )(h3, h3, shift0.reshape(bsz, 1, A_SHIFT_W), mu.reshape(1, -1), w0.reshape(1, -1), wup.astype(BF16),
      a0.reshape(1, -1), aup.astype(BF16), k_k.reshape(1, -1), k_a.reshape(1, -1), r_k.reshape(1, -1), head_ones)


def _rwkv_scan_kernel(steps, r_ref, d_ref, k_ref, kk_ref, b_ref, vt_ref, s0_ref, yt_ref, sout_ref, s_scr, p_scr):
    c = pl.program_id(1)

    @pl.when(c == 0)
    def _():
        s_scr[...] = s0_ref[0]

    tp = vt_ref.shape[3]
    lane = lax.broadcasted_iota(jnp.int32, (A_HEAD_DIM, tp), 1)
    yt_ref[...] = jnp.zeros_like(yt_ref)
    p_scr[...] = jnp.zeros_like(p_scr)
    ones_bf = jnp.ones((A_HEAD_DIM, tp), BF16)

    def write_y(t):
        msk = lane == t
        for hd in range(A_HEADS):
            y = jnp.dot(p_scr[hd], ones_bf, preferred_element_type=F32)
            yt_ref[0, hd] = jnp.where(msk, y, yt_ref[0, hd])

    def body(t, carry):
        write_y(t - 1)
        msk = lane == t
        sas, vcols = [], []
        for hd in range(A_HEADS):
            sas.append(-jnp.sum(s_scr[hd] * kk_ref[0, hd, pl.ds(t, 1), :], axis=-1, keepdims=True))
            vcols.append(jnp.sum(jnp.where(msk, vt_ref[0, hd], 0.0), axis=-1, keepdims=True))
        for hd in range(A_HEADS):
            s = (s_scr[hd] * jnp.exp(-d_ref[0, hd, pl.ds(t, 1), :]) + sas[hd] * b_ref[0, hd, pl.ds(t, 1), :]
                 + vcols[hd] * k_ref[0, hd, pl.ds(t, 1), :])
            s_scr[hd] = s
            p_scr[hd] = (s * r_ref[0, hd, pl.ds(t, 1), :]).astype(BF16)
        return carry

    lax.fori_loop(0, steps, body, 0, unroll=4)
    write_y(steps - 1)

    @pl.when(c == pl.num_programs(1) - 1)
    def _():
        sout_ref[0] = s_scr[...]


A_CHUNK = 128
A_CHUNK_HEADS = 6
_NT = (((1,), (1,)), ((), ()))


def _split2(x):
    hi = x.astype(BF16)
    return hi, (x - hi.astype(F32)).astype(BF16)


def _mm3(a, b):
    ah, al = _split2(a)
    bh, bl = _split2(b)
    return jnp.dot(jnp.concatenate([ah, ah, al], axis=1), jnp.concatenate([bh, bl, bh], axis=0),
                   preferred_element_type=F32)


def _mm3_nt(a, b):
    ah, al = _split2(a)
    bh, bl = _split2(b)
    return lax.dot_general(jnp.concatenate([ah, ah, al], axis=1), jnp.concatenate([bh, bl, bh], axis=1), _NT,
                           preferred_element_type=F32)


def _inv_unit_lower(lbs, row, col):
    def blk(s):
        return (row // s) == (col // s)
    eye = jnp.where(row == col, 1.0, 0.0)
    ds = [jnp.where(blk(8), lb, 0.0) for lb in lbs]
    d2s = [_mm3(d, d) for d in ds]
    d4s = [_mm3(d2, d2) for d2 in d2s]
    xs = [eye + d for d in ds]
    xs = [x + _mm3(x, d2) for x, d2 in zip(xs, d2s)]
    xs = [x + _mm3(x, d4) for x, d4 in zip(xs, d4s)]
    s = 8
    while s < A_CHUNK:
        msk = blk(2 * s) & jnp.logical_not(blk(s))
        ts = [_mm3(x, jnp.where(msk, lb, 0.0)) for x, lb in zip(xs, lbs)]
        xs = [x + _mm3(t, x) for x, t in zip(xs, ts)]
        s *= 2
    return xs


def _rwkv_chunk_heads(es, rs, kks, bs, k2s, vs, sps, tril, row, col):
    n_tok, n = A_CHUNK, A_HEAD_DIM
    n_pair = len(es)
    idx = range(2 * n_pair)
    cs = []
    for e in es:
        e_hi = e.astype(BF16)
        e_r1 = e - e_hi.astype(F32)
        e_mid = e_r1.astype(BF16)
        e_lo = (e_r1 - e_mid.astype(F32)).astype(BF16)
        cs.append(jnp.dot(jnp.concatenate([tril, tril, tril], axis=1),
                          jnp.concatenate([e_hi, e_mid, e_lo], axis=0), preferred_element_type=F32))
    ms = [c[n_tok // 2 - 1:n_tok // 2] for c in cs]
    ccs = [c - m for c, m in zip(cs, ms)]
    gbs = [jnp.exp(cc) for cc in ccs]
    own = [lax.broadcasted_iota(jnp.int32, (n_tok, LANES), 1) < n]
    own.append(jnp.logical_not(own[0]))
    own_s = [lax.broadcasted_iota(jnp.int32, (n, LANES), 1) < n]
    own_s.append(jnp.logical_not(own_s[0]))

    def halves(xs, masks):
        return [jnp.where(masks[i % 2], xs[i // 2], 0.0) for i in idx]

    ats = halves([-kks[p] * jnp.exp(es[p] - ccs[p]) for p in range(n_pair)], own)
    bts = halves([bs[p] * gbs[p] for p in range(n_pair)], own)
    kts = halves([k2s[p] * gbs[p] for p in range(n_pair)], own)
    rts = halves([rs[p] * jnp.exp(-ccs[p]) for p in range(n_pair)], own)
    s0ps = halves([sps[p] * jnp.exp(-ms[p]) for p in range(n_pair)], own_s)
    bks = [jnp.concatenate([bts[i], kts[i]], axis=0) for i in idx]
    zs = [_mm3_nt(ats[i], bks[i]) for i in idx]
    m2s = [_mm3_nt(bks[i], rts[i]) for i in idx]
    low = row > col
    upp = row <= col
    lbs = [jnp.where(low, z[:, :n_tok], 0.0) for z in zs]
    lks = [jnp.where(low, z[:, n_tok:], 0.0) for z in zs]
    mbks = [jnp.concatenate([jnp.where(upp, m2[:n_tok], 0.0), jnp.where(upp, m2[n_tok:], 0.0)], axis=0)
            for m2 in m2s]
    ws = _inv_unit_lower(lbs, row, col)
    ptqs = [_mm3(ws[i], jnp.concatenate([ats[i], lks[i]], axis=1)) for i in idx]
    srps = [_mm3_nt(s0ps[i], jnp.concatenate([rts[i], ptqs[i][:, :LANES]], axis=0)) for i in idx]
    us = [srps[i][:, n_tok:] + _mm3_nt(vs[i], ptqs[i][:, LANES:]) for i in idx]
    yss = [_mm3(jnp.concatenate([us[i], vs[i]], axis=1), jnp.concatenate([mbks[i], bks[i]], axis=1)) for i in idx]
    ys = [srps[i][:, :n_tok] + yss[i][:, :n_tok] for i in idx]
    s1s = [(s0ps[2 * p] + yss[2 * p][:, n_tok:] + s0ps[2 * p + 1] + yss[2 * p + 1][:, n_tok:])
           * jnp.exp(-ccs[p][n_tok - 1:n_tok]) for p in range(n_pair)]
    return ys, s1s


def _rwkv_chunk_kernel(r_ref, e_ref, k_ref, kk_ref, b_ref, vt_ref, s0_ref, yt_ref, sout_ref, s_scr):
    c = pl.program_id(1)
    n_pair = A_HEADS // 2

    @pl.when(c == 0)
    def _():
        for p in range(n_pair):
            s_scr[p] = jnp.concatenate([s0_ref[0, 2 * p], s0_ref[0, 2 * p + 1]], axis=1)

    row = lax.broadcasted_iota(jnp.int32, (A_CHUNK, A_CHUNK), 0)
    col = lax.broadcasted_iota(jnp.int32, (A_CHUNK, A_CHUNK), 1)
    tril = jnp.where(row >= col, 1.0, 0.0).astype(BF16)
    pairs_per_group = A_CHUNK_HEADS // 2

    def group(gi, carry):
        prs = [gi * pairs_per_group + j for j in range(pairs_per_group)]
        hds = [2 * p + k for p in prs for k in range(2)]
        ys, s1s = _rwkv_chunk_heads([e_ref[0, p] for p in prs], [r_ref[0, p] for p in prs],
                                    [kk_ref[0, p] for p in prs], [b_ref[0, p] for p in prs],
                                    [k_ref[0, p] for p in prs], [vt_ref[0, hd] for hd in hds],
                                    [s_scr[p] for p in prs], tril, row, col)
        for hd, y in zip(hds, ys):
            yt_ref[0, hd] = y
        for p, s1 in zip(prs, s1s):
            s_scr[p] = s1
        return carry

    lax.fori_loop(0, n_pair // pairs_per_group, group, 0)

    @pl.when(c == pl.num_programs(1) - 1)
    def _():
        for p in range(n_pair):
            sp = s_scr[p]
            sout_ref[0, 2 * p] = sp[:, :A_HEAD_DIM]
            sout_ref[0, 2 * p + 1] = sp[:, A_HEAD_DIM:]


def rwkv_scan(ops, vt, s0):
    r, d, k, kk, b = ops
    bsz, n_slab, t, width = r.shape
    tt = min(t, 128)
    tp = max(tt, LANES)
    nt = t // tt
    vt = vt.reshape(bsz, A_HEADS, A_HEAD_DIM, nt * tp)
    chunked = t % A_CHUNK == 0
    assert width == (LANES if chunked else A_HEAD_DIM)
    head_spec = pl.BlockSpec((1, n_slab, tt, width), lambda b_, i: (b_, 0, i, 0))
    vt_spec = pl.BlockSpec((1, A_HEADS, A_HEAD_DIM, tp), lambda b_, i: (b_, 0, 0, i))
    st_spec = pl.BlockSpec((1, A_HEADS, A_HEAD_DIM, A_HEAD_DIM), lambda b_, i: (b_, 0, 0, 0))
    if chunked:
        scratch = [pltpu.VMEM((A_HEADS // 2, A_HEAD_DIM, LANES), F32)]
    else:
        scratch = [pltpu.VMEM((A_HEADS, A_HEAD_DIM, A_HEAD_DIM), F32),
                   pltpu.VMEM((A_HEADS, A_HEAD_DIM, A_HEAD_DIM), BF16)]
    yt, s_new = pl.pallas_call(
        _rwkv_chunk_kernel if chunked else functools.partial(_rwkv_scan_kernel, tt),
        grid=(bsz, nt),
        in_specs=[head_spec] * 5 + [vt_spec, st_spec],
        out_specs=[vt_spec, st_spec],
        out_shape=[jax.ShapeDtypeStruct(vt.shape, F32), jax.ShapeDtypeStruct(s0.shape, F32)],
        scratch_shapes=scratch,
        compiler_params=_cparams(("parallel", "arbitrary")),
        name="rwkv_chunk" if chunked else "rwkv_scan",
    )(r, d, k, kk, b, vt, s0)
    return yt.reshape(bsz, A_HEADS // 2, LANES, nt * tp), s_new


def _dwa_prompt_kernel(dil, *refs):
    npair = B_OUT // LANES
    in_refs = [refs[5 * hp:5 * hp + 5] for hp in range(npair)]
    out_refs = refs[5 * npair:]
    n = pl.program_id(1)
    i = lax.broadcasted_iota(jnp.int32, (B_BLK, B_BLK), 0)
    j = lax.broadcasted_iota(jnp.int32, (B_BLK, B_BLK), 1)
    mask_prev = j >= i + jnp.where(n > 0, 0, B_BLK)
    mask_cur = j <= i
    neg = -jnp.inf
    scale = B_HEAD_DIM ** -0.5
    dn = (((1,), (1,)), ((), ()))

    n_res = B_RES_PER_ITER
    n_hh = LANES // B_HEAD_DIM

    def residues(it, carry):
        if dil > 1:
            rows = [pl.ds(it * n_res + k, B_BLK, stride=dil) for k in range(n_res)]
            loaded = {(k, hp): [ref[0, rows[k], :] for ref in in_refs[hp]]
                      for k in range(n_res) for hp in range(npair)}
            prev_masks = [mask_prev] * n_res
        else:
            rows = [pl.ds(k * B_BLK, B_BLK) for k in range(n_res)]
            loaded = {}
            for hp in range(npair):
                q_ref, kc_ref, kp_ref, vc_ref, vp_ref = in_refs[hp]
                for k in range(n_res):
                    loaded[(k, hp)] = [q_ref[0, rows[k], :], kc_ref[0, rows[k], :],
                                       kp_ref[0] if k == 0 else kc_ref[0, rows[k - 1], :],
                                       vc_ref[0, rows[k], :],
                                       vp_ref[0] if k == 0 else vc_ref[0, rows[k - 1], :]]
            prev_masks = [mask_prev] + [j >= i] * (n_res - 1)
        units = [(k, hp, hh) for k in range(n_res) for hp in range(npair) for hh in range(n_hh)]

        def part(u, which):
            k, hp, hh = u
            return loaded[(k, hp)][which][:, hh * B_HEAD_DIM:(hh + 1) * B_HEAD_DIM].astype(BF16)

        qs = [part(u, 0) for u in units]
        sps = [jnp.where(prev_masks[u[0]], lax.dot_general(q, part(u, 2), dn, preferred_element_type=F32) * scale, neg)
               for q, u in zip(qs, units)]
        scs = [jnp.where(mask_cur, lax.dot_general(q, part(u, 1), dn, preferred_element_type=F32) * scale, neg)
               for q, u in zip(qs, units)]
        ms = [jnp.maximum(jnp.max(sp, axis=-1, keepdims=True), jnp.max(sc, axis=-1, keepdims=True))
              for sp, sc in zip(sps, scs)]
        eps = [jnp.exp(sp - m) for sp, m in zip(sps, ms)]
        ecs = [jnp.exp(sc - m) for sc, m in zip(scs, ms)]
        ls = [jnp.sum(ep, axis=-1, keepdims=True) + jnp.sum(ec, axis=-1, keepdims=True) for ep, ec in zip(eps, ecs)]
        os_ = [jnp.dot((ep / l).astype(BF16), part(u, 4), preferred_element_type=F32)
               + jnp.dot((ec / l).astype(BF16), part(u, 3), preferred_element_type=F32)
               for ep, ec, l, u in zip(eps, ecs, ls, units)]
        lses = [jnp.broadcast_to(m + jnp.log(l), (B_BLK, B_HEAD_DIM)) for m, l in zip(ms, ls)]
        for k in range(n_res):
            for hp in range(npair):
                sel = [i_u for i_u, u in enumerate(units) if u[0] == k and u[1] == hp]
                out_refs[2 * hp][0, rows[k], :] = jnp.concatenate([os_[i_u] for i_u in sel], axis=-1)
                out_refs[2 * hp + 1][0, rows[k], :] = jnp.concatenate([lses[i_u] for i_u in sel], axis=-1)
        return carry

    lax.fori_loop(0, max(dil // n_res, 1), residues, 0)


def dwa_prompt(h3, g):
    bsz, t, _ = h3.shape
    win, dil = B_CONFIGS[g]
    nw = B_RES_PER_ITER if dil == 1 else 1
    assert win == B_BLK * dil and t % (nw * win) == 0 and (dil == 1 or dil % B_RES_PER_ITER == 0)
    npair = B_OUT // LANES

    def in_spec(unit, prev, hp):
        def imap(b, n):
            nn = jnp.maximum(n * nw - 1, 0) if prev else n
            return (b, nn, unit + npair * g + hp)
        return pl.BlockSpec((1, win if prev else nw * win, LANES), imap)

    in_specs = []
    for hp in range(npair):
        in_specs += [in_spec(U_QB, False, hp), in_spec(U_KB, False, hp), in_spec(U_KB, True, hp),
                     in_spec(U_VB, False, hp), in_spec(U_VB, True, hp)]
    out_spec = pl.BlockSpec((1, nw * win, LANES), lambda b, n: (b, n, 0))
    out_shape = jax.ShapeDtypeStruct((bsz, t, LANES), F32)
    outs = pl.pallas_call(
        functools.partial(_dwa_prompt_kernel, dil),
        grid=(bsz, t // (nw * win)),
        in_specs=in_specs,
        out_specs=[out_spec] * (2 * npair),
        out_shape=[out_shape] * (2 * npair),
        compiler_params=_cparams(("parallel", "parallel")),
        name="dwa_prompt",
    )(*([h3] * len(in_specs)))
    return list(outs)


def _dwa_decode_kernel(t_new, h_ref, c0_ref, c1_ref, c2_ref, *out_refs):
    scale = B_HEAD_DIM ** -0.5
    dn = (((1,), (1,)), ((), ()))
    head_of_lane = lax.broadcasted_iota(jnp.int32, (SUBLANES, B_OUT), 1) // B_HEAD_DIM
    sub = lax.broadcasted_iota(jnp.int32, (SUBLANES, B_OUT), 0)
    own = head_of_lane == sub
    m_idx = lax.broadcasted_iota(jnp.int32, (SUBLANES, B_BLK), 1)
    neg = -jnp.inf
    hrow = h_ref[0]
    for g, c_ref in enumerate((c0_ref, c1_ref, c2_ref)):
        _, dil = B_CONFIGS[g]
        g_refs = out_refs[4 * g:4 * g + 4]
        qs = hrow[:, U_QB * LANES + g * B_OUT:U_QB * LANES + (g + 1) * B_OUT]
        ks = hrow[:, U_KB * LANES + g * B_OUT:U_KB * LANES + (g + 1) * B_OUT]
        vs = hrow[:, U_VB * LANES + g * B_OUT:U_VB * LANES + (g + 1) * B_OUT]
        for t in range(t_new):
            res = t % dil
            kbuf = jnp.concatenate([c_ref[0, 0, :, res, 0, hd, :] for hd in range(B_HEADS_PER_GROUP)], axis=-1)
            vbuf = jnp.concatenate([c_ref[0, 0, :, res, 1, hd, :] for hd in range(B_HEADS_PER_GROUP)], axis=-1)
            qbd = jnp.where(own, jnp.broadcast_to(qs[t:t + 1], (SUBLANES, B_OUT)), 0.0)
            s_buf = lax.dot_general(qbd.astype(BF16), kbuf.astype(BF16), dn, preferred_element_type=F32) * scale
            s_buf = jnp.where(m_idx * dil + res >= t, s_buf, neg)
            new_rows = [n for n in range(t + 1) if (t - n) % dil == 0]
            kq = qbd.astype(BF16).astype(F32)
            s_new = [jnp.sum(kq * ks[n:n + 1].astype(BF16).astype(F32), axis=-1, keepdims=True) * scale
                     for n in new_rows]
            m = jnp.max(s_buf, axis=-1, keepdims=True)
            for sn in s_new:
                m = jnp.maximum(m, sn)
            e_buf = jnp.exp(s_buf - m)
            e_new = [jnp.exp(sn - m) for sn in s_new]
            l = jnp.sum(e_buf, axis=-1, keepdims=True)
            for en in e_new:
                l = l + en
            o = jnp.dot((e_buf / l).astype(BF16), vbuf.astype(BF16), preferred_element_type=F32)
            for n, en in zip(new_rows, e_new):
                o = o + (en / l).astype(BF16).astype(F32) * vs[n:n + 1].astype(BF16).astype(F32)
            lse = jnp.broadcast_to(m + jnp.log(l), (SUBLANES, B_OUT))
            o_row = jnp.sum(jnp.where(own, o, 0.0), axis=0, keepdims=True)
            lse_row = jnp.sum(jnp.where(own, lse, 0.0), axis=0, keepdims=True)
            for hp in range(B_OUT // LANES):
                g_refs[2 * hp][0, t:t + 1, :] = o_row[:, hp * LANES:(hp + 1) * LANES]
                g_refs[2 * hp + 1][0, t:t + 1, :] = lse_row[:, hp * LANES:(hp + 1) * LANES]


def dwa_decode(h3, caches, e):
    bsz, t, _ = h3.shape
    views, specs = [], []
    for g, (win, dil) in enumerate(B_CONFIGS):
        assert caches[g].shape[2] == win and (dil == 1 or t <= dil)
        views.append(caches[g].reshape(caches[g].shape[0], bsz, win // dil, dil, 2, B_HEADS_PER_GROUP, B_HEAD_DIM))
        specs.append(pl.BlockSpec((1, 1, B_BLK, min(dil, t), 2, B_HEADS_PER_GROUP, B_HEAD_DIM),
                                  lambda b: (e, b, 0, 0, 0, 0, 0)))
    out_spec = pl.BlockSpec((1, t, LANES), lambda b: (b, 0, 0))
    out_shape = jax.ShapeDtypeStruct((bsz, t, LANES), F32)
    n_out = 4 * B_GROUPS
    outs = pl.pallas_call(
        functools.partial(_dwa_decode_kernel, t),
        grid=(bsz,),
        in_specs=[pl.BlockSpec((1, t, NU * LANES), lambda b: (b, 0, 0))] + specs,
        out_specs=[out_spec] * n_out,
        out_shape=[out_shape] * n_out,
        compiler_params=_cparams(("parallel",)),
        name="dwa_decode",
    )(h3, *views)
    return [list(outs[4 * g:4 * g + 4]) for g in range(B_GROUPS)]


def _even_post_kernel(tt, yt_ref, bonus_ref, *refs):
    n_ga = A_WIDTH // LANES
    ga_refs = refs[:n_ga]
    gb0_ref, gb1_ref, lnxg_ref, lnxb_ref, ones_ref = refs[n_ga:n_ga + 5]
    dwa_refs = refs[n_ga + 5:n_ga + 5 + 4 * B_GROUPS]
    ua_ref, ub_ref = refs[n_ga + 5 + 4 * B_GROUPS:]
    ys = []
    for p in range(A_HEADS // 2):
        ys.append(yt_ref[0, p].T[0:tt, :])
    y = jnp.concatenate(ys, axis=-1)
    ones_bf = ones_ref[...]
    inv = 1.0 / A_HEAD_DIM
    mu = _seg_sum(y, ones_bf) * inv
    yc = y - mu
    var = _seg_sum(yc * yc, ones_bf) * inv
    ya = yc * lax.rsqrt(var + A_GN_EPS) * lnxg_ref[...] + lnxb_ref[...] + bonus_ref[0]
    gate_a = jnp.concatenate([g_ref[0] for g_ref in ga_refs], axis=-1)
    ua_ref[0] = (ya * _silu(gate_a)).astype(BF16)
    ybs = []
    for hp, gb_ref in enumerate((gb0_ref, gb1_ref)):
        o0, l0, o1, l1, o2, l2 = (dwa_refs[4 * g + 2 * hp + k][0] for g in range(B_GROUPS) for k in range(2))
        m = jnp.maximum(jnp.maximum(l0, l1), l2)
        e0, e1, e2 = jnp.exp(l0 - m), jnp.exp(l1 - m), jnp.exp(l2 - m)
        den = e0 + e1 + e2
        yb = (e0 / den) * o0 + (e1 / den) * o1 + (e2 / den) * o2
        ybs.append(yb * _silu(gb_ref[0]))
    ub_ref[0] = jnp.concatenate(ybs, axis=-1).astype(BF16)


def even_post(yt, bonus, h3, lnx_g, lnx_b, head_ones, dwa):
    bsz, t, _ = h3.shape
    tt = min(t, 128)
    tp = max(tt, LANES)
    nt = t // tt
    dwa_flat = [a for grp in dwa for a in grp]
    bspec = pl.BlockSpec((1, tt, B_OUT), lambda b, i: (b, i, 0))
    pspec = pl.BlockSpec((1, tt, LANES), lambda b, i: (b, i, 0))
    row_spec = pl.BlockSpec((1, A_WIDTH), lambda b, i: (0, 0))
    return pl.pallas_call(
        functools.partial(_even_post_kernel, tt),
        grid=(bsz, nt),
        in_specs=[pl.BlockSpec((1, A_HEADS // 2, LANES, tp), lambda b, i: (b, 0, 0, i)),
                  pl.BlockSpec((1, tt, A_WIDTH), lambda b, i: (b, i, 0)),
                  *[pl.BlockSpec((1, tt, LANES), functools.partial(lambda b, i, u: (b, i, u), u=U_GATE_A + u))
                    for u in range(A_WIDTH // LANES)],
                  pl.BlockSpec((1, tt, LANES), lambda b, i: (b, i, U_GATE_B)),
                  pl.BlockSpec((1, tt, LANES), lambda b, i: (b, i, U_GATE_B + 1)),
                  row_spec, row_spec,
                  pl.BlockSpec((LANES, LANES), lambda b, i: (0, 0))] + [pspec] * len(dwa_flat),
        out_specs=[pl.BlockSpec((1, tt, A_WIDTH), lambda b, i: (b, i, 0)), bspec],
        out_shape=[jax.ShapeDtypeStruct((bsz, t, A_WIDTH), BF16), jax.ShapeDtypeStruct((bsz, t, B_OUT), BF16)],
        compiler_params=_cparams(("parallel", "parallel")),
        name="even_post",
    )(yt, bonus, *([h3] * (A_WIDTH // LANES + 2)), lnx_g.reshape(1, -1), lnx_b.reshape(1, -1), head_ones, *dwa_flat)


def _mem_attn_kernel(*refs):
    q_refs, g_refs = refs[:M_HEADS], refs[M_HEADS:2 * M_HEADS]
    kv_refs, u_ref = refs[2 * M_HEADS:-1], refs[-1]
    dn = (((1,), (1,)), ((), ()))
    hds = range(M_HEADS)
    if len(kv_refs) == 1:
        ks = [kv_refs[0][0, 0, :, 0, hd, :] for hd in hds]
        vs = [kv_refs[0][0, 0, :, 1, hd, :] for hd in hds]
    else:
        ks = [kv_refs[0][0][:, hd * M_HEAD_DIM:(hd + 1) * M_HEAD_DIM] for hd in hds]
        vs = [kv_refs[1][0][:, hd * M_HEAD_DIM:(hd + 1) * M_HEAD_DIM] for hd in hds]
    ss = [lax.dot_general(q_refs[hd][0].astype(BF16), ks[hd].astype(BF16), dn,
                          preferred_element_type=F32) * (M_HEAD_DIM ** -0.5) for hd in hds]
    ms = [jnp.max(s, axis=-1, keepdims=True) for s in ss]
    es = [jnp.exp(s - m) for s, m in zip(ss, ms)]
    ps = [e / jnp.sum(e, axis=-1, keepdims=True) for e in es]
    os_ = [jnp.dot(ps[hd].astype(BF16), vs[hd].astype(BF16), preferred_element_type=F32) for hd in hds]
    u_ref[0] = jnp.concatenate([os_[hd] * _silu(g_refs[hd][0]) for hd in hds], axis=-1).astype(BF16)


def mem_attn(h3, mkv, layer, u_q, u_g):
    bsz, t, _ = h3.shape
    tq = min(t, 256)

    def col_spec(u):
        return pl.BlockSpec((1, tq, LANES), lambda b, i: (b, i, u))

    if layer is None:
        kv_specs = [pl.BlockSpec((1, M_TOKENS, M_WIDTH), lambda b, i: (b, 0, 0)),
                    pl.BlockSpec((1, M_TOKENS, M_WIDTH), lambda b, i: (b, 0, 1))]
    else:
        kv_specs = [pl.BlockSpec((1, 1, M_TOKENS, 2, M_HEADS, M_HEAD_DIM), lambda b, i: (layer, b, 0, 0, 0, 0))]
    return pl.pallas_call(
        _mem_attn_kernel,
        grid=(bsz, t // tq),
        in_specs=[col_spec(u_q + hd) for hd in range(M_HEADS)] + [col_spec(u_g + hd) for hd in range(M_HEADS)]
        + kv_specs,
        out_specs=pl.BlockSpec((1, tq, M_WIDTH), lambda b, i: (b, i, 0)),
        out_shape=jax.ShapeDtypeStruct((bsz, t, M_WIDTH), BF16),
        compiler_params=_cparams(("parallel", "parallel")),
        name="mem_attn",
    )(*([h3] * (2 * M_HEADS)), *([mkv] * len(kv_specs)))


def _rope_kernel(pos0, ang_ref, cos_ref, sin_ref):
    rows = cos_ref.shape[0]
    base = pl.program_id(0) * rows
    pos = (lax.broadcasted_iota(jnp.int32, cos_ref.shape, 0) + base).astype(F32) + pos0
    ph = pos * ang_ref[...]
    cos_ref[...] = jnp.cos(ph)
    sin_ref[...] = jnp.sin(ph)


def rope_tables(rows, pos0):
    angle = 1.0 / (C_ROT_BASE ** jnp.linspace(0.0, 1.0, C_HEAD_DIM // 2, dtype=F32))
    ang = jnp.repeat(angle, 2).reshape(1, C_HEAD_DIM)
    tr = min(rows, 512)
    spec = pl.BlockSpec((tr, C_HEAD_DIM), lambda i: (i, 0))
    shape = jax.ShapeDtypeStruct((rows, C_HEAD_DIM), F32)
    return pl.pallas_call(
        functools.partial(_rope_kernel, float(pos0)),
        grid=(rows // tr,),
        in_specs=[pl.BlockSpec((1, C_HEAD_DIM), lambda i: (0, 0))],
        out_specs=[spec, spec],
        out_shape=[shape, shape],
        compiler_params=_cparams(("parallel",)),
        name="rope_tables",
    )(ang)


def _rot_pairs(z):
    even = lax.broadcasted_iota(jnp.int32, (z.shape[0], LANES), 1) % 2 == 0
    parts = []
    for blk in range(z.shape[1] // LANES):
        zb = z[:, blk * LANES:(blk + 1) * LANES]
        nxt = pltpu.roll(zb, LANES - 1, axis=1)
        prv = pltpu.roll(zb, 1, axis=1)
        parts.append(jnp.where(even, -nxt, prv))
    return jnp.concatenate(parts, axis=-1)


def _retention_kernel(tb, chunk, lg_ref, q_ref, k_ref, v_ref, g_ref, cos_ref, sin_ref, r0_ref,
                      u_ref, rout_ref, r_scr, dmat_scr, *pad):
    c = pl.program_id(2)

    @pl.when(c == 0)
    def _():
        r_scr[...] = r0_ref[0]
        di = lax.broadcasted_iota(jnp.int32, (C_CHUNK, C_CHUNK), 0)
        dj = lax.broadcasted_iota(jnp.int32, (C_CHUNK, C_CHUNK), 1)
        dd = (di - dj).astype(F32)
        for hh in range(C_HEADS_PER_STEP):
            lg0 = lg_ref[pl.program_id(1) * C_HEADS_PER_STEP + hh]
            dmat_scr[hh] = jnp.where(dd >= 0, jnp.exp(lg0 * jnp.maximum(dd, 0.0)), 0.0)

    if tb == C_CHUNK:
        q2, k2, v2 = q_ref[0], k_ref[0], v_ref[0]
    else:
        vals = []
        for src, buf in zip((q_ref, k_ref, v_ref), pad):
            buf[...] = jnp.zeros_like(buf)
            buf[0:tb, :] = src[0]
            vals.append(buf[...])
        q2, k2, v2 = vals
    cos, sin = cos_ref[...], sin_ref[...]
    idx = lax.broadcasted_iota(jnp.int32, (C_CHUNK, 1), 0).astype(F32)
    dn = (((1,), (1,)), ((), ()))
    hh_r = range(C_HEADS_PER_STEP)
    sls = [slice(hh * C_HEAD_DIM, (hh + 1) * C_HEAD_DIM) for hh in hh_r]
    lgs = [lg_ref[pl.program_id(1) * C_HEADS_PER_STEP + hh] for hh in hh_r]
    qrs = [q2[:, sl] * cos + _rot_pairs(q2[:, sl]) * sin for sl in sls]
    krs = [(k2[:, sl] * cos + _rot_pairs(k2[:, sl]) * sin) * (C_HEAD_DIM ** -0.5) for sl in sls]
    qbs = [qr.astype(BF16) for qr in qrs]
    vbs = [v2[:, sl].astype(BF16) for sl in sls]
    scs = [lax.dot_general(qbs[hh], krs[hh].astype(BF16), dn, preferred_element_type=F32) * dmat_scr[hh]
           for hh in hh_r]
    r_olds = [r_scr[hh] for hh in hh_r]
    kz_ts = [(kr * jnp.exp(lg * (chunk - 1.0 - idx))).T.astype(BF16) for kr, lg in zip(krs, lgs)]
    os_ = [jnp.dot(sc.astype(BF16), vb, preferred_element_type=F32)
           + jnp.dot(qb, r_old.astype(BF16), preferred_element_type=F32) * jnp.exp(lg * (idx + 1.0))
           for sc, vb, qb, r_old, lg in zip(scs, vbs, qbs, r_olds, lgs)]
    for hh in hh_r:
        r_scr[hh] = (r_olds[hh] * jnp.exp(lgs[hh] * jnp.full((1, 1), chunk, F32))
                     + jnp.dot(kz_ts[hh], vbs[hh], preferred_element_type=F32))
    us = []
    for hh in hh_r:
        o = os_[hh][0:tb]
        y = o * lax.rsqrt(jnp.mean(o * o, axis=-1, keepdims=True) + C_NORM_EPS)
        us.append(y * _silu(g_ref[0][:, sls[hh]]))
    u_ref[0] = jnp.concatenate(us, axis=-1).astype(BF16)

    @pl.when(c == pl.num_programs(2) - 1)
    def _():
        rout_ref[0] = r_scr[...]


def retention(h3, cos, sin, r0):
    bsz, t, _ = h3.shape
    tb = min(t, C_CHUNK)
    nc = t // tb
    chunk = float(tb)
    lg = jnp.log(1.0 - 2.0 ** (-5.0 - jnp.arange(C_HEADS, dtype=F32)))
    hps = C_HEADS_PER_STEP
    width = hps * C_HEAD_DIM
    nq = C_WIDTH // width

    def col_spec(off):
        return pl.BlockSpec((1, tb, width), lambda b, hd, c: (b, c, off + hd))

    tab_spec = pl.BlockSpec((C_CHUNK, C_HEAD_DIM), lambda b, hd, c: (c, 0))
    st_spec = pl.BlockSpec((1, hps, C_HEAD_DIM, C_HEAD_DIM), lambda b, hd, c: (b, hd, 0, 0))
    scratch = [pltpu.VMEM((hps, C_HEAD_DIM, C_HEAD_DIM), F32), pltpu.VMEM((hps, C_CHUNK, C_CHUNK), F32)]
    if tb != C_CHUNK:
        scratch += [pltpu.VMEM((C_CHUNK, width), F32)] * 3
    return pl.pallas_call(
        functools.partial(_retention_kernel, tb, chunk),
        grid=(bsz, C_HEADS // hps, nc),
        in_specs=[pl.BlockSpec(memory_space=pltpu.SMEM),
                  col_spec(0), col_spec(nq), col_spec(2 * nq), col_spec(3 * nq),
                  tab_spec, tab_spec, st_spec],
        out_specs=[pl.BlockSpec((1, tb, width), lambda b, hd, c: (b, c, hd)), st_spec],
        out_shape=[jax.ShapeDtypeStruct((bsz, t, C_WIDTH), BF16), jax.ShapeDtypeStruct(r0.shape, F32)],
        scratch_shapes=scratch,
        compiler_params=_cparams(("parallel", "parallel", "arbitrary")),
        name="retention",
    )(lg, h3, h3, h3, h3, cos, sin, r0)


def _even_layer(x, x_bf, mkv, dwa_bufs, s0, shift0, w_in_bf, w_out_bf, e, ln_g, ln_b, rw, head_ones):
    bsz, t, d = x.shape
    h = matmul(x_bf.reshape(bsz * t, d), w_in_bf, e)
    h3 = h.reshape(bsz, t, NU * LANES)
    mu, w0, wup, a0, aup, k_k, k_a, r_k, lnx_g, lnx_b = rw
    r, dcy, k2, kk, bb, vt, bonus = rwkv_prep(h3, shift0, (mu, w0, wup, a0, aup, k_k, k_a, r_k), head_ones)
    yt, s_new = rwkv_scan((r, dcy, k2, kk, bb), vt, s0)
    if dwa_bufs is None:
        dwa = [dwa_prompt(h3, g) for g in range(B_GROUPS)]
    else:
        dwa = dwa_decode(h3, dwa_bufs, e)
    u_a, u_b = even_post(yt, bonus, h3, lnx_g, lnx_b, head_ones, dwa)
    u_m = mem_attn(h3, mkv[0], mkv[1], U_QM_EVEN, U_GM_EVEN)
    x2, x2_bf = outproj_ln([u_a.reshape(bsz * t, -1), u_b.reshape(bsz * t, -1), u_m.reshape(bsz * t, -1)],
                           w_out_bf, e, x.reshape(bsz * t, d), ln_g, ln_b)
    rows = []
    for g, (win, _) in enumerate(B_CONFIGS):
        keep = t if dwa_bufs is not None else min(win, t)
        kg = h3[:, t - keep:, U_KB * LANES + g * B_OUT:U_KB * LANES + (g + 1) * B_OUT]
        vg = h3[:, t - keep:, U_VB * LANES + g * B_OUT:U_VB * LANES + (g + 1) * B_OUT]
        rows.append(jnp.stack([kg.reshape(bsz, keep, B_HEADS_PER_GROUP, B_HEAD_DIM),
                               vg.reshape(bsz, keep, B_HEADS_PER_GROUP, B_HEAD_DIM)], axis=2))
    return x2.reshape(bsz, t, d), x2_bf.reshape(bsz, t, d), s_new, h3[:, t - 1, :A_SHIFT_W], rows


def _odd_layer(x, x_bf, mkv, r0, tabs, w_in_bf, w_out_bf, o, ln_g, ln_b):
    bsz, t, d = x.shape
    h = matmul(x_bf.reshape(bsz * t, d), w_in_bf, o)
    h3 = h.reshape(bsz, t, NU * LANES)
    u_c, r_new = retention(h3, tabs[0], tabs[1], r0)
    u_m = mem_attn(h3, mkv[0], mkv[1], U_QM_ODD, U_GM_ODD)
    x2, x2_bf = outproj_ln([u_c.reshape(bsz * t, -1), u_m.reshape(bsz * t, -1)],
                           w_out_bf, o, x.reshape(bsz * t, d), ln_g, ln_b)
    return x2.reshape(bsz, t, d), x2_bf.reshape(bsz, t, d), r_new


def kernel(x_prompt, x_sample, state_rwkv, state_rwkv_shift, cache_dwa_g0, cache_dwa_g1, cache_dwa_g2, state_ret, cache_mem_kv, mem_prompt, w_in_even, w_out_even, w_in_odd, w_out_odd, w_mem_kv, ln_g, ln_b, rwkv_mu, rwkv_w0, rwkv_w_up, rwkv_a0, rwkv_a_up, rwkv_k_k, rwkv_k_a, rwkv_r_k, rwkv_lnx_g, rwkv_lnx_b):
    xp, xs = x_prompt, x_sample
    xp_bf, xs_bf = xp.astype(BF16), xs.astype(BF16)
    bp, tp_len, d = xp.shape
    bs, ts_len, _ = xs.shape
    dwa_cache = (cache_dwa_g0, cache_dwa_g1, cache_dwa_g2)
    head_ones = _block_ones(LANES, A_HEAD_DIM)
    mem_bf = mem_prompt.reshape(bp * M_TOKENS, d).astype(BF16)
    tabs_p = rope_tables(max(tp_len, C_CHUNK), 0)
    tabs_s = rope_tables(max(ts_len, C_CHUNK), PAST_LEN)
    rwkv_p, rwkv_s, shift_p, shift_s, ret_p, ret_s, mem_p = [], [], [], [], [], [], []
    dwa_p = [[] for _ in B_CONFIGS]
    dwa_s = [[] for _ in B_CONFIGS]
    w_in_even_bf = jnp.concatenate(
        [w_in_even.astype(BF16), jnp.zeros(w_in_even.shape[:2] + (EVEN_IN_PAD - EVEN_IN,), BF16)], axis=-1)
    w_out_even_bf = w_out_even.astype(BF16)
    w_in_odd_bf = w_in_odd
    w_out_odd_bf = w_out_odd.astype(BF16)
    w_mem_bf = w_mem_kv
    for l in range(DEPTH):
        mkv_new = matmul(mem_bf, w_mem_bf, l).reshape(bp, M_TOKENS, 2 * M_WIDTH)
        mem_p.append(mkv_new.reshape(bp, M_TOKENS, 2, M_HEADS, M_HEAD_DIM))
        mkv_p = (mkv_new, None)
        mkv_s = (cache_mem_kv, l)
        if l % 2 == 0:
            e = l // 2
            w_in_bf, w_out_bf = w_in_even_bf, w_out_even_bf
            rw = (rwkv_mu[e], rwkv_w0[e], rwkv_w_up[e], rwkv_a0[e], rwkv_a_up[e], rwkv_k_k[e], rwkv_k_a[e],
                  rwkv_r_k[e], rwkv_lnx_g[e], rwkv_lnx_b[e])
            s0 = jnp.zeros((bp, A_HEADS, A_HEAD_DIM, A_HEAD_DIM), F32)
            sh0 = jnp.zeros((bp, A_SHIFT_W), F32)
            xp, xp_bf, st, sh, rows = _even_layer(xp, xp_bf, mkv_p, None, s0, sh0, w_in_bf, w_out_bf, e,
                                                  ln_g[l], ln_b[l], rw, head_ones)
            rwkv_p.append(st)
            shift_p.append(sh)
            for g in range(B_GROUPS):
                dwa_p[g].append(rows[g])
            bufs = dwa_cache
            xs, xs_bf, st, sh, rows = _even_layer(xs, xs_bf, mkv_s, bufs, state_rwkv[e], state_rwkv_shift[e],
                                                  w_in_bf, w_out_bf, e, ln_g[l], ln_b[l], rw, head_ones)
            rwkv_s.append(st)
            shift_s.append(sh)
            for g in range(B_GROUPS):
                dwa_s[g].append(rows[g])
        else:
            o = l // 2
            r0 = jnp.zeros((bp, C_HEADS, C_HEAD_DIM, C_HEAD_DIM), F32)
            xp, xp_bf, st = _odd_layer(xp, xp_bf, mkv_p, r0, tabs_p, w_in_odd_bf, w_out_odd_bf, o, ln_g[l], ln_b[l])
            ret_p.append(st)
            xs, xs_bf, st = _odd_layer(xs, xs_bf, mkv_s, state_ret[o], tabs_s, w_in_odd_bf, w_out_odd_bf, o,
                                       ln_g[l], ln_b[l])
            ret_s.append(st)
    return (xp, xs, jnp.stack(rwkv_p), jnp.stack(rwkv_s), jnp.stack(shift_p), jnp.stack(shift_s),
            jnp.stack(dwa_p[0]), jnp.stack(dwa_s[0]), jnp.stack(dwa_p[1]), jnp.stack(dwa_s[1]),
            jnp.stack(dwa_p[2]), jnp.stack(dwa_s[2]), jnp.stack(ret_p), jnp.stack(ret_s), jnp.stack(mem_p))
```

```python
import functools

import numpy as np
import jax
import jax.numpy as jnp
from jax import lax
from jax.experimental import pallas as pl
from jax.experimental.pallas import tpu as pltpu

F32 = jnp.float32
BF16 = jnp.bfloat16

D_MODEL = 2048
DEPTH = 4
PAST_LEN = 16384
ALPHA = (2 * DEPTH) ** 0.25
LN_EPS = 1e-5
A_HEADS = 12
A_HEAD_DIM = 64
A_WIDTH = A_HEADS * A_HEAD_DIM
A_LORA = 64
A_SHIFT_W = 3 * A_WIDTH + 2 * A_LORA
A_GN_EPS = 64e-5
B_CONFIGS = ((128, 1), (512, 4), (2048, 16))
B_GROUPS = 3
B_HEADS_PER_GROUP = 4
B_HEAD_DIM = 64
B_OUT = B_HEADS_PER_GROUP * B_HEAD_DIM
B_WIDTH = B_GROUPS * B_OUT
B_BLK = 128
B_RES_PER_ITER = 4
C_HEADS = 6
C_HEAD_DIM = 256
C_WIDTH = C_HEADS * C_HEAD_DIM
C_CHUNK = 128
C_HEADS_PER_STEP = 6
C_ROT_BASE = 10000.0
C_NORM_EPS = 1e-6
M_TOKENS = 256
M_HEADS = 4
M_HEAD_DIM = 128
M_WIDTH = M_HEADS * M_HEAD_DIM
EVEN_IN = A_SHIFT_W + A_WIDTH + 3 * B_WIDTH + B_OUT + 2 * M_WIDTH
ODD_IN = 4 * C_WIDTH + 2 * M_WIDTH

LANES = 128
SUBLANES = 8
VMEM_LIMIT = 48 * 1024 * 1024

EVEN_IN_PAD = 7168
U_GATE_A = A_SHIFT_W // LANES
U_QB = U_GATE_A + A_WIDTH // LANES
U_KB = U_QB + B_WIDTH // LANES
U_VB = U_KB + B_WIDTH // LANES
U_GATE_B = U_VB + B_WIDTH // LANES
U_QM_EVEN = U_GATE_B + B_OUT // LANES
U_GM_EVEN = U_QM_EVEN + M_WIDTH // LANES
U_QM_ODD = 4 * C_WIDTH // LANES
U_GM_ODD = U_QM_ODD + M_WIDTH // LANES
NU = EVEN_IN_PAD // LANES


def _cparams(sem):
    return pltpu.CompilerParams(dimension_semantics=sem, vmem_limit_bytes=VMEM_LIMIT)


def _sigmoid(z):
    return 1.0 / (1.0 + jnp.exp(-z))


def _silu(z):
    return z * _sigmoid(z)


def _mm_kernel(x_ref, w_ref, o_ref, w_scr):
    @pl.when(pl.program_id(1) == 0)
    def _():
        w_scr[...] = w_ref[0].astype(BF16)

    o_ref[...] = jnp.dot(x_ref[...], w_scr[...], preferred_element_type=F32)


def matmul(x, w, layer):
    m, k = x.shape
    n = w.shape[2]
    tm = min(m, 1024)
    tn = 1024 if n % 1024 == 0 else 512
    return pl.pallas_call(
        _mm_kernel,
        grid=(n // tn, m // tm),
        in_specs=[pl.BlockSpec((tm, k), lambda j, i: (i, 0)),
                  pl.BlockSpec((1, k, tn), lambda j, i: (layer, 0, j))],
        out_specs=pl.BlockSpec((tm, tn), lambda j, i: (i, j)),
        out_shape=jax.ShapeDtypeStruct((m, n), F32),
        scratch_shapes=[pltpu.VMEM((k, tn), BF16)],
        compiler_params=_cparams(("parallel", "arbitrary")),
        name="matmul",
    )(x, w)


def _outproj_kernel(n_u, *refs):
    u_refs = refs[:n_u]
    w_refs = refs[n_u:2 * n_u]
    x_ref, g_ref, b_ref, o_ref, obf_ref = refs[2 * n_u:]
    acc = jnp.dot(u_refs[0][...], w_refs[0][0], preferred_element_type=F32)
    for u_ref, w_ref in zip(u_refs[1:], w_refs[1:]):
        acc = acc + jnp.dot(u_ref[...], w_ref[0], preferred_element_type=F32)
    z = ALPHA * x_ref[...] + acc
    mu = jnp.mean(z, axis=-1, keepdims=True)
    zc = z - mu
    var = jnp.mean(zc * zc, axis=-1, keepdims=True)
    y = zc * lax.rsqrt(var + LN_EPS) * g_ref[...] + b_ref[...]
    o_ref[...] = y
    obf_ref[...] = y.astype(BF16)


def outproj_ln(us, w_out, layer, x, g, b):
    m, d = x.shape
    tm = min(m, 512)
    n_u = len(us)
    in_specs = [pl.BlockSpec((tm, u.shape[1]), lambda i: (i, 0)) for u in us]
    row = 0
    for u in us:
        kw = u.shape[1]
        assert row % kw == 0
        in_specs.append(pl.BlockSpec((1, kw, d), functools.partial(lambda i, r: (layer, r, 0), r=row // kw)))
        row += kw
    assert row == w_out.shape[1]
    in_specs += [pl.BlockSpec((tm, d), lambda i: (i, 0)),
                 pl.BlockSpec((1, d), lambda i: (0, 0)),
                 pl.BlockSpec((1, d), lambda i: (0, 0))]
    return pl.pallas_call(
        functools.partial(_outproj_kernel, n_u),
        grid=(m // tm,),
        in_specs=in_specs,
        out_specs=[pl.BlockSpec((tm, d), lambda i: (i, 0)), pl.BlockSpec((tm, d), lambda i: (i, 0))],
        out_shape=[jax.ShapeDtypeStruct((m, d), F32), jax.ShapeDtypeStruct((m, d), BF16)],
        compiler_params=_cparams(("parallel",)),
        name="outproj_ln",
    )(*us, *([w_out] * n_u), x, g.reshape(1, d), b.reshape(1, d))


def _seg_sum(x, ones_bf):
    hi = x.astype(BF16)
    r1 = x - hi.astype(F32)
    mid = r1.astype(BF16)
    lo = (r1 - mid.astype(F32)).astype(BF16)
    outs = []
    for p in range(x.shape[1] // LANES):
        sl = slice(p * LANES, (p + 1) * LANES)
        terms = jnp.concatenate([hi[:, sl], mid[:, sl], lo[:, sl]], axis=1)
        outs.append(jnp.dot(terms, jnp.concatenate([ones_bf] * 3, axis=0), preferred_element_type=F32))
    return jnp.concatenate(outs, axis=1)


def _block_ones(width, seg):
    i = np.arange(width) // seg
    return jnp.asarray((i[:, None] == i[None, :]).astype(np.float32), dtype=BF16)


def _rwkv_prep_kernel(tt, tp, h_ref, prev_ref, sh0_ref, mu_ref, w0_ref, wup_ref, a0_ref, aup_ref,
                      kk_ref_p, ka_ref_p, rk_ref_p, ones_ref,
                      r_o, d_o, k_o, kk_o, b_o, vt_o, bonus_o, *scratch):
    i = pl.program_id(1)
    x = h_ref[0]
    pr = prev_ref[0]
    prev_last = pr[pr.shape[0] - 1:pr.shape[0], :]
    first = jnp.where(i == 0, sh0_ref[0], prev_last)
    row = lax.broadcasted_iota(jnp.int32, x.shape, 0)
    if tt % SUBLANES == 0:
        rolled = pltpu.roll(x, 1, axis=0)
    else:
        rolled = jnp.concatenate([x[tt - 1:tt], x[:tt - 1]], axis=0)
    prev = jnp.where(row == 0, first, rolled)
    hs = x + (prev - x) * mu_ref[...]
    r = hs[:, 0:A_WIDTH]
    k = hs[:, A_WIDTH:2 * A_WIDTH]
    v = hs[:, 2 * A_WIDTH:3 * A_WIDTH]
    hw = hs[:, 3 * A_WIDTH:3 * A_WIDTH + A_LORA]
    ha = hs[:, 3 * A_WIDTH + A_LORA:A_SHIFT_W]
    zw = w0_ref[...] + jnp.dot(jnp.tanh(hw).astype(BF16), wup_ref[...], preferred_element_type=F32)
    nz = -zw
    softplus = jnp.maximum(nz, 0.0) + jnp.log(1.0 + jnp.exp(-jnp.abs(nz)))
    w_log = -softplus - 0.5
    neg_log_decay = jnp.exp(w_log)
    a = _sigmoid(a0_ref[...] + jnp.dot(ha.astype(BF16), aup_ref[...], preferred_element_type=F32))
    ones_bf = ones_ref[...]
    kk = k * kk_ref_p[...]
    kk = kk * lax.rsqrt(jnp.maximum(_seg_sum(kk * kk, ones_bf), 1e-24))
    k2 = k * (1.0 + (a - 1.0) * ka_ref_p[...])
    bb = kk * a
    bonus_o[0] = _seg_sum(r * k2 * rk_ref_p[...], ones_bf) * v
    width = r_o.shape[3]
    for s in range(A_WIDTH // width):
        sl = slice(s * width, (s + 1) * width)
        r_o[0, s] = r[:, sl]
        d_o[0, s] = neg_log_decay[:, sl]
        k_o[0, s] = k2[:, sl]
        kk_o[0, s] = kk[:, sl]
        b_o[0, s] = bb[:, sl]
    for p in range(A_HEADS // 2):
        vp = v[:, p * LANES:(p + 1) * LANES]
        if tt == tp:
            vt_o[0, p] = vp.T
        else:
            pad = scratch[0]
            pad[...] = jnp.zeros_like(pad)
            pad[0:tt, :] = vp
            vt_o[0, p] = pad[...].T


def rwkv_prep(h3, shift0, prm, head_ones):
    bsz, t, _ = h3.shape
    tt = min(t, 128)
    tp = max(tt, LANES)
    nt = t // tt
    pr_rows = min(t, SUBLANES)
    pb = tt // pr_rows
    mu, w0, wup, a0, aup, k_k, k_a, r_k = prm
    row_spec = pl.BlockSpec((1, A_WIDTH), lambda b, i: (0, 0))
    lora_spec = pl.BlockSpec((A_LORA, A_WIDTH), lambda b, i: (0, 0))
    width = LANES if t % A_CHUNK == 0 else A_HEAD_DIM
    head_spec = pl.BlockSpec((1, A_WIDTH // width, tt, width), lambda b, i: (b, 0, i, 0))
    head_shape = jax.ShapeDtypeStruct((bsz, A_WIDTH // width, t, width), F32)
    scratch = [] if tt == tp else [pltpu.VMEM((tp, LANES), F32)]
    return pl.pallas_call(
        functools.partial(_rwkv_prep_kernel, tt, tp),
        grid=(bsz, nt),
        in_specs=[pl.BlockSpec((1, tt, A_SHIFT_W), lambda b, i: (b, i, 0)),
                  pl.BlockSpec((1, pr_rows, A_SHIFT_W), lambda b, i: (b, jnp.maximum(i * pb - 1, 0), 0)),
                  pl.BlockSpec((1, 1, A_SHIFT_W), lambda b, i: (b, 0, 0)),
                  pl.BlockSpec((1, A_SHIFT_W), lambda b, i: (0, 0)),
                  row_spec, lora_spec, row_spec, lora_spec, row_spec, row_spec, row_spec,
                  pl.BlockSpec((LANES, LANES), lambda b, i: (0, 0))],
        out_specs=[head_spec] * 5 + [
            pl.BlockSpec((1, A_HEADS // 2, LANES, tp), lambda b, i: (b, 0, 0, i)),
            pl.BlockSpec((1, tt, A_WIDTH), lambda b, i: (b, i, 0))],
        out_shape=[head_shape] * 5 + [
            jax.ShapeDtypeStruct((bsz, A_HEADS // 2, LANES, nt * tp), F32),
            jax.ShapeDtypeStruct((bsz, t, A_WIDTH), F32)],
        scratch_shapes=scratch,
        compiler_params=_cparams(("parallel", "parallel")),
        name="rwkv_prep",
    )(h3, h3, shift0.reshape(bsz, 1, A_SHIFT_W), mu.reshape(1, -1), w0.reshape(1, -1), wup.astype(BF16),
      a0.reshape(1, -1), aup.astype(BF16), k_k.reshape(1, -1), k_a.reshape(1, -1), r_k.reshape(1, -1), head_ones)


def _rwkv_scan_kernel(steps, r_ref, d_ref, k_ref, kk_ref, b_ref, vt_ref, s0_ref, yt_ref, sout_ref, s_scr, p_scr):
    c = pl.program_id(1)

    @pl.when(c == 0)
    def _():
        s_scr[...] = s0_ref[0]

    tp = vt_ref.shape[3]
    lane = lax.broadcasted_iota(jnp.int32, (A_HEAD_DIM, tp), 1)
    yt_ref[...] = jnp.zeros_like(yt_ref)
    p_scr[...] = jnp.zeros_like(p_scr)
    ones_bf = jnp.ones((A_HEAD_DIM, tp), BF16)

    def write_y(t):
        msk = lane == t
        for hd in range(A_HEADS):
            y = jnp.dot(p_scr[hd], ones_bf, preferred_element_type=F32)
            yt_ref[0, hd] = jnp.where(msk, y, yt_ref[0, hd])

    def body(t, carry):
        write_y(t - 1)
        msk = lane == t
        sas, vcols = [], []
        for hd in range(A_HEADS):
            sas.append(-jnp.sum(s_scr[hd] * kk_ref[0, hd, pl.ds(t, 1), :], axis=-1, keepdims=True))
            vcols.append(jnp.sum(jnp.where(msk, vt_ref[0, hd], 0.0), axis=-1, keepdims=True))
        for hd in range(A_HEADS):
            s = (s_scr[hd] * jnp.exp(-d_ref[0, hd, pl.ds(t, 1), :]) + sas[hd] * b_ref[0, hd, pl.ds(t, 1), :]
                 + vcols[hd] * k_ref[0, hd, pl.ds(t, 1), :])
            s_scr[hd] = s
            p_scr[hd] = (s * r_ref[0, hd, pl.ds(t, 1), :]).astype(BF16)
        return carry

    lax.fori_loop(0, steps, body, 0, unroll=4)
    write_y(steps - 1)

    @pl.when(c == pl.num_programs(1) - 1)
    def _():
        sout_ref[0] = s_scr[...]


A_CHUNK = 128
A_CHUNK_HEADS = 6
_NT = (((1,), (1,)), ((), ()))


def _split2(x):
    hi = x.astype(BF16)
    return hi, (x - hi.astype(F32)).astype(BF16)


def _mm3(a, b):
    ah, al = _split2(a)
    bh, bl = _split2(b)
    return jnp.dot(jnp.concatenate([ah, ah, al], axis=1), jnp.concatenate([bh, bl, bh], axis=0),
                   preferred_element_type=F32)


def _mm3_nt(a, b):
    ah, al = _split2(a)
    bh, bl = _split2(b)
    return lax.dot_general(jnp.concatenate([ah, ah, al], axis=1), jnp.concatenate([bh, bl, bh], axis=1), _NT,
                           preferred_element_type=F32)


def _inv_unit_lower(lbs, row, col):
    def blk(s):
        return (row // s) == (col // s)
    eye = jnp.where(row == col, 1.0, 0.0)
    ds = [jnp.where(blk(8), lb, 0.0) for lb in lbs]
    d2s = [_mm3(d, d) for d in ds]
    d4s = [_mm3(d2, d2) for d2 in d2s]
    xs = [eye + d for d in ds]
    xs = [x + _mm3(x, d2) for x, d2 in zip(xs, d2s)]
    xs = [x + _mm3(x, d4) for x, d4 in zip(xs, d4s)]
    s = 8
    while s < A_CHUNK:
        msk = blk(2 * s) & jnp.logical_not(blk(s))
        ts = [_mm3(x, jnp.where(msk, lb, 0.0)) for x, lb in zip(xs, lbs)]
        xs = [x + _mm3(t, x) for x, t in zip(xs, ts)]
        s *= 2
    return xs


def _rwkv_chunk_heads(es, rs, kks, bs, k2s, vs, sps, tril, row, col):
    n_tok, n = A_CHUNK, A_HEAD_DIM
    n_pair = len(es)
    idx = range(2 * n_pair)
    cs = []
    for e in es:
        e_hi = e.astype(BF16)
        e_r1 = e - e_hi.astype(F32)
        e_mid = e_r1.astype(BF16)
        e_lo = (e_r1 - e_mid.astype(F32)).astype(BF16)
        cs.append(jnp.dot(jnp.concatenate([tril, tril, tril], axis=1),
                          jnp.concatenate([e_hi, e_mid, e_lo], axis=0), preferred_element_type=F32))
    ms = [c[n_tok // 2 - 1:n_tok // 2] for c in cs]
    ccs = [c - m for c, m in zip(cs, ms)]
    gbs = [jnp.exp(cc) for cc in ccs]
    own = [lax.broadcasted_iota(jnp.int32, (n_tok, LANES), 1) < n]
    own.append(jnp.logical_not(own[0]))
    own_s = [lax.broadcasted_iota(jnp.int32, (n, LANES), 1) < n]
    own_s.append(jnp.logical_not(own_s[0]))

    def halves(xs, masks):
        return [jnp.where(masks[i % 2], xs[i // 2], 0.0) for i in idx]

    ats = halves([-kks[p] * jnp.exp(es[p] - ccs[p]) for p in range(n_pair)], own)
    bts = halves([bs[p] * gbs[p] for p in range(n_pair)], own)
    kts = halves([k2s[p] * gbs[p] for p in range(n_pair)], own)
    rts = halves([rs[p] * jnp.exp(-ccs[p]) for p in range(n_pair)], own)
    s0ps = halves([sps[p] * jnp.exp(-ms[p]) for p in range(n_pair)], own_s)
    bks = [jnp.concatenate([bts[i], kts[i]], axis=0) for i in idx]
    zs = [_mm3_nt(ats[i], bks[i]) for i in idx]
    m2s = [_mm3_nt(bks[i], rts[i]) for i in idx]
    low = row > col
    upp = row <= col
    lbs = [jnp.where(low, z[:, :n_tok], 0.0) for z in zs]
    lks = [jnp.where(low, z[:, n_tok:], 0.0) for z in zs]
    mbks = [jnp.concatenate([jnp.where(upp, m2[:n_tok], 0.0), jnp.where(upp, m2[n_tok:], 0.0)], axis=0)
            for m2 in m2s]
    ws = _inv_unit_lower(lbs, row, col)
    ptqs = [_mm3(ws[i], jnp.concatenate([ats[i], lks[i]], axis=1)) for i in idx]
    srps = [_mm3_nt(s0ps[i], jnp.concatenate([rts[i], ptqs[i][:, :LANES]], axis=0)) for i in idx]
    us = [srps[i][:, n_tok:] + _mm3_nt(vs[i], ptqs[i][:, LANES:]) for i in idx]
    yss = [_mm3(jnp.concatenate([us[i], vs[i]], axis=1), jnp.concatenate([mbks[i], bks[i]], axis=1)) for i in idx]
    ys = [srps[i][:, :n_tok] + yss[i][:, :n_tok] for i in idx]
    s1s = [(s0ps[2 * p] + yss[2 * p][:, n_tok:] + s0ps[2 * p + 1] + yss[2 * p + 1][:, n_tok:])
           * jnp.exp(-ccs[p][n_tok - 1:n_tok]) for p in range(n_pair)]
    return ys, s1s


def _rwkv_chunk_kernel(r_ref, e_ref, k_ref, kk_ref, b_ref, vt_ref, s0_ref, yt_ref, sout_ref, s_scr):
    c = pl.program_id(1)
    n_pair = A_HEADS // 2

    @pl.when(c == 0)
    def _():
        for p in range(n_pair):
            s_scr[p] = jnp.concatenate([s0_ref[0, 2 * p], s0_ref[0, 2 * p + 1]], axis=1)

    row = lax.broadcasted_iota(jnp.int32, (A_CHUNK, A_CHUNK), 0)
    col = lax.broadcasted_iota(jnp.int32, (A_CHUNK, A_CHUNK), 1)
    tril = jnp.where(row >= col, 1.0, 0.0).astype(BF16)
    pairs_per_group = A_CHUNK_HEADS // 2

    def group(gi, carry):
        prs = [gi * pairs_per_group + j for j in range(pairs_per_group)]
        hds = [2 * p + k for p in prs for k in range(2)]
        ys, s1s = _rwkv_chunk_heads([e_ref[0, p] for p in prs], [r_ref[0, p] for p in prs],
                                    [kk_ref[0, p] for p in prs], [b_ref[0, p] for p in prs],
                                    [k_ref[0, p] for p in prs], [vt_ref[0, hd] for hd in hds],
                                    [s_scr[p] for p in prs], tril, row, col)
        for hd, y in zip(hds, ys):
            yt_ref[0, hd] = y
        for p, s1 in zip(prs, s1s):
            s_scr[p] = s1
        return carry

    lax.fori_loop(0, n_pair // pairs_per_group, group, 0)

    @pl.when(c == pl.num_programs(1) - 1)
    def _():
        for p in range(n_pair):
            sp = s_scr[p]
            sout_ref[0, 2 * p] = sp[:, :A_HEAD_DIM]
            sout_ref[0, 2 * p + 1] = sp[:, A_HEAD_DIM:]


def rwkv_scan(ops, vt, s0):
    r, d, k, kk, b = ops
    bsz, n_slab, t, width = r.shape
    tt = min(t, 128)
    tp = max(tt, LANES)
    nt = t // tt
    vt = vt.reshape(bsz, A_HEADS, A_HEAD_DIM, nt * tp)
    chunked = t % A_CHUNK == 0
    assert width == (LANES if chunked else A_HEAD_DIM)
    head_spec = pl.BlockSpec((1, n_slab, tt, width), lambda b_, i: (b_, 0, i, 0))
    vt_spec = pl.BlockSpec((1, A_HEADS, A_HEAD_DIM, tp), lambda b_, i: (b_, 0, 0, i))
    st_spec = pl.BlockSpec((1, A_HEADS, A_HEAD_DIM, A_HEAD_DIM), lambda b_, i: (b_, 0, 0, 0))
    if chunked:
        scratch = [pltpu.VMEM((A_HEADS // 2, A_HEAD_DIM, LANES), F32)]
    else:
        scratch = [pltpu.VMEM((A_HEADS, A_HEAD_DIM, A_HEAD_DIM), F32),
                   pltpu.VMEM((A_HEADS, A_HEAD_DIM, A_HEAD_DIM), BF16)]
    yt, s_new = pl.pallas_call(
        _rwkv_chunk_kernel if chunked else functools.partial(_rwkv_scan_kernel, tt),
        grid=(bsz, nt),
        in_specs=[head_spec] * 5 + [vt_spec, st_spec],
        out_specs=[vt_spec, st_spec],
        out_shape=[jax.ShapeDtypeStruct(vt.shape, F32), jax.ShapeDtypeStruct(s0.shape, F32)],
        scratch_shapes=scratch,
        compiler_params=_cparams(("parallel", "arbitrary")),
        name="rwkv_chunk" if chunked else "rwkv_scan",
    )(r, d, k, kk, b, vt, s0)
    return yt.reshape(bsz, A_HEADS // 2, LANES, nt * tp), s_new


def _dwa_prompt_kernel(dil, *refs):
    npair = B_OUT // LANES
    in_refs = [refs[5 * hp:5 * hp + 5] for hp in range(npair)]
    out_refs = refs[5 * npair:]
    n = pl.program_id(1)
    i = lax.broadcasted_iota(jnp.int32, (B_BLK, B_BLK), 0)
    j = lax.broadcasted_iota(jnp.int32, (B_BLK, B_BLK), 1)
    mask_prev = j >= i + jnp.where(n > 0, 0, B_BLK)
    mask_cur = j <= i
    neg = -jnp.inf
    scale = B_HEAD_DIM ** -0.5
    dn = (((1,), (1,)), ((), ()))

    n_res = B_RES_PER_ITER
    n_hh = LANES // B_HEAD_DIM

    def residues(it, carry):
        if dil > 1:
            rows = [pl.ds(it * n_res + k, B_BLK, stride=dil) for k in range(n_res)]
            loaded = {(k, hp): [ref[0, rows[k], :] for ref in in_refs[hp]]
                      for k in range(n_res) for hp in range(npair)}
            prev_masks = [mask_prev] * n_res
        else:
            rows = [pl.ds(k * B_BLK, B_BLK) for k in range(n_res)]
            loaded = {}
            for hp in range(npair):
                q_ref, kc_ref, kp_ref, vc_ref, vp_ref = in_refs[hp]
                for k in range(n_res):
                    loaded[(k, hp)] = [q_ref[0, rows[k], :], kc_ref[0, rows[k], :],
                                       kp_ref[0] if k == 0 else kc_ref[0, rows[k - 1], :],
                                       vc_ref[0, rows[k], :],
                                       vp_ref[0] if k == 0 else vc_ref[0, rows[k - 1], :]]
            prev_masks = [mask_prev] + [j >= i] * (n_res - 1)
        units = [(k, hp, hh) for k in range(n_res) for hp in range(npair) for hh in range(n_hh)]

        def part(u, which):
            k, hp, hh = u
            return loaded[(k, hp)][which][:, hh * B_HEAD_DIM:(hh + 1) * B_HEAD_DIM].astype(BF16)

        qs = [part(u, 0) for u in units]
        sps = [jnp.where(prev_masks[u[0]], lax.dot_general(q, part(u, 2), dn, preferred_element_type=F32) * scale, neg)
               for q, u in zip(qs, units)]
        scs = [jnp.where(mask_cur, lax.dot_general(q, part(u, 1), dn, preferred_element_type=F32) * scale, neg)
               for q, u in zip(qs, units)]
        ms = [jnp.maximum(jnp.max(sp, axis=-1, keepdims=True), jnp.max(sc, axis=-1, keepdims=True))
              for sp, sc in zip(sps, scs)]
        eps = [jnp.exp(sp - m) for sp, m in zip(sps, ms)]
        ecs = [jnp.exp(sc - m) for sc, m in zip(scs, ms)]
        ls = [jnp.sum(ep, axis=-1, keepdims=True) + jnp.sum(ec, axis=-1, keepdims=True) for ep, ec in zip(eps, ecs)]
        os_ = [jnp.dot((ep / l).astype(BF16), part(u, 4), preferred_element_type=F32)
               + jnp.dot((ec / l).astype(BF16), part(u, 3), preferred_element_type=F32)
               for ep, ec, l, u in zip(eps, ecs, ls, units)]
        lses = [jnp.broadcast_to(m + jnp.log(l), (B_BLK, B_HEAD_DIM)) for m, l in zip(ms, ls)]
        for k in range(n_res):
            for hp in range(npair):
                sel = [i_u for i_u, u in enumerate(units) if u[0] == k and u[1] == hp]
                out_refs[2 * hp][0, rows[k], :] = jnp.concatenate([os_[i_u] for i_u in sel], axis=-1)
                out_refs[2 * hp + 1][0, rows[k], :] = jnp.concatenate([lses[i_u] for i_u in sel], axis=-1)
        return carry

    lax.fori_loop(0, max(dil // n_res, 1), residues, 0)


def dwa_prompt(h3, g):
    bsz, t, _ = h3.shape
    win, dil = B_CONFIGS[g]
    nw = B_RES_PER_ITER if dil == 1 else 1
    assert win == B_BLK * dil and t % (nw * win) == 0 and (dil == 1 or dil % B_RES_PER_ITER == 0)
    npair = B_OUT // LANES

    def in_spec(unit, prev, hp):
        def imap(b, n):
            nn = jnp.maximum(n * nw - 1, 0) if prev else n
            return (b, nn, unit + npair * g + hp)
        return pl.BlockSpec((1, win if prev else nw * win, LANES), imap)

    in_specs = []
    for hp in range(npair):
        in_specs += [in_spec(U_QB, False, hp), in_spec(U_KB, False, hp), in_spec(U_KB, True, hp),
                     in_spec(U_VB, False, hp), in_spec(U_VB, True, hp)]
    out_spec = pl.BlockSpec((1, nw * win, LANES), lambda b, n: (b, n, 0))
    out_shape = jax.ShapeDtypeStruct((bsz, t, LANES), F32)
    outs = pl.pallas_call(
        functools.partial(_dwa_prompt_kernel, dil),
        grid=(bsz, t // (nw * win)),
        in_specs=in_specs,
        out_specs=[out_spec] * (2 * npair),
        out_shape=[out_shape] * (2 * npair),
        compiler_params=_cparams(("parallel", "parallel")),
        name="dwa_prompt",
    )(*([h3] * len(in_specs)))
    return list(outs)


def _dwa_decode_kernel(t_new, h_ref, c0_ref, c1_ref, c2_ref, *out_refs):
    scale = B_HEAD_DIM ** -0.5
    dn = (((1,), (1,)), ((), ()))
    head_of_lane = lax.broadcasted_iota(jnp.int32, (SUBLANES, B_OUT), 1) // B_HEAD_DIM
    sub = lax.broadcasted_iota(jnp.int32, (SUBLANES, B_OUT), 0)
    own = head_of_lane == sub
    m_idx = lax.broadcasted_iota(jnp.int32, (SUBLANES, B_BLK), 1)
    neg = -jnp.inf
    hrow = h_ref[0]
    for g, c_ref in enumerate((c0_ref, c1_ref, c2_ref)):
        _, dil = B_CONFIGS[g]
        g_refs = out_refs[4 * g:4 * g + 4]
        qs = hrow[:, U_QB * LANES + g * B_OUT:U_QB * LANES + (g + 1) * B_OUT]
        ks = hrow[:, U_KB * LANES + g * B_OUT:U_KB * LANES + (g + 1) * B_OUT]
        vs = hrow[:, U_VB * LANES + g * B_OUT:U_VB * LANES + (g + 1) * B_OUT]
        for t in range(t_new):
            res = t % dil
            kbuf = jnp.concatenate([c_ref[0, 0, :, res, 0, hd, :] for hd in range(B_HEADS_PER_GROUP)], axis=-1)
            vbuf = jnp.concatenate([c_ref[0, 0, :, res, 1, hd, :] for hd in range(B_HEADS_PER_GROUP)], axis=-1)
            qbd = jnp.where(own, jnp.broadcast_to(qs[t:t + 1], (SUBLANES, B_OUT)), 0.0)
            s_buf = lax.dot_general(qbd.astype(BF16), kbuf.astype(BF16), dn, preferred_element_type=F32) * scale
            s_buf = jnp.where(m_idx * dil + res >= t, s_buf, neg)
            new_rows = [n for n in range(t + 1) if (t - n) % dil == 0]
            kq = qbd.astype(BF16).astype(F32)
            s_new = [jnp.sum(kq * ks[n:n + 1].astype(BF16).astype(F32), axis=-1, keepdims=True) * scale
                     for n in new_rows]
            m = jnp.max(s_buf, axis=-1, keepdims=True)
            for sn in s_new:
                m = jnp.maximum(m, sn)
            e_buf = jnp.exp(s_buf - m)
            e_new = [jnp.exp(sn - m) for sn in s_new]
            l = jnp.sum(e_buf, axis=-1, keepdims=True)
            for en in e_new:
                l = l + en
            o = jnp.dot((e_buf / l).astype(BF16), vbuf.astype(BF16), preferred_element_type=F32)
            for n, en in zip(new_rows, e_new):
                o = o + (en / l).astype(BF16).astype(F32) * vs[n:n + 1].astype(BF16).astype(F32)
            lse = jnp.broadcast_to(m + jnp.log(l), (SUBLANES, B_OUT))
            o_row = jnp.sum(jnp.where(own, o, 0.0), axis=0, keepdims=True)
            lse_row = jnp.sum(jnp.where(own, lse, 0.0), axis=0, keepdims=True)
            for hp in range(B_OUT // LANES):
                g_refs[2 * hp][0, t:t + 1, :] = o_row[:, hp * LANES:(hp + 1) * LANES]
                g_refs[2 * hp + 1][0, t:t + 1, :] = lse_row[:, hp * LANES:(hp + 1) * LANES]


def dwa_decode(h3, caches, e):
    bsz, t, _ = h3.shape
    views, specs = [], []
    for g, (win, dil) in enumerate(B_CONFIGS):
        assert caches[g].shape[2] == win and (dil == 1 or t <= dil)
        used = min(dil, t)
        if used < dil:
            view = jnp.stack([caches[g][:, :, res::dil] for res in range(used)], axis=3)
        else:
            view = caches[g].reshape(caches[g].shape[0], bsz, win // dil, dil, 2, B_HEADS_PER_GROUP, B_HEAD_DIM)
        views.append(view)
        specs.append(pl.BlockSpec((1, 1, B_BLK, used, 2, B_HEADS_PER_GROUP, B_HEAD_DIM),
                                  lambda b: (e, b, 0, 0, 0, 0, 0)))
    out_spec = pl.BlockSpec((1, t, LANES), lambda b: (b, 0, 0))
    out_shape = jax.ShapeDtypeStruct((bsz, t, LANES), F32)
    n_out = 4 * B_GROUPS
    outs = pl.pallas_call(
        functools.partial(_dwa_decode_kernel, t),
        grid=(bsz,),
        in_specs=[pl.BlockSpec((1, t, NU * LANES), lambda b: (b, 0, 0))] + specs,
        out_specs=[out_spec] * n_out,
        out_shape=[out_shape] * n_out,
        compiler_params=_cparams(("parallel",)),
        name="dwa_decode",
    )(h3, *views)
    return [list(outs[4 * g:4 * g + 4]) for g in range(B_GROUPS)]


def _even_post_kernel(tt, yt_ref, bonus_ref, *refs):
    n_ga = A_WIDTH // LANES
    ga_refs = refs[:n_ga]
    gb0_ref, gb1_ref, lnxg_ref, lnxb_ref, ones_ref = refs[n_ga:n_ga + 5]
    dwa_refs = refs[n_ga + 5:n_ga + 5 + 4 * B_GROUPS]
    ua_ref, ub_ref = refs[n_ga + 5 + 4 * B_GROUPS:]
    ys = []
    for p in range(A_HEADS // 2):
        ys.append(yt_ref[0, p].T[0:tt, :])
    y = jnp.concatenate(ys, axis=-1)
    ones_bf = ones_ref[...]
    inv = 1.0 / A_HEAD_DIM
    mu = _seg_sum(y, ones_bf) * inv
    yc = y - mu
    var = _seg_sum(yc * yc, ones_bf) * inv
    ya = yc * lax.rsqrt(var + A_GN_EPS) * lnxg_ref[...] + lnxb_ref[...] + bonus_ref[0]
    gate_a = jnp.concatenate([g_ref[0] for g_ref in ga_refs], axis=-1)
    ua_ref[0] = (ya * _silu(gate_a)).astype(BF16)
    ybs = []
    for hp, gb_ref in enumerate((gb0_ref, gb1_ref)):
        o0, l0, o1, l1, o2, l2 = (dwa_refs[4 * g + 2 * hp + k][0] for g in range(B_GROUPS) for k in range(2))
        m = jnp.maximum(jnp.maximum(l0, l1), l2)
        e0, e1, e2 = jnp.exp(l0 - m), jnp.exp(l1 - m), jnp.exp(l2 - m)
        den = e0 + e1 + e2
        yb = (e0 / den) * o0 + (e1 / den) * o1 + (e2 / den) * o2
        ybs.append(yb * _silu(gb_ref[0]))
    ub_ref[0] = jnp.concatenate(ybs, axis=-1).astype(BF16)


def even_post(yt, bonus, h3, lnx_g, lnx_b, head_ones, dwa):
    bsz, t, _ = h3.shape
    tt = min(t, 128)
    tp = max(tt, LANES)
    nt = t // tt
    dwa_flat = [a for grp in dwa for a in grp]
    bspec = pl.BlockSpec((1, tt, B_OUT), lambda b, i: (b, i, 0))
    pspec = pl.BlockSpec((1, tt, LANES), lambda b, i: (b, i, 0))
    row_spec = pl.BlockSpec((1, A_WIDTH), lambda b, i: (0, 0))
    return pl.pallas_call(
        functools.partial(_even_post_kernel, tt),
        grid=(bsz, nt),
        in_specs=[pl.BlockSpec((1, A_HEADS // 2, LANES, tp), lambda b, i: (b, 0, 0, i)),
                  pl.BlockSpec((1, tt, A_WIDTH), lambda b, i: (b, i, 0)),
                  *[pl.BlockSpec((1, tt, LANES), functools.partial(lambda b, i, u: (b, i, u), u=U_GATE_A + u))
                    for u in range(A_WIDTH // LANES)],
                  pl.BlockSpec((1, tt, LANES), lambda b, i: (b, i, U_GATE_B)),
                  pl.BlockSpec((1, tt, LANES), lambda b, i: (b, i, U_GATE_B + 1)),
                  row_spec, row_spec,
                  pl.BlockSpec((LANES, LANES), lambda b, i: (0, 0))] + [pspec] * len(dwa_flat),
        out_specs=[pl.BlockSpec((1, tt, A_WIDTH), lambda b, i: (b, i, 0)), bspec],
        out_shape=[jax.ShapeDtypeStruct((bsz, t, A_WIDTH), BF16), jax.ShapeDtypeStruct((bsz, t, B_OUT), BF16)],
        compiler_params=_cparams(("parallel", "parallel")),
        name="even_post",
    )(yt, bonus, *([h3] * (A_WIDTH // LANES + 2)), lnx_g.reshape(1, -1), lnx_b.reshape(1, -1), head_ones, *dwa_flat)


def _mem_attn_kernel(*refs):
    q_refs, g_refs = refs[:M_HEADS], refs[M_HEADS:2 * M_HEADS]
    kv_refs, u_ref = refs[2 * M_HEADS:-1], refs[-1]
    dn = (((1,), (1,)), ((), ()))
    hds = range(M_HEADS)
    if len(kv_refs) == 1:
        ks = [kv_refs[0][0, 0, :, 0, hd, :] for hd in hds]
        vs = [kv_refs[0][0, 0, :, 1, hd, :] for hd in hds]
    else:
        ks = [kv_refs[0][0][:, hd * M_HEAD_DIM:(hd + 1) * M_HEAD_DIM] for hd in hds]
        vs = [kv_refs[1][0][:, hd * M_HEAD_DIM:(hd + 1) * M_HEAD_DIM] for hd in hds]
    ss = [lax.dot_general(q_refs[hd][0].astype(BF16), ks[hd].astype(BF16), dn,
                          preferred_element_type=F32) * (M_HEAD_DIM ** -0.5) for hd in hds]
    ms = [jnp.max(s, axis=-1, keepdims=True) for s in ss]
    es = [jnp.exp(s - m) for s, m in zip(ss, ms)]
    ps = [e / jnp.sum(e, axis=-1, keepdims=True) for e in es]
    os_ = [jnp.dot(ps[hd].astype(BF16), vs[hd].astype(BF16), preferred_element_type=F32) for hd in hds]
    u_ref[0] = jnp.concatenate([os_[hd] * _silu(g_refs[hd][0]) for hd in hds], axis=-1).astype(BF16)


def mem_attn(h3, mkv, layer, u_q, u_g):
    bsz, t, _ = h3.shape
    tq = min(t, 256)

    def col_spec(u):
        return pl.BlockSpec((1, tq, LANES), lambda b, i: (b, i, u))

    if layer is None:
        kv_specs = [pl.BlockSpec((1, M_TOKENS, M_WIDTH), lambda b, i: (b, 0, 0)),
                    pl.BlockSpec((1, M_TOKENS, M_WIDTH), lambda b, i: (b, 0, 1))]
    else:
        kv_specs = [pl.BlockSpec((1, 1, M_TOKENS, 2, M_HEADS, M_HEAD_DIM), lambda b, i: (layer, b, 0, 0, 0, 0))]
    return pl.pallas_call(
        _mem_attn_kernel,
        grid=(bsz, t // tq),
        in_specs=[col_spec(u_q + hd) for hd in range(M_HEADS)] + [col_spec(u_g + hd) for hd in range(M_HEADS)]
        + kv_specs,
        out_specs=pl.BlockSpec((1, tq, M_WIDTH), lambda b, i: (b, i, 0)),
        out_shape=jax.ShapeDtypeStruct((bsz, t, M_WIDTH), BF16),
        compiler_params=_cparams(("parallel", "parallel")),
        name="mem_attn",
    )(*([h3] * (2 * M_HEADS)), *([mkv] * len(kv_specs)))


def _rope_kernel(pos0, ang_ref, cos_ref, sin_ref):
    rows = cos_ref.shape[0]
    base = pl.program_id(0) * rows
    pos = (lax.broadcasted_iota(jnp.int32, cos_ref.shape, 0) + base).astype(F32) + pos0
    ph = pos * ang_ref[...]
    cos_ref[...] = jnp.cos(ph)
    sin_ref[...] = jnp.sin(ph)


def rope_tables(rows, pos0):
    angle = 1.0 / (C_ROT_BASE ** jnp.linspace(0.0, 1.0, C_HEAD_DIM // 2, dtype=F32))
    ang = jnp.repeat(angle, 2).reshape(1, C_HEAD_DIM)
    tr = min(rows, 512)
    spec = pl.BlockSpec((tr, C_HEAD_DIM), lambda i: (i, 0))
    shape = jax.ShapeDtypeStruct((rows, C_HEAD_DIM), F32)
    return pl.pallas_call(
        functools.partial(_rope_kernel, float(pos0)),
        grid=(rows // tr,),
        in_specs=[pl.BlockSpec((1, C_HEAD_DIM), lambda i: (0, 0))],
        out_specs=[spec, spec],
        out_shape=[shape, shape],
        compiler_params=_cparams(("parallel",)),
        name="rope_tables",
    )(ang)


def _rot_pairs(z):
    even = lax.broadcasted_iota(jnp.int32, (z.shape[0], LANES), 1) % 2 == 0
    parts = []
    for blk in range(z.shape[1] // LANES):
        zb = z[:, blk * LANES:(blk + 1) * LANES]
        nxt = pltpu.roll(zb, LANES - 1, axis=1)
        prv = pltpu.roll(zb, 1, axis=1)
        parts.append(jnp.where(even, -nxt, prv))
    return jnp.concatenate(parts, axis=-1)


def _retention_kernel(tb, chunk, lg_ref, q_ref, k_ref, v_ref, g_ref, cos_ref, sin_ref, r0_ref,
                      u_ref, rout_ref, r_scr, *pad):
    c = pl.program_id(2)

    @pl.when(c == 0)
    def _():
        r_scr[...] = r0_ref[0]

    if tb == C_CHUNK:
        q2, k2, v2 = q_ref[0], k_ref[0], v_ref[0]
    else:
        vals = []
        for src, buf in zip((q_ref, k_ref, v_ref), pad):
            buf[...] = jnp.zeros_like(buf)
            buf[0:tb, :] = src[0]
            vals.append(buf[...])
        q2, k2, v2 = vals
    cos, sin = cos_ref[...], sin_ref[...]
    ii = lax.broadcasted_iota(jnp.int32, (C_CHUNK, C_CHUNK), 0)
    jj = lax.broadcasted_iota(jnp.int32, (C_CHUNK, C_CHUNK), 1)
    diff = (ii - jj).astype(F32)
    idx = lax.broadcasted_iota(jnp.int32, (C_CHUNK, 1), 0).astype(F32)
    dn = (((1,), (1,)), ((), ()))
    hh_r = range(C_HEADS_PER_STEP)
    sls = [slice(hh * C_HEAD_DIM, (hh + 1) * C_HEAD_DIM) for hh in hh_r]
    lgs = [lg_ref[pl.program_id(1) * C_HEADS_PER_STEP + hh] for hh in hh_r]
    qrs = [q2[:, sl] * cos + _rot_pairs(q2[:, sl]) * sin for sl in sls]
    krs = [(k2[:, sl] * cos + _rot_pairs(k2[:, sl]) * sin) * (C_HEAD_DIM ** -0.5) for sl in sls]
    qbs = [qr.astype(BF16) for qr in qrs]
    vbs = [v2[:, sl].astype(BF16) for sl in sls]
    scs = [lax.dot_general(qb, kr.astype(BF16), dn, preferred_element_type=F32)
           * jnp.where(diff >= 0, jnp.exp(lg * jnp.maximum(diff, 0.0)), 0.0) for qb, kr, lg in zip(qbs, krs, lgs)]
    r_olds = [r_scr[hh] for hh in hh_r]
    kz_ts = [(kr * jnp.exp(lg * (chunk - 1.0 - idx))).T.astype(BF16) for kr, lg in zip(krs, lgs)]
    os_ = [jnp.dot(sc.astype(BF16), vb, preferred_element_type=F32)
           + jnp.dot(qb, r_old.astype(BF16), preferred_element_type=F32) * jnp.exp(lg * (idx + 1.0))
           for sc, vb, qb, r_old, lg in zip(scs, vbs, qbs, r_olds, lgs)]
    for hh in hh_r:
        r_scr[hh] = (r_olds[hh] * jnp.exp(lgs[hh] * jnp.full((1, 1), chunk, F32))
                     + jnp.dot(kz_ts[hh], vbs[hh], preferred_element_type=F32))
    us = []
    for hh in hh_r:
        o = os_[hh][0:tb]
        y = o * lax.rsqrt(jnp.mean(o * o, axis=-1, keepdims=True) + C_NORM_EPS)
        us.append(y * _silu(g_ref[0][:, sls[hh]]))
    u_ref[0] = jnp.concatenate(us, axis=-1).astype(BF16)

    @pl.when(c == pl.num_programs(2) - 1)
    def _():
        rout_ref[0] = r_scr[...]


def retention(h3, cos, sin, r0):
    bsz, t, _ = h3.shape
    tb = min(t, C_CHUNK)
    nc = t // tb
    chunk = float(tb)
    lg = jnp.log(1.0 - 2.0 ** (-5.0 - jnp.arange(C_HEADS, dtype=F32)))
    hps = C_HEADS_PER_STEP
    width = hps * C_HEAD_DIM
    nq = C_WIDTH // width

    def col_spec(off):
        return pl.BlockSpec((1, tb, width), lambda b, hd, c: (b, c, off + hd))

    tab_spec = pl.BlockSpec((C_CHUNK, C_HEAD_DIM), lambda b, hd, c: (c, 0))
    st_spec = pl.BlockSpec((1, hps, C_HEAD_DIM, C_HEAD_DIM), lambda b, hd, c: (b, hd, 0, 0))
    scratch = [pltpu.VMEM((hps, C_HEAD_DIM, C_HEAD_DIM), F32)]
    if tb != C_CHUNK:
        scratch += [pltpu.VMEM((C_CHUNK, width), F32)] * 3
    return pl.pallas_call(
        functools.partial(_retention_kernel, tb, chunk),
        grid=(bsz, C_HEADS // hps, nc),
        in_specs=[pl.BlockSpec(memory_space=pltpu.SMEM),
                  col_spec(0), col_spec(nq), col_spec(2 * nq), col_spec(3 * nq),
                  tab_spec, tab_spec, st_spec],
        out_specs=[pl.BlockSpec((1, tb, width), lambda b, hd, c: (b, c, hd)), st_spec],
        out_shape=[jax.ShapeDtypeStruct((bsz, t, C_WIDTH), BF16), jax.ShapeDtypeStruct(r0.shape, F32)],
        scratch_shapes=scratch,
        compiler_params=_cparams(("parallel", "parallel", "arbitrary")),
        name="retention",
    )(lg, h3, h3, h3, h3, cos, sin, r0)


def _even_layer(x, x_bf, mkv, dwa_bufs, s0, shift0, w_in_bf, w_out_bf, e, ln_g, ln_b, rw, head_ones):
    bsz, t, d = x.shape
    h = matmul(x_bf.reshape(bsz * t, d), w_in_bf, e)
    h3 = h.reshape(bsz, t, NU * LANES)
    mu, w0, wup, a0, aup, k_k, k_a, r_k, lnx_g, lnx_b = rw
    r, dcy, k2, kk, bb, vt, bonus = rwkv_prep(h3, shift0, (mu, w0, wup, a0, aup, k_k, k_a, r_k), head_ones)
    yt, s_new = rwkv_scan((r, dcy, k2, kk, bb), vt, s0)
    if dwa_bufs is None:
        dwa = [dwa_prompt(h3, g) for g in range(B_GROUPS)]
    else:
        dwa = dwa_decode(h3, dwa_bufs, e)
    u_a, u_b = even_post(yt, bonus, h3, lnx_g, lnx_b, head_ones, dwa)
    u_m = mem_attn(h3, mkv[0], mkv[1], U_QM_EVEN, U_GM_EVEN)
    x2, x2_bf = outproj_ln([u_a.reshape(bsz * t, -1), u_b.reshape(bsz * t, -1), u_m.reshape(bsz * t, -1)],
                           w_out_bf, e, x.reshape(bsz * t, d), ln_g, ln_b)
    rows = []
    for g, (win, _) in enumerate(B_CONFIGS):
        keep = t if dwa_bufs is not None else min(win, t)
        kg = h3[:, t - keep:, U_KB * LANES + g * B_OUT:U_KB * LANES + (g + 1) * B_OUT]
        vg = h3[:, t - keep:, U_VB * LANES + g * B_OUT:U_VB * LANES + (g + 1) * B_OUT]
        rows.append(jnp.stack([kg.reshape(bsz, keep, B_HEADS_PER_GROUP, B_HEAD_DIM),
                               vg.reshape(bsz, keep, B_HEADS_PER_GROUP, B_HEAD_DIM)], axis=2))
    return x2.reshape(bsz, t, d), x2_bf.reshape(bsz, t, d), s_new, h3[:, t - 1, :A_SHIFT_W], rows


def _odd_layer(x, x_bf, mkv, r0, tabs, w_in_bf, w_out_bf, o, ln_g, ln_b):
    bsz, t, d = x.shape
    h = matmul(x_bf.reshape(bsz * t, d), w_in_bf, o)
    h3 = h.reshape(bsz, t, NU * LANES)
    u_c, r_new = retention(h3, tabs[0], tabs[1], r0)
    u_m = mem_attn(h3, mkv[0], mkv[1], U_QM_ODD, U_GM_ODD)
    x2, x2_bf = outproj_ln([u_c.reshape(bsz * t, -1), u_m.reshape(bsz * t, -1)],
                           w_out_bf, o, x.reshape(bsz * t, d), ln_g, ln_b)
    return x2.reshape(bsz, t, d), x2_bf.reshape(bsz, t, d), r_new


def kernel(x_prompt, x_sample, state_rwkv, state_rwkv_shift, cache_dwa_g0, cache_dwa_g1, cache_dwa_g2, state_ret, cache_mem_kv, mem_prompt, w_in_even, w_out_even, w_in_odd, w_out_odd, w_mem_kv, ln_g, ln_b, rwkv_mu, rwkv_w0, rwkv_w_up, rwkv_a0, rwkv_a_up, rwkv_k_k, rwkv_k_a, rwkv_r_k, rwkv_lnx_g, rwkv_lnx_b):
    xp, xs = x_prompt, x_sample
    xp_bf, xs_bf = xp.astype(BF16), xs.astype(BF16)
    bp, tp_len, d = xp.shape
    bs, ts_len, _ = xs.shape
    dwa_cache = (cache_dwa_g0, cache_dwa_g1, cache_dwa_g2)
    head_ones = _block_ones(LANES, A_HEAD_DIM)
    mem_bf = mem_prompt.reshape(bp * M_TOKENS, d).astype(BF16)
    tabs_p = rope_tables(max(tp_len, C_CHUNK), 0)
    tabs_s = rope_tables(max(ts_len, C_CHUNK), PAST_LEN)
    rwkv_p, rwkv_s, shift_p, shift_s, ret_p, ret_s, mem_p = [], [], [], [], [], [], []
    dwa_p = [[] for _ in B_CONFIGS]
    dwa_s = [[] for _ in B_CONFIGS]
    w_in_even_bf = jnp.concatenate(
        [w_in_even.astype(BF16), jnp.zeros(w_in_even.shape[:2] + (EVEN_IN_PAD - EVEN_IN,), BF16)], axis=-1)
    w_out_even_bf = w_out_even.astype(BF16)
    w_in_odd_bf = w_in_odd
    w_out_odd_bf = w_out_odd.astype(BF16)
    w_mem_bf = w_mem_kv
    for l in range(DEPTH):
        mkv_new = matmul(mem_bf, w_mem_bf, l).reshape(bp, M_TOKENS, 2 * M_WIDTH)
        mem_p.append(mkv_new.reshape(bp, M_TOKENS, 2, M_HEADS, M_HEAD_DIM))
        mkv_p = (mkv_new, None)
        mkv_s = (cache_mem_kv, l)
        if l % 2 == 0:
            e = l // 2
            w_in_bf, w_out_bf = w_in_even_bf, w_out_even_bf
            rw = (rwkv_mu[e], rwkv_w0[e], rwkv_w_up[e], rwkv_a0[e], rwkv_a_up[e], rwkv_k_k[e], rwkv_k_a[e],
                  rwkv_r_k[e], rwkv_lnx_g[e], rwkv_lnx_b[e])
            s0 = jnp.zeros((bp, A_HEADS, A_HEAD_DIM, A_HEAD_DIM), F32)
            sh0 = jnp.zeros((bp, A_SHIFT_W), F32)
            xp, xp_bf, st, sh, rows = _even_layer(xp, xp_bf, mkv_p, None, s0, sh0, w_in_bf, w_out_bf, e,
                                                  ln_g[l], ln_b[l], rw, head_ones)
            rwkv_p.append(st)
            shift_p.append(sh)
            for g in range(B_GROUPS):
                dwa_p[g].append(rows[g])
            bufs = dwa_cache
            xs, xs_bf, st, sh, rows = _even_layer(xs, xs_bf, mkv_s, bufs, state_rwkv[e], state_rwkv_shift[e],
                                                  w_in_bf, w_out_bf, e, ln_g[l], ln_b[l], rw, head_ones)
            rwkv_s.append(st)
            shift_s.append(sh)
            for g in range(B_GROUPS):
                dwa_s[g].append(rows[g])
        else:
            o = l // 2
            r0 = jnp.zeros((bp, C_HEADS, C_HEAD_DIM, C_HEAD_DIM), F32)
            xp, xp_bf, st = _odd_layer(xp, xp_bf, mkv_p, r0, tabs_p, w_in_odd_bf, w_out_odd_bf, o, ln_g[l], ln_b[l])
            ret_p.append(st)
            xs, xs_bf, st = _odd_layer(xs, xs_bf, mkv_s, state_ret[o], tabs_s, w_in_odd_bf, w_out_odd_bf, o,
                                       ln_g[l], ln_b[l])
            ret_s.append(st)
    return (xp, xs, jnp.stack(rwkv_p), jnp.stack(rwkv_s), jnp.stack(shift_p), jnp.stack(shift_s),
            jnp.stack(dwa_p[0]), jnp.stack(dwa_s[0]), jnp.stack(dwa_p[1]), jnp.stack(dwa_s[1]),
            jnp.stack(dwa_p[2]), jnp.stack(dwa_s[2]), jnp.stack(ret_p), jnp.stack(ret_s), jnp.stack(mem_p))
```
